```python
import math
import jax, jax.numpy as jnp
from jax import lax
import numpy as np


D_MODEL = 2048
BATCH = 4
SEQ = 8192
DEPTH = 2

HEAD_DIM = 64
GROUP_HEADS = 8
GROUP_WIDTH = GROUP_HEADS * HEAD_DIM
N_GROUPS = 4
MIX_WIDTH = N_GROUPS * GROUP_WIDTH
D_FF = 4 * D_MODEL
Q_BLOCK = 128
EPS = 1e-6
NEG_INF = -1e30
BIG = 1e9

MLA_Q_RANK = 384
MLA_KV_RANK = 128
MLA_NOPE = 64
MLA_ROPE = 32
MLA_V = HEAD_DIM
ROPE_THETA = 10000.0

NSA_KV_HEADS = 2
NSA_KV_WIDTH = NSA_KV_HEADS * HEAD_DIM
NSA_CMP_LEN = 32
NSA_CMP_STRIDE = 16
NSA_SEL_LEN = 64
NSA_TOP_N = 8
NSA_WINDOW = 256

SWA_KV_HEADS = 2
SWA_KV_WIDTH = SWA_KV_HEADS * HEAD_DIM
SWA_WINDOW = 128

REL_BUCKETS = 32
REL_MAX_DIST = 128
REL_HEADS = 2 * GROUP_HEADS

IN_SPLITS = (
    MLA_Q_RANK, MLA_KV_RANK, MLA_ROPE,
    GROUP_WIDTH, GROUP_WIDTH, GROUP_WIDTH, GROUP_HEADS,
    GROUP_WIDTH, NSA_KV_WIDTH, NSA_KV_WIDTH, NSA_KV_WIDTH,
    NSA_KV_WIDTH, NSA_KV_WIDTH, NSA_KV_WIDTH, 3 * GROUP_HEADS,
    GROUP_WIDTH, SWA_KV_WIDTH, SWA_KV_WIDTH,
)
IN_COLS = sum(IN_SPLITS)

kernel_name = 'hybrid_mla_fox_nsa_swa_block'


def rmsnorm(x, g):
    xf = x.astype(jnp.float32)
    y = xf * lax.rsqrt(jnp.mean(xf * xf, axis=-1, keepdims=True) + EPS)
    return (y * g.astype(jnp.float32)).astype(x.dtype)


def split_heads(x, h):
    b, t, _ = x.shape
    return x.reshape(b, t, h, -1).transpose(0, 2, 1, 3)


def merge_heads(x):
    b, h, t, d = x.shape
    return x.transpose(0, 2, 1, 3).reshape(b, t, h * d)


def rope(x, pos):
    d = x.shape[-1]
    inv = ROPE_THETA ** (-jnp.arange(0, d, 2, dtype=jnp.float32) / d)
    ang = pos.astype(jnp.float32)[:, None] * inv[None, :]
    cos, sin = jnp.cos(ang), jnp.sin(ang)
    x1 = x[..., : d // 2].astype(jnp.float32)
    x2 = x[..., d // 2:].astype(jnp.float32)
    return jnp.concatenate([x1 * cos - x2 * sin, x1 * sin + x2 * cos], axis=-1).astype(x.dtype)


def t5_bucket(dist):
    max_exact = REL_BUCKETS // 2
    d = jnp.maximum(dist, 0)
    large = max_exact + (jnp.log(jnp.maximum(d, 1).astype(jnp.float32) / max_exact)
                         / math.log(REL_MAX_DIST / max_exact)
                         * (REL_BUCKETS - max_exact)).astype(jnp.int32)
    large = jnp.minimum(large, REL_BUCKETS - 1)
    return jnp.where(d < max_exact, d, large)


def dense_causal_attention(q, k, v, scale, fcum=None):
    b, h, t, dk = q.shape
    nb = t // Q_BLOCK
    kpos = jnp.arange(t)
    qb = q.reshape(b, h, nb, Q_BLOCK, dk).transpose(2, 0, 1, 3, 4)
    xs = (jnp.arange(nb), qb)
    if fcum is not None:
        xs = xs + (fcum.reshape(b, h, nb, Q_BLOCK).transpose(2, 0, 1, 3),)

    def one_block(args):
        i, qi = args[0], args[1]
        s = jnp.einsum('bhqd,bhkd->bhqk', qi, k, preferred_element_type=jnp.float32) * scale
        if fcum is not None:
            s = s + args[2][..., :, None] - fcum[:, :, None, :]
        qpos = i * Q_BLOCK + jnp.arange(Q_BLOCK)
        s = jnp.where(kpos[None, :] <= qpos[:, None], s, NEG_INF)
        p = jax.nn.softmax(s, axis=-1)
        return jnp.einsum('bhqk,bhkd->bhqd', p.astype(v.dtype), v)

    o = lax.map(one_block, xs)
    return o.transpose(1, 2, 0, 3, 4).reshape(b, h, t, v.shape[-1])


def banded_attention(q, k, v, window, tbl, sinks=None):
    b, g, r, t, d = q.shape
    nb, nw = t // Q_BLOCK, window // Q_BLOCK

    def blocks(a):
        ap = jnp.pad(a, ((0, 0), (0, 0), (window, 0), (0, 0))).reshape(b, g, nb + nw, Q_BLOCK, d)
        return jnp.concatenate([ap[:, :, i:i + nb] for i in range(nw + 1)], axis=3)

    kb, vb = blocks(k), blocks(v)
    qb = q.reshape(b, g, r, nb, Q_BLOCK, d)
    s = jnp.einsum('bgrnqd,bgnkd->bgrnqk', qb, kb, preferred_element_type=jnp.float32) * d ** -0.5
    kw = (nw + 1) * Q_BLOCK
    dist = (jnp.arange(Q_BLOCK)[:, None] + window) - jnp.arange(kw)[None, :]
    key_real = jnp.arange(nb)[:, None, None] * Q_BLOCK - window + jnp.arange(kw)[None, None, :]
    mask = (dist >= 0) & (dist < window) & (key_real >= 0)
    s = s + tbl[:, :, t5_bucket(dist)][:, :, None].astype(jnp.float32)
    s = jnp.where(mask, s, NEG_INF)
    if sinks is not None:
        sink = jnp.broadcast_to(sinks.astype(jnp.float32)[None, :, :, None, None, None], s.shape[:-1] + (1,))
        p = jax.nn.softmax(jnp.concatenate([s, sink], axis=-1), axis=-1)[..., :-1]
    else:
        p = jax.nn.softmax(s, axis=-1)
    o = jnp.einsum('bgrnqk,bgnkd->bgrnqd', p.astype(v.dtype), vb)
    return o.reshape(b, g, r, t, d)


def mla_mixer(c_q, c_kv, k_pe, q_norm, w_uq, kv_norm, w_ukv, pos):
    b, t, _ = c_q.shape
    q = split_heads(rmsnorm(c_q, q_norm) @ w_uq, GROUP_HEADS)
    q = jnp.concatenate([q[..., :MLA_NOPE], rope(q[..., MLA_NOPE:], pos)], axis=-1)
    kv = split_heads(rmsnorm(c_kv, kv_norm) @ w_ukv, GROUP_HEADS)
    k_pe = jnp.broadcast_to(rope(k_pe[:, None], pos), (b, GROUP_HEADS, t, MLA_ROPE))
    k = jnp.concatenate([kv[..., :MLA_NOPE], k_pe], axis=-1)
    v = kv[..., MLA_NOPE:]
    o = dense_causal_attention(q, k, v, (MLA_NOPE + MLA_ROPE) ** -0.5)
    return merge_heads(o)


def fox_mixer(q, k, v, f_logit, b_f):
    log_f = jax.nn.log_sigmoid(f_logit.astype(jnp.float32) + b_f.astype(jnp.float32))
    fcum = jnp.cumsum(log_f, axis=1).transpose(0, 2, 1)
    o = dense_causal_attention(split_heads(q, GROUP_HEADS), split_heads(k, GROUP_HEADS),
                               split_heads(v, GROUP_HEADS), HEAD_DIM ** -0.5, fcum)
    return merge_heads(o)


def nsa_compress(kv, pos_emb, w1, w2):
    b, g, t, d = kv.shape
    nc = (t - NSA_CMP_LEN) // NSA_CMP_STRIDE + 1
    idx = jnp.arange(nc)[:, None] * NSA_CMP_STRIDE + jnp.arange(NSA_CMP_LEN)[None, :]
    blk = kv[:, :, idx] + pos_emb
    flat = blk.reshape(b, g, nc, NSA_CMP_LEN * d)
    return jax.nn.gelu(flat @ w1) @ w2


def nsa_cmp_sel(qg, k_cmp, v_cmp, ks, vs, tbl):
    b, g, r, t, d = qg.shape
    nb = t // Q_BLOCK
    nc = k_cmp.shape[2]
    ns = t // NSA_SEL_LEN
    n_top = min(NSA_TOP_N, ns)
    scale = d ** -0.5
    cmp_end = jnp.arange(nc) * NSA_CMP_STRIDE + NSA_CMP_LEN - 1
    ci = jnp.arange(nc)[:, None]
    sj = jnp.arange(ns)[None, :]
    overlap = ((ci * NSA_CMP_STRIDE + NSA_CMP_LEN - 1 >= sj * NSA_SEL_LEN)
               & (ci * NSA_CMP_STRIDE <= sj * NSA_SEL_LEN + NSA_SEL_LEN - 1)).astype(jnp.float32)
    ks_blk = ks.reshape(b, g, ns, NSA_SEL_LEN, d)
    vs_blk = vs.reshape(b, g, ns, NSA_SEL_LEN, d)
    bi = jnp.arange(b)[:, None, None, None]
    gi = jnp.arange(g)[None, :, None, None]
    g_ix = jnp.arange(g)[None, :, None, None, None]
    r_ix = jnp.arange(r)[None, None, :, None, None]
    jsel = jnp.arange(ns)
    qb = qg.reshape(b, g, r, nb, Q_BLOCK, d).transpose(3, 0, 1, 2, 4, 5)

    def one_block(args):
        n, qi = args
        qpos = n * Q_BLOCK + jnp.arange(Q_BLOCK)
        sc = jnp.einsum('bgrqd,bgcd->bgrqc', qi, k_cmp, preferred_element_type=jnp.float32) * scale
        dist_c = qpos[:, None] - cmp_end[None, :]
        valid_c = dist_c >= 0
        sc = sc + tbl[:, :, t5_bucket(dist_c)].astype(jnp.float32)
        sc = jnp.where(valid_c, sc, NEG_INF)
        pc = jnp.where(valid_c, jax.nn.softmax(sc, axis=-1), 0.0)
        o_cmp = jnp.einsum('bgrqc,bgcd->bgrqd', pc.astype(v_cmp.dtype), v_cmp)
        imp = jnp.einsum('bgrqc,cs->bgqs', pc, overlap)
        cur = (qpos // NSA_SEL_LEN)[:, None]
        forced = (jsel[None, :] == 0) | (jsel[None, :] == cur) | (jsel[None, :] == cur - 1)
        score = jnp.where(jsel[None, :] <= cur, jnp.where(forced, BIG, imp), NEG_INF)
        _, sel = lax.top_k(score, n_top)
        k_sel = ks_blk[bi, gi, sel].reshape(b, g, Q_BLOCK, n_top * NSA_SEL_LEN, d)
        v_sel = vs_blk[bi, gi, sel].reshape(b, g, Q_BLOCK, n_top * NSA_SEL_LEN, d)
        kpos = (sel[..., None] * NSA_SEL_LEN + jnp.arange(NSA_SEL_LEN)).reshape(b, g, Q_BLOCK, n_top * NSA_SEL_LEN)
        dist_s = qpos[None, None, :, None] - kpos
        ss = jnp.einsum('bgrqd,bgqkd->bgrqk', qi, k_sel, preferred_element_type=jnp.float32) * scale
        ss = ss + tbl[g_ix, r_ix, t5_bucket(dist_s)[:, :, None]].astype(jnp.float32)
        ss = jnp.where(dist_s[:, :, None] >= 0, ss, NEG_INF)
        ps = jax.nn.softmax(ss, axis=-1)
        o_sel = jnp.einsum('bgrqk,bgqkd->bgrqd', ps.astype(v_sel.dtype), v_sel)
        return o_cmp, o_sel

    o_cmp, o_sel = lax.map(one_block, (jnp.arange(nb), qb))
    fix = lambda o: o.transpose(1, 2, 3, 0, 4, 5).reshape(b, g, r, t, d)
    return fix(o_cmp), fix(o_sel)


def nsa_mixer(q, kc, vc, ksl, vsl, kwn, vwn, gate, cmp_pos, cmp_w1, cmp_w2, tbl):
    b, t, _ = q.shape
    g, r = NSA_KV_HEADS, GROUP_HEADS // NSA_KV_HEADS
    qg = split_heads(q, GROUP_HEADS).reshape(b, g, r, t, HEAD_DIM)
    kc, vc = split_heads(kc, g), split_heads(vc, g)
    ksl, vsl = split_heads(ksl, g), split_heads(vsl, g)
    kwn, vwn = split_heads(kwn, g), split_heads(vwn, g)
    k_cmp = nsa_compress(kc, cmp_pos[0], cmp_w1[0], cmp_w2[0])
    v_cmp = nsa_compress(vc, cmp_pos[1], cmp_w1[1], cmp_w2[1])
    o_cmp, o_sel = nsa_cmp_sel(qg, k_cmp, v_cmp, ksl, vsl, tbl)
    o_win = banded_attention(qg, kwn, vwn, NSA_WINDOW, tbl)
    gts = jax.nn.sigmoid(gate).reshape(b, t, g, r, 3).transpose(0, 2, 3, 1, 4)
    o = gts[..., 0:1] * o_cmp + gts[..., 1:2] * o_sel + gts[..., 2:3] * o_win
    return merge_heads(o.reshape(b, GROUP_HEADS, t, HEAD_DIM))


def swa_mixer(q, k, v, sinks, tbl):
    b, t, _ = q.shape
    g, r = SWA_KV_HEADS, GROUP_HEADS // SWA_KV_HEADS
    qg = split_heads(q, GROUP_HEADS).reshape(b, g, r, t, HEAD_DIM)
    o = banded_attention(qg, split_heads(k, g), split_heads(v, g), SWA_WINDOW, tbl, sinks.reshape(g, r))
    return merge_heads(o.reshape(b, GROUP_HEADS, t, HEAD_DIM))


def setup_inputs(seed: int = 0) -> dict:
    key = jax.random.key(seed)
    ks = jax.random.split(key, 20)
    nrm = lambda k, shape, fan_in: jax.random.normal(k, shape, jnp.float32) * fan_in ** -0.5
    gain = lambda k, shape: 1.0 + 0.05 * jax.random.normal(k, shape, jnp.float32)
    L = DEPTH
    return {
        'x': jax.random.normal(ks[0], (BATCH, SEQ, D_MODEL), jnp.float32),
        'norm_attn': gain(ks[1], (L, D_MODEL)),
        'w_in': nrm(ks[2], (L, D_MODEL, IN_COLS), D_MODEL),
        'mla_q_norm': gain(ks[3], (L, MLA_Q_RANK)),
        'mla_w_uq': nrm(ks[4], (L, MLA_Q_RANK, GROUP_HEADS * (MLA_NOPE + MLA_ROPE)), MLA_Q_RANK),
        'mla_kv_norm': gain(ks[5], (L, MLA_KV_RANK)),
        'mla_w_ukv': nrm(ks[6], (L, MLA_KV_RANK, GROUP_HEADS * (MLA_NOPE + MLA_V)), MLA_KV_RANK),
        'fox_b_f': 3.0 + 0.5 * jax.random.normal(ks[7], (L, GROUP_HEADS), jnp.float32),
        'nsa_cmp_pos': 0.1 * jax.random.normal(ks[8], (L, 2, NSA_CMP_LEN, HEAD_DIM), jnp.float32),
        'nsa_cmp_w1': nrm(ks[9], (L, 2, NSA_CMP_LEN * HEAD_DIM, HEAD_DIM), NSA_CMP_LEN * HEAD_DIM),
        'nsa_cmp_w2': nrm(ks[10], (L, 2, HEAD_DIM, HEAD_DIM), HEAD_DIM),
        'swa_sinks': 0.5 * jax.random.normal(ks[11], (L, GROUP_HEADS), jnp.float32),
        'group_norm': gain(ks[12], (L, MIX_WIDTH)),
        'w_out': nrm(ks[13], (L, MIX_WIDTH, D_MODEL), MIX_WIDTH),
        'norm_mlp': gain(ks[14], (L, D_MODEL)),
        'w_up': nrm(ks[15], (L, D_MODEL, D_FF), D_MODEL),
        'w_down': nrm(ks[16], (L, D_FF, D_MODEL), D_FF),
        'rel_bias': 0.5 * jax.random.normal(ks[17], (REL_BUCKETS, REL_HEADS), jnp.float32),
        'final_norm': gain(ks[18], (D_MODEL,)),
    }


def reference(x, norm_attn, w_in, mla_q_norm, mla_w_uq, mla_kv_norm, mla_w_ukv, fox_b_f,
              nsa_cmp_pos, nsa_cmp_w1, nsa_cmp_w2, swa_sinks, group_norm, w_out,
              norm_mlp, w_up, w_down, rel_bias, final_norm):
    b, t, _ = x.shape
    pos = jnp.arange(t)
    r_nsa = GROUP_HEADS // NSA_KV_HEADS
    r_swa = GROUP_HEADS // SWA_KV_HEADS
    tbl_nsa = rel_bias[:, :GROUP_HEADS].T.reshape(NSA_KV_HEADS, r_nsa, REL_BUCKETS)
    tbl_swa = rel_bias[:, GROUP_HEADS:].T.reshape(SWA_KV_HEADS, r_swa, REL_BUCKETS)
    offsets = [int(o) for o in np.cumsum(IN_SPLITS)[:-1]]
    h = x
    for l in range(DEPTH):
        u = rmsnorm(h, norm_attn[l])
        p = jnp.split(u @ w_in[l], offsets, axis=-1)
        o_mla = mla_mixer(p[0], p[1], p[2], mla_q_norm[l], mla_w_uq[l], mla_kv_norm[l], mla_w_ukv[l], pos)
        o_fox = fox_mixer(p[3], p[4], p[5], p[6], fox_b_f[l])
        o_nsa = nsa_mixer(p[7], p[8], p[9], p[10], p[11], p[12], p[13], p[14],
                          nsa_cmp_pos[l], nsa_cmp_w1[l], nsa_cmp_w2[l], tbl_nsa)
        o_swa = swa_mixer(p[15], p[16], p[17], swa_sinks[l], tbl_swa)
        o = jnp.concatenate([o_mla, o_fox, o_nsa, o_swa], axis=-1).reshape(b, t, N_GROUPS, GROUP_WIDTH)
        o = rmsnorm(o, group_norm[l].reshape(N_GROUPS, GROUP_WIDTH)).reshape(b, t, MIX_WIDTH)
        h = h + o @ w_out[l]
        m = rmsnorm(h, norm_mlp[l]) @ w_up[l]
        h = h + jnp.square(jax.nn.relu(m)) @ w_down[l]
    return rmsnorm(h, final_norm)
```

```python
import functools
import math

import numpy as np
import jax
import jax.numpy as jnp
from jax import lax
from jax.experimental import pallas as pl
from jax.experimental.pallas import tpu as pltpu

HEAD_DIM = 64
GROUP_HEADS = 8
GROUP_WIDTH = GROUP_HEADS * HEAD_DIM
N_GROUPS = 4
Q_BLOCK = 128
EPS = 1e-6
NEG_INF = -1e30
BIG = 1e9

MLA_Q_RANK = 384
MLA_KV_RANK = 128
MLA_NOPE = 64
MLA_ROPE = 32
ROPE_THETA = 10000.0

NSA_KV_HEADS = 2
NSA_REP = GROUP_HEADS // NSA_KV_HEADS
NSA_CMP_LEN = 32
NSA_CMP_STRIDE = 16
NSA_SEL_LEN = 64
NSA_TOP_N = 8
NSA_WINDOW = 256
SWA_WINDOW = 128

REL_BUCKETS = 32
REL_MAX_DIST = 128

V7X_VMEM_BYTES = 64 * 2**20
VMEM_LIMIT = (V7X_VMEM_BYTES * 7) // 8
LANES = 128
CD = jnp.bfloat16
F32 = jnp.float32
MASK_NEG = -(2.0 ** 80)
M_INIT = -(2.0 ** 100)
LOG2E = math.log2(math.e)
_TRANS_B = (((1,), (1,)), ((), ()))


def _cparams(*sem):
    return pltpu.CompilerParams(dimension_semantics=sem, vmem_limit_bytes=VMEM_LIMIT)


def _split_parts(x, n):
    parts, r = [], x
    for _ in range(n):
        p = r.astype(CD)
        parts.append(p)
        r = r - p.astype(F32)
    return parts


def _rms(x, g):
    return x * lax.rsqrt(jnp.mean(x * x, axis=-1, keepdims=True) + EPS) * g


def _norm_matmul_kernel(x_ref, g_ref, w_ref, o_ref, u_ref):
    @pl.when(pl.program_id(1) == 0)
    def _():
        u_ref[...] = _rms(x_ref[...], g_ref[...]).astype(u_ref.dtype)

    o_ref[...] = jnp.dot(u_ref[...], w_ref[...], preferred_element_type=F32).astype(o_ref.dtype)


def _norm_matmul(x, g, w, out_dtype, tm, tn):
    n, d = x.shape
    nc = w.shape[1]
    return pl.pallas_call(
        _norm_matmul_kernel,
        out_shape=jax.ShapeDtypeStruct((n, nc), out_dtype),
        grid=(n // tm, nc // tn),
        in_specs=[
            pl.BlockSpec((tm, d), lambda i, j: (i, 0)),
            pl.BlockSpec((1, d), lambda i, j: (0, 0)),
            pl.BlockSpec((d, tn), lambda i, j: (0, j)),
        ],
        out_specs=pl.BlockSpec((tm, tn), lambda i, j: (i, j)),
        scratch_shapes=[pltpu.VMEM((tm, d), CD)],
        compiler_params=_cparams("parallel", "arbitrary"),
        name="norm_matmul",
    )(x, g.reshape(1, d), w)


def _mla_prep_kernel(sm_ref, qn_ref, kvn_ref, wq_ref, wk_ref, wv_ref, cq_ref, sq_ref, ck_ref,
                     q_ref, k_ref, v_ref):
    sm = sm_ref[...]
    nq = _rms(sm[:, :MLA_Q_RANK], qn_ref[...]).astype(CD)
    nkv = _rms(sm[:, MLA_Q_RANK:MLA_Q_RANK + MLA_KV_RANK], kvn_ref[...]).astype(CD)
    blk_a = sm[:, 512:640]
    blk_b = sm[:, 640:768]
    cosq, sinq, cosk = cq_ref[...], sq_ref[...], ck_ref[...]
    hw = GROUP_HEADS * LANES
    qq = jnp.dot(nq, wq_ref[...], preferred_element_type=F32)
    kk = jnp.dot(nkv, wk_ref[...], preferred_element_type=F32)
    kpe = blk_a * cosk + blk_b * sinq
    for h in range(GROUP_HEADS):
        sl = slice(h * LANES, (h + 1) * LANES)
        q_ref[:, sl] = (qq[:, sl] * cosq + qq[:, hw + h * LANES:hw + (h + 1) * LANES] * sinq).astype(q_ref.dtype)
        k_ref[:, sl] = (kk[:, sl] + kpe).astype(k_ref.dtype)
    v_ref[...] = jnp.dot(nkv, wv_ref[...], preferred_element_type=F32).astype(v_ref.dtype)


def _mla_prep(small, qn, kvn, wq, wk, wv, cosq, sinq, cosk, t, tm):
    n = small.shape[0]
    tb = t // tm
    hw = GROUP_HEADS * LANES
    full = lambda a: pl.BlockSpec(a.shape, lambda i: (0,) * a.ndim)
    tab = pl.BlockSpec((tm, LANES), lambda i: (i % tb, 0))
    qn = qn.reshape(1, -1)
    kvn = kvn.reshape(1, -1)
    return pl.pallas_call(
        _mla_prep_kernel,
        out_shape=(jax.ShapeDtypeStruct((n, hw), CD), jax.ShapeDtypeStruct((n, hw), CD),
                   jax.ShapeDtypeStruct((n, GROUP_WIDTH), CD)),
        grid=(n // tm,),
        in_specs=[pl.BlockSpec((tm, small.shape[1]), lambda i: (i, 0)), full(qn), full(kvn),
                  full(wq), full(wk), full(wv), tab, tab, tab],
        out_specs=(pl.BlockSpec((tm, hw), lambda i: (i, 0)), pl.BlockSpec((tm, hw), lambda i: (i, 0)),
                   pl.BlockSpec((tm, GROUP_WIDTH), lambda i: (i, 0))),
        compiler_params=_cparams("parallel"),
        name="mla_prep",
    )(small, qn, kvn, wq, wk, wv, cosq, sinq, cosk)


def _fox_prep_kernel(fl_ref, bf_ref, k_ref, efq_ref, ekf_ref, kaug_ref, fq_ref, carry_ref, *, tc):
    @pl.when(pl.program_id(1) == 0)
    def _():
        carry_ref[...] = jnp.zeros_like(carry_ref)

    x = fl_ref[...] + bf_ref[...]
    logf = -(jnp.maximum(-x, 0.0) + jnp.log1p(jnp.exp(-jnp.abs(x))))
    lane = lax.broadcasted_iota(jnp.int32, logf.shape, 1)
    logf = jnp.where(lane < GROUP_HEADS, logf, 0.0)
    row = lax.broadcasted_iota(jnp.int32, (tc, tc), 0)
    col = lax.broadcasted_iota(jnp.int32, (tc, tc), 1)
    tri = jnp.where(row >= col, 1.0, 0.0).astype(CD)
    cs = jnp.zeros(logf.shape, F32)
    for part in _split_parts(logf, 3):
        cs = cs + jnp.dot(tri, part, preferred_element_type=F32)
    fc = cs + carry_ref[...]
    carry_ref[...] = fc[tc - 1:tc, :]
    parts = _split_parts(fc, 3)
    fq = jnp.zeros(fq_ref.shape, F32)
    kf = jnp.zeros(fq_ref.shape, F32)
    for i, part in enumerate(parts):
        fq = fq + jnp.dot(part, efq_ref[...], preferred_element_type=F32)
        kf = kf + jnp.dot(part, ekf_ref[i], preferred_element_type=F32)
    fq_ref[...] = fq
    k = k_ref[...]
    for p in range(GROUP_HEADS // 2):
        kaug_ref[:, 2 * p * LANES:(2 * p + 1) * LANES] = k[:, p * LANES:(p + 1) * LANES]
        kaug_ref[:, (2 * p + 1) * LANES:(2 * p + 2) * LANES] = kf[:, p * LANES:(p + 1) * LANES].astype(kaug_ref.dtype)


def _fox_prep(small3, b_f, main3, k_col, tc):
    b, t, _ = small3.shape
    pairs = GROUP_HEADS // 2
    efq = np.zeros((LANES, pairs * LANES), np.float32)
    ekf = np.zeros((3, LANES, pairs * LANES), np.float32)
    for h in range(GROUP_HEADS):
        p, a = divmod(h, 2)
        efq[h, p * LANES + a] = 1.0
        for i in range(3):
            ekf[i, h, p * LANES + 3 * a + i] = -1.0
    bf = jnp.zeros((1, LANES), F32).at[0, :GROUP_HEADS].set(b_f.astype(F32))
    return pl.pallas_call(
        functools.partial(_fox_prep_kernel, tc=tc),
        out_shape=(jax.ShapeDtypeStruct((b, t, 2 * GROUP_WIDTH), CD),
                   jax.ShapeDtypeStruct((b, t, GROUP_WIDTH), F32)),
        grid=(b, t // tc),
        in_specs=[pl.BlockSpec((None, tc, LANES), lambda i, j: (i, j, 4)),
                  pl.BlockSpec((1, LANES), lambda i, j: (0, 0)),
                  pl.BlockSpec((None, tc, GROUP_WIDTH), lambda i, j: (i, j, k_col)),
                  pl.BlockSpec(efq.shape, lambda i, j: (0, 0)),
                  pl.BlockSpec(ekf.shape, lambda i, j: (0, 0, 0))],
        out_specs=(pl.BlockSpec((None, tc, 2 * GROUP_WIDTH), lambda i, j: (i, j, 0)),
                   pl.BlockSpec((None, tc, GROUP_WIDTH), lambda i, j: (i, j, 0))),
        scratch_shapes=[pltpu.VMEM((1, LANES), F32)],
        compiler_params=_cparams("arbitrary", "arbitrary"),
        name="fox_prep",
    )(small3, bf, main3, jnp.asarray(efq, CD), jnp.asarray(ekf, CD))


def _dense_attn_kernel(*refs, fox, tq, c):
    if fox:
        q_ref, k_ref, v_ref, fq_ref, o_ref = refs
    else:
        q_ref, k_ref, v_ref, o_ref = refs
    qi = pl.program_id(2)
    q = q_ref[...]
    lane = lax.broadcasted_iota(jnp.int32, (tq, LANES), 1)
    if fox:
        qs = (jnp.concatenate([q * _lane_mask(tq, 0, HEAD_DIM, q.dtype), _lane_mask(tq, 0, 3, q.dtype)], axis=1),
              jnp.concatenate([q * _lane_mask(tq, HEAD_DIM, LANES, q.dtype), _lane_mask(tq, 3, 6, q.dtype)], axis=1))
        fq = fq_ref[...]
        fqs = (fq[:, 0:1], fq[:, 1:2])
    else:
        qs = (q[:, :LANES], q[:, LANES:])
        fqs = (None, None)
    row = lax.broadcasted_iota(jnp.int32, (tq, tq), 0)
    col = lax.broadcasted_iota(jnp.int32, (tq, tq), 1)

    def shift_of(m, f):
        return m if f is None else (m + f) - f

    def step(j, carry, masked):
        off = pl.multiple_of(j * tq, tq)
        k = k_ref[pl.ds(off, tq), :]
        v = v_ref[pl.ds(off, tq), :]
        ks = (k, k) if fox else (k[:, :LANES], k[:, LANES:])
        out = []
        for h in range(2):
            m, l, acc = carry[h]
            s = lax.dot_general(qs[h], ks[h], _TRANS_B, preferred_element_type=F32)
            if masked:
                s = jnp.where(col <= row, s, NEG_INF)
            m_new = jnp.maximum(m, jnp.max(s, axis=1, keepdims=True))
            sh_new = shift_of(m_new, fqs[h])
            alpha = jnp.exp2((shift_of(m, fqs[h]) - sh_new) * c)
            p = jnp.exp2((s - sh_new) * c)
            l = alpha * l + jnp.sum(p, axis=1, keepdims=True)
            acc = alpha * acc + jnp.dot(p.astype(v.dtype), v, preferred_element_type=F32)
            out.append((m_new, l, acc))
        return tuple(out)

    init = tuple((jnp.full((tq, 1), M_INIT, F32), jnp.zeros((tq, 1), F32), jnp.zeros((tq, LANES), F32))
                 for _ in range(2))
    carry = lax.fori_loop(0, qi, lambda j, cr: step(j, cr, False), init)
    (_, la, acca), (_, lb, accb) = step(qi, carry, True)
    o_ref[...] = jnp.where(lane < HEAD_DIM, acca * (1.0 / la), accb * (1.0 / lb)).astype(o_ref.dtype)


def _dense_attn(q3, k3, v3, fq3, *, fox, q_col, v_col, scale, tq):
    b, t, _ = q3.shape
    pairs = GROUP_HEADS // 2
    qw = LANES if fox else 2 * LANES
    in_specs = [pl.BlockSpec((None, tq, qw), lambda i, p, j: (i, j, q_col + p)),
                pl.BlockSpec((None, t, 2 * LANES), lambda i, p, j: (i, 0, p)),
                pl.BlockSpec((None, t, LANES), lambda i, p, j: (i, 0, v_col + p))]
    args = [q3, k3, v3]
    if fox:
        in_specs.append(pl.BlockSpec((None, tq, LANES), lambda i, p, j: (i, j, p)))
        args.append(fq3)
    return pl.pallas_call(
        functools.partial(_dense_attn_kernel, fox=fox, tq=tq, c=scale * LOG2E),
        out_shape=jax.ShapeDtypeStruct((b, t, GROUP_WIDTH), F32),
        grid=(b, pairs, t // tq),
        in_specs=in_specs,
        out_specs=pl.BlockSpec((None, tq, LANES), lambda i, p, j: (i, j, p)),
        compiler_params=_cparams("parallel", "parallel", "arbitrary"),
        name="fox_attn" if fox else "mla_attn",
    )(*args)


def _gelu_tanh(x):
    return 0.5 * x * (1.0 + jnp.tanh(math.sqrt(2.0 / math.pi) * (x + 0.044715 * (x * x * x))))


def _compress_kernel(r_ref, w1a_ref, w1b_ref, pa_ref, pb_ref, w2_ref, o_ref):
    r = r_ref[...]
    nr = r.shape[0]
    ya = jnp.dot(r, w1a_ref[...], preferred_element_type=F32)
    yb = jnp.dot(r, w1b_ref[...], preferred_element_type=F32)
    pc = (jnp.dot(pa_ref[...], w1a_ref[...], preferred_element_type=F32)
          + jnp.dot(pb_ref[...], w1b_ref[...], preferred_element_type=F32))[0:1, :]
    pre = ya + pltpu.roll(yb, nr - 1, 0) + pc
    h1 = _gelu_tanh(pre).astype(CD)
    rowi = lax.broadcasted_iota(jnp.int32, (nr, LANES), 0)
    for g in range(NSA_KV_HEADS):
        y = jnp.dot(h1, w2_ref[g], preferred_element_type=F32)
        o_ref[g] = jnp.where(rowi < nr - 1, y, 0.0).astype(o_ref.dtype)


def _nsa_compress(kv16, cmp_pos, cmp_w1, cmp_w2):
    b, nr, kw = kv16.shape
    half = NSA_CMP_LEN // 2
    eye2 = jnp.eye(2, dtype=F32)
    w1r = cmp_w1.astype(F32).reshape(2, NSA_CMP_LEN, HEAD_DIM, HEAD_DIM)
    expand = lambda w: jnp.einsum('klde,kw,gh->klwgdhe', w, eye2, eye2).reshape(2, kw, LANES).astype(CD)
    w1a, w1b = expand(w1r[:, :half]), expand(w1r[:, half:])
    posr = cmp_pos.astype(F32)
    tile = lambda p: jnp.broadcast_to(p[:, None, :, None, None, :], (2, 8, half, 2, 2, HEAD_DIM)).reshape(2, 8, kw).astype(CD)
    pa, pb = tile(posr[:, :half]), tile(posr[:, half:])
    w2 = cmp_w2.astype(F32)
    w2d = jnp.zeros((2, NSA_KV_HEADS, LANES, LANES), F32)
    for g in range(NSA_KV_HEADS):
        blk = jnp.concatenate([w2, w2], axis=2)
        w2d = w2d.at[:, g, g * HEAD_DIM:(g + 1) * HEAD_DIM, :].set(blk)
    w2d = w2d.astype(CD)
    return pl.pallas_call(
        _compress_kernel,
        out_shape=jax.ShapeDtypeStruct((b, 2, NSA_KV_HEADS, nr, LANES), CD),
        grid=(b, 2),
        in_specs=[pl.BlockSpec((None, nr, kw), lambda i, k: (i, 0, 0)),
                  pl.BlockSpec((None, kw, LANES), lambda i, k: (k, 0, 0)),
                  pl.BlockSpec((None, kw, LANES), lambda i, k: (k, 0, 0)),
                  pl.BlockSpec((None, 8, kw), lambda i, k: (k, 0, 0)),
                  pl.BlockSpec((None, 8, kw), lambda i, k: (k, 0, 0)),
                  pl.BlockSpec((None, NSA_KV_HEADS, LANES, LANES), lambda i, k: (k, 0, 0, 0))],
        out_specs=pl.BlockSpec((None, None, NSA_KV_HEADS, nr, LANES), lambda i, k: (i, k, 0, 0, 0)),
        compiler_params=_cparams("parallel", "arbitrary"),
        name="nsa_compress",
    )(kv16, w1a, w1b, pa, pb, w2d)


def _lane_mask(rows, lo, hi, dtype):
    lane = lax.broadcasted_iota(jnp.int32, (rows, LANES), 1)
    return jnp.where((lane >= lo) & (lane < hi), 1.0, 0.0).astype(dtype)


def _stack_heads_native(q):
    lo = _lane_mask(Q_BLOCK, 0, HEAD_DIM, q.dtype)
    hi = _lane_mask(Q_BLOCK, HEAD_DIM, LANES, q.dtype)
    c0, c1 = q[:, :LANES], q[:, LANES:]
    return jnp.concatenate([c0 * lo, c0 * hi, c1 * lo, c1 * hi], axis=0)


def _unstack_heads(o):
    lane = lax.broadcasted_iota(jnp.int32, (Q_BLOCK, LANES), 1)
    lo = lane < HEAD_DIM
    return jnp.concatenate([jnp.where(lo, o[0:128], o[128:256]), jnp.where(lo, o[256:384], o[384:512])], axis=1)


def _dup_matrix(g):
    i = lax.broadcasted_iota(jnp.int32, (LANES, LANES), 0)
    j = lax.broadcasted_iota(jnp.int32, (LANES, LANES), 1)
    return jnp.where(i == g * HEAD_DIM + (j & (HEAD_DIM - 1)), 1.0, 0.0).astype(CD)


def _fill_rows(dst_ref, row0, src_ref, mat, chunk=1024):
    n = src_ref.shape[0]

    def body(i, _):
        off = pl.multiple_of(i * chunk, chunk)
        dst_ref[pl.ds(row0 + off, chunk), :] = jnp.dot(
            src_ref[pl.ds(off, chunk), :], mat, preferred_element_type=F32).astype(dst_ref.dtype)
        return 0

    lax.fori_loop(0, n // chunk, body, 0)


def _cmp_select_kernel(q_ref, kc_ref, vc_ref, pq_ref, ov_ref, o_ref, mf_ref):
    n = pl.program_id(2)
    nc = kc_ref.shape[0]
    qaug = jnp.concatenate([_stack_heads_native(q_ref[...]), pq_ref[...]], axis=1)
    ci = lax.broadcasted_iota(jnp.int32, (nc, LANES), 0)
    f = lax.broadcasted_iota(jnp.int32, (nc, LANES), 1)
    uc = ci - 8 * n + 9
    one_if = lambda cond: jnp.where(cond, 1.0, 0.0)
    feat = jnp.where(f < 32, one_if(uc == (f & 15)),
                     jnp.where(f < 34, one_if(uc < 0), one_if((f == 34) & (uc > 15)))).astype(CD)
    kaug = jnp.concatenate([kc_ref[...], feat], axis=1)
    s = lax.dot_general(qaug, kaug, _TRANS_B, preferred_element_type=F32)
    m = jnp.max(s, axis=1, keepdims=True)
    p = jnp.exp(s - m)
    l = jnp.sum(p, axis=1, keepdims=True)
    pc = p * jnp.where(m > 0.5 * MASK_NEG, 1.0 / l, 0.0)
    o = jnp.dot(pc.astype(CD), vc_ref[...], preferred_element_type=F32)
    o_ref[...] = _unstack_heads(o)
    pcs = pc[0:128] + pc[128:256] + pc[256:384] + pc[384:512]
    imp = jnp.dot(pcs.astype(CD), ov_ref[...], preferred_element_type=F32)
    j = lax.broadcasted_iota(jnp.int32, (Q_BLOCK, LANES), 1)
    i = lax.broadcasted_iota(jnp.int32, (Q_BLOCK, LANES), 0)
    cur = 2 * n + jnp.where(i >= NSA_SEL_LEN, 1, 0)
    forced = (j == 0) | (j == cur) | (j == cur - 1)
    score = jnp.where(j <= cur, jnp.where(forced, BIG, imp), NEG_INF)
    sel = jnp.zeros((Q_BLOCK, LANES), F32)
    jf = j.astype(F32)
    for _ in range(NSA_TOP_N):
        mx = jnp.max(score, axis=1, keepdims=True)
        idx = jnp.min(jnp.where(score == mx, jf, float(LANES)), axis=1, keepdims=True)
        pick = jf == idx
        sel = jnp.where(pick, 1.0, sel)
        score = jnp.where(pick, -3.0e38, score)
    mf_ref[...] = jnp.where(sel > 0.5, 0.0, MASK_NEG).astype(mf_ref.dtype)


def _cmp_select(main3, q_col, cmp, pq, overlap):
    b, t, _ = main3.shape
    nc = cmp.shape[3]
    return pl.pallas_call(
        _cmp_select_kernel,
        out_shape=(jax.ShapeDtypeStruct((b, t, GROUP_WIDTH), F32),
                   jax.ShapeDtypeStruct((b, t, NSA_KV_HEADS * LANES), CD)),
        grid=(b, NSA_KV_HEADS, t // Q_BLOCK),
        in_specs=[pl.BlockSpec((None, Q_BLOCK, 2 * LANES), lambda i, g, n: (i, n, q_col + g)),
                  pl.BlockSpec((None, None, None, nc, LANES), lambda i, g, n: (i, 0, g, 0, 0)),
                  pl.BlockSpec((None, None, None, nc, LANES), lambda i, g, n: (i, 1, g, 0, 0)),
                  pl.BlockSpec((None, NSA_REP * Q_BLOCK, LANES), lambda i, g, n: (g, 0, 0)),
                  pl.BlockSpec(overlap.shape, lambda i, g, n: (0, 0))],
        out_specs=(pl.BlockSpec((None, Q_BLOCK, 2 * LANES), lambda i, g, n: (i, n, g)),
                   pl.BlockSpec((None, Q_BLOCK, LANES), lambda i, g, n: (i, n, g))),
        compiler_params=_cparams("parallel", "parallel", "arbitrary"),
        name="nsa_cmp_select",
    )(main3, cmp, cmp, pq, overlap)


def _banded_kernel(*refs, window, sinks):
    if sinks:
        q_ref, k_ref, v_ref, bm_ref, sk_ref, o_ref, kp_ref, vp_ref = refs
    else:
        q_ref, k_ref, v_ref, bm_ref, o_ref, kp_ref, vp_ref = refs
    g = pl.program_id(1)
    n = pl.program_id(2)
    kw = window + Q_BLOCK

    @pl.when(n == 0)
    def _():
        dup = _dup_matrix(g)
        kp_ref[0:window, :] = jnp.zeros((window, LANES), kp_ref.dtype)
        vp_ref[0:window, :] = jnp.zeros((window, LANES), vp_ref.dtype)
        _fill_rows(kp_ref, window, k_ref, dup)
        _fill_rows(vp_ref, window, v_ref, dup)

    start = pl.multiple_of(n * Q_BLOCK, Q_BLOCK)
    ks = kp_ref[pl.ds(start, kw), :]
    vs = vp_ref[pl.ds(start, kw), :]
    s = lax.dot_general(_stack_heads_native(q_ref[...]), ks, _TRANS_B, preferred_element_type=F32)
    s = s + bm_ref[...]
    kpos = start - window + lax.broadcasted_iota(jnp.int32, s.shape, 1)
    s = jnp.where(kpos >= 0, s, NEG_INF)
    m = jnp.max(s, axis=1, keepdims=True)
    if sinks:
        sink = sk_ref[:, 0:1]
        m = jnp.maximum(m, sink)
    p = jnp.exp(s - m)
    l = jnp.sum(p, axis=1, keepdims=True)
    if sinks:
        l = l + jnp.exp(sink - m)
    o = jnp.dot(p.astype(CD), vs, preferred_element_type=F32) * (1.0 / l)
    o_ref[...] = _unstack_heads(o)


def _banded_attn(main3, q_col, k_col, v_col, biasmask, sink_rows, window):
    b, t, _ = main3.shape
    kw = window + Q_BLOCK
    sinks = sink_rows is not None
    in_specs = [pl.BlockSpec((None, Q_BLOCK, 2 * LANES), lambda i, g, n: (i, n, q_col + g)),
                pl.BlockSpec((None, t, LANES), lambda i, g, n: (i, 0, k_col)),
                pl.BlockSpec((None, t, LANES), lambda i, g, n: (i, 0, v_col)),
                pl.BlockSpec((None, NSA_REP * Q_BLOCK, kw), lambda i, g, n: (g, 0, 0))]
    args = [main3, main3, main3, biasmask]
    if sinks:
        in_specs.append(pl.BlockSpec((None, NSA_REP * Q_BLOCK, LANES), lambda i, g, n: (g, 0, 0)))
        args.append(sink_rows)
    return pl.pallas_call(
        functools.partial(_banded_kernel, window=window, sinks=sinks),
        out_shape=jax.ShapeDtypeStruct((b, t, GROUP_WIDTH), F32),
        grid=(b, NSA_KV_HEADS, t // Q_BLOCK),
        in_specs=in_specs,
        out_specs=pl.BlockSpec((None, Q_BLOCK, 2 * LANES), lambda i, g, n: (i, n, g)),
        scratch_shapes=[pltpu.VMEM((window + t, LANES), CD), pltpu.VMEM((window + t, LANES), CD)],
        compiler_params=_cparams("arbitrary", "arbitrary", "arbitrary"),
        name="swa_attn" if sinks else "nsa_win_attn",
    )(*args)


def _sel_kernel(q_ref, k_ref, v_ref, mf_ref, fq_ref, nb_ref, oc_ref, ow_ref, gt_ref, o_ref, ka_ref, vd_ref):
    g = pl.program_id(1)
    n = pl.program_id(2)
    t = k_ref.shape[0]

    @pl.when(n == 0)
    def _():
        i = lax.broadcasted_iota(jnp.int32, (LANES, LANES), 0)
        j = lax.broadcasted_iota(jnp.int32, (LANES, LANES), 1)
        pick = jnp.where((j < HEAD_DIM) & (i == g * HEAD_DIM + j), 1.0, 0.0).astype(CD)
        chunk = 1024

        def body(c, _):
            off = pl.multiple_of(c * chunk, chunk)
            kk = jnp.dot(k_ref[pl.ds(off, chunk), :], pick, preferred_element_type=F32)
            ln = lax.broadcasted_iota(jnp.int32, (chunk, LANES), 1)
            kk = jnp.where((ln == HEAD_DIM) | (ln == HEAD_DIM + 1), 1.0, kk)
            ka_ref[pl.ds(off, chunk), 0:LANES] = kk.astype(ka_ref.dtype)
            key = off + lax.broadcasted_iota(jnp.int32, (chunk, LANES), 0)
            ka_ref[pl.ds(off, chunk), LANES:2 * LANES] = jnp.where(
                (key >> 6) == ln, 1.0, 0.0).astype(ka_ref.dtype)
            return 0

        lax.fori_loop(0, t // chunk, body, 0)
        _fill_rows(vd_ref, 0, v_ref, _dup_matrix(g))

    q = q_ref[...]
    ii = lax.broadcasted_iota(jnp.int32, (LANES, LANES), 0)
    jj = lax.broadcasted_iota(jnp.int32, (LANES, LANES), 1)
    shift = jnp.where(ii == jj + HEAD_DIM, 1.0, 0.0).astype(CD)
    lom = _lane_mask(Q_BLOCK, 0, HEAD_DIM, q.dtype)
    c0, c1 = q[:, :LANES], q[:, LANES:]
    q0 = jnp.concatenate([
        c0 * lom, jnp.dot(c0, shift, preferred_element_type=F32).astype(q.dtype),
        c1 * lom, jnp.dot(c1, shift, preferred_element_type=F32).astype(q.dtype)], axis=0)
    q1 = q0 + fq_ref[...]
    mf = mf_ref[...]
    qaug = jnp.concatenate([q1, jnp.concatenate([mf, mf, mf, mf], axis=0)], axis=1)

    def update(carry, s, v):
        m, l, acc = carry
        m_new = jnp.maximum(m, jnp.max(s, axis=1, keepdims=True))
        alpha = jnp.exp(m - m_new)
        p = jnp.exp(s - m_new)
        l = alpha * l + jnp.sum(p, axis=1, keepdims=True)
        acc = alpha * acc + jnp.dot(p.astype(CD), v, preferred_element_type=F32)
        return m_new, l, acc

    def far(jt, carry):
        off = pl.multiple_of(jt * Q_BLOCK, Q_BLOCK)
        s = lax.dot_general(qaug, ka_ref[pl.ds(off, Q_BLOCK), :], _TRANS_B, preferred_element_type=F32)
        return update(carry, s, vd_ref[pl.ds(off, Q_BLOCK), :])

    def near(ti, carry):
        off = pl.multiple_of((n - 1 + ti) * Q_BLOCK, Q_BLOCK)
        s = lax.dot_general(qaug, ka_ref[pl.ds(off, Q_BLOCK), :], _TRANS_B, preferred_element_type=F32)
        return update(carry, s + nb_ref[ti], vd_ref[pl.ds(off, Q_BLOCK), :])

    rows = NSA_REP * Q_BLOCK
    init = (jnp.full((rows, 1), M_INIT, F32), jnp.zeros((rows, 1), F32), jnp.zeros((rows, LANES), F32))
    carry = lax.fori_loop(0, jnp.maximum(n - 1, 0), far, init)
    carry = lax.fori_loop(jnp.where(n == 0, 1, 0), 2, near, carry)
    _, l, acc = carry
    o_sel = _unstack_heads(acc * (1.0 / l))

    gt = jax.nn.sigmoid(gt_ref[...])
    lane2 = lax.broadcasted_iota(jnp.int32, (Q_BLOCK, 2 * LANES), 1)

    def gate(branch):
        cols = []
        for r in range(NSA_REP):
            a = 8 + 3 * r + branch
            cols.append(jnp.where(g == 0, gt[:, a:a + 1], gt[:, a + 3 * NSA_REP:a + 3 * NSA_REP + 1]))
        w = jnp.where(lane2 < HEAD_DIM, cols[0], jnp.where(lane2 < 2 * HEAD_DIM, cols[1],
                      jnp.where(lane2 < 3 * HEAD_DIM, cols[2], cols[3])))
        return w

    o_ref[...] = gate(0) * oc_ref[...] + gate(1) * o_sel + gate(2) * ow_ref[...]


def _sel_attn(main3, q_col, k_col, v_col, maskfeat, farq, nearbias, o_cmp, o_win, small3):
    b, t, _ = main3.shape
    rows = NSA_REP * Q_BLOCK
    blk = lambda w: pl.BlockSpec((None, Q_BLOCK, w), lambda i, g, n: (i, n, g))
    return pl.pallas_call(
        _sel_kernel,
        out_shape=jax.ShapeDtypeStruct((b, t, GROUP_WIDTH), F32),
        grid=(b, NSA_KV_HEADS, t // Q_BLOCK),
        in_specs=[pl.BlockSpec((None, Q_BLOCK, 2 * LANES), lambda i, g, n: (i, n, q_col + g)),
                  pl.BlockSpec((None, t, LANES), lambda i, g, n: (i, 0, k_col)),
                  pl.BlockSpec((None, t, LANES), lambda i, g, n: (i, 0, v_col)),
                  blk(LANES),
                  pl.BlockSpec((None, rows, LANES), lambda i, g, n: (g, 0, 0)),
                  pl.BlockSpec((None, 2, rows, LANES), lambda i, g, n: (g, 0, 0, 0)),
                  blk(2 * LANES), blk(2 * LANES),
                  pl.BlockSpec((None, Q_BLOCK, LANES), lambda i, g, n: (i, n, 4))],
        out_specs=blk(2 * LANES),
        scratch_shapes=[pltpu.VMEM((t, 2 * LANES), CD), pltpu.VMEM((t, LANES), CD)],
        compiler_params=_cparams("arbitrary", "arbitrary", "arbitrary"),
        name="nsa_sel_attn",
    )(main3, main3, main3, maskfeat, farq, nearbias, o_cmp, o_win, small3)


def _outproj_kernel(a_ref, b_ref, c_ref, d_ref, gn_ref, w_ref, h_ref, o_ref, u_ref):
    @pl.when(pl.program_id(1) == 0)
    def _():
        for k, r in enumerate((a_ref, b_ref, c_ref, d_ref)):
            sl = slice(k * GROUP_WIDTH, (k + 1) * GROUP_WIDTH)
            u_ref[:, sl] = _rms(r[...], gn_ref[:, sl]).astype(u_ref.dtype)

    o_ref[...] = h_ref[...] + jnp.dot(u_ref[...], w_ref[...], preferred_element_type=F32)


def _outproj(parts, gn, w, h, tm, tn):
    n, d = h.shape
    mix = N_GROUPS * GROUP_WIDTH
    part = pl.BlockSpec((tm, GROUP_WIDTH), lambda i, j: (i, 0))
    return pl.pallas_call(
        _outproj_kernel,
        out_shape=jax.ShapeDtypeStruct((n, d), F32),
        grid=(n // tm, d // tn),
        in_specs=[part, part, part, part,
                  pl.BlockSpec((1, mix), lambda i, j: (0, 0)),
                  pl.BlockSpec((mix, tn), lambda i, j: (0, j)),
                  pl.BlockSpec((tm, tn), lambda i, j: (i, j))],
        out_specs=pl.BlockSpec((tm, tn), lambda i, j: (i, j)),
        scratch_shapes=[pltpu.VMEM((tm, mix), CD)],
        compiler_params=_cparams("parallel", "arbitrary"),
        name="outproj",
    )(*parts, gn.reshape(1, mix), w, h)


def _mlp_kernel(*refs, final):
    if final:
        h_ref, g_ref, wu_ref, wd_ref, gf_ref, o_ref, u_ref = refs
    else:
        h_ref, g_ref, wu_ref, wd_ref, o_ref, u_ref = refs
    c = pl.program_id(1)

    @pl.when(c == 0)
    def _():
        x = h_ref[...]
        u_ref[...] = _rms(x, g_ref[...]).astype(u_ref.dtype)
        o_ref[...] = x

    m = jnp.dot(u_ref[...], wu_ref[...], preferred_element_type=F32)
    a = jnp.square(jnp.maximum(m, 0.0)).astype(CD)
    o_ref[...] += jnp.dot(a, wd_ref[...], preferred_element_type=F32)

    if final:
        @pl.when(c == pl.num_programs(1) - 1)
        def _():
            o_ref[...] = _rms(o_ref[...], gf_ref[...])


def _mlp(h, g, wu, wd, gf, tm, tf):
    n, d = h.shape
    dff = wu.shape[1]
    final = gf is not None
    in_specs = [pl.BlockSpec((tm, d), lambda i, c: (i, 0)),
                pl.BlockSpec((1, d), lambda i, c: (0, 0)),
                pl.BlockSpec((d, tf), lambda i, c: (0, c)),
                pl.BlockSpec((tf, d), lambda i, c: (c, 0))]
    args = [h, g.reshape(1, d), wu, wd]
    if final:
        in_specs.append(pl.BlockSpec((1, d), lambda i, c: (0, 0)))
        args.append(gf.reshape(1, d))
    return pl.pallas_call(
        functools.partial(_mlp_kernel, final=final),
        out_shape=jax.ShapeDtypeStruct((n, d), F32),
        grid=(n // tm, dff // tf),
        in_specs=in_specs,
        out_specs=pl.BlockSpec((tm, d), lambda i, c: (i, 0)),
        scratch_shapes=[pltpu.VMEM((tm, d), CD)],
        compiler_params=_cparams("parallel", "arbitrary"),
        name="mlp",
    )(*args)


def _t5_bucket_np(dist):
    max_exact = REL_BUCKETS // 2
    d = np.maximum(dist, 0)
    ratio = np.log(np.maximum(d, 1).astype(np.float32) / np.float32(max_exact)) / np.float32(
        math.log(REL_MAX_DIST / max_exact))
    large = max_exact + (ratio * np.float32(REL_BUCKETS - max_exact)).astype(np.int32)
    large = np.minimum(large, REL_BUCKETS - 1)
    return np.where(d < max_exact, d, large)


def _bias_tables(rel_bias):
    tbl = rel_bias.astype(F32).T.reshape(2, NSA_KV_HEADS, NSA_REP, REL_BUCKETS)
    tbl_nsa, tbl_swa = tbl[0], tbl[1]
    i = np.arange(Q_BLOCK)[:, None]
    far_bucket = REL_BUCKETS - 1

    def banded(tb, window):
        jk = np.arange(window + Q_BLOCK)[None, :]
        dist = i + window - jk
        ok = (dist >= 0) & (dist < window)
        vals = tb[:, :, _t5_bucket_np(dist)]
        vals = jnp.where(jnp.asarray(ok)[None, None], vals, NEG_INF)
        return vals.reshape(NSA_KV_HEADS, NSA_REP * Q_BLOCK, window + Q_BLOCK)

    bm_win = banded(tbl_nsa, NSA_WINDOW)
    bm_swa = banded(tbl_swa, SWA_WINDOW)

    far = tbl_nsa[:, :, far_bucket]
    jk = np.arange(2 * Q_BLOCK)[None, :]
    dist = i + Q_BLOCK - jk
    near = tbl_nsa[:, :, _t5_bucket_np(dist)] - far[:, :, None, None]
    near = jnp.where(jnp.asarray(dist >= 0)[None, None], near, NEG_INF)
    near = near.reshape(NSA_KV_HEADS, NSA_REP, Q_BLOCK, 2, Q_BLOCK).transpose(0, 3, 1, 2, 4)
    near = near.reshape(NSA_KV_HEADS, 2, NSA_REP * Q_BLOCK, Q_BLOCK)
    far_rows = jnp.broadcast_to(far[:, :, None], (NSA_KV_HEADS, NSA_REP, Q_BLOCK)).reshape(NSA_KV_HEADS, -1)
    hi, lo_ = _split_parts(far_rows, 2)
    farq = jnp.zeros((NSA_KV_HEADS, NSA_REP * Q_BLOCK, LANES), CD)
    farq = farq.at[:, :, HEAD_DIM].set(hi).at[:, :, HEAD_DIM + 1].set(lo_)

    u = np.arange(16)[None, :]
    dist_c = i - 16 * u + 113
    band = tbl_nsa[:, :, _t5_bucket_np(dist_c)]
    band = jnp.where(jnp.asarray(dist_c >= 0)[None, None], band, MASK_NEG)
    band = band.reshape(NSA_KV_HEADS, NSA_REP * Q_BLOCK, 16)
    bh, bl = _split_parts(band, 2)
    pq = jnp.zeros((NSA_KV_HEADS, NSA_REP * Q_BLOCK, LANES), CD)
    pq = pq.at[:, :, 0:16].set(bh).at[:, :, 16:32].set(bl)
    pq = pq.at[:, :, 32].set(hi).at[:, :, 33].set(lo_).at[:, :, 34].set(jnp.asarray(MASK_NEG, CD))
    return bm_win, bm_swa, near, farq, pq


def _overlap_matrix(nc_pad, ns):
    ci = np.arange(nc_pad)[:, None]
    sj = np.arange(LANES)[None, :]
    ov = ((ci * NSA_CMP_STRIDE + NSA_CMP_LEN - 1 >= sj * NSA_SEL_LEN)
          & (ci * NSA_CMP_STRIDE <= sj * NSA_SEL_LEN + NSA_SEL_LEN - 1) & (sj < ns))
    return jnp.asarray(ov.astype(np.float32), CD)


def _layout_w_in(w):
    w = w.astype(F32)
    sc = HEAD_DIM ** -0.5
    z = lambda k: jnp.zeros((w.shape[0], k), F32)
    main = jnp.concatenate([w[:, 544:1056] * sc, w[:, 1056:2080], w[:, 2088:2600] * sc, w[:, 3392:3904] * sc,
                            w[:, 2600:3368], w[:, 3904:4160]], axis=1)
    kr = w[:, 512:544]
    blk_a = jnp.concatenate([w[:, 2080:2088], w[:, 3368:3392], z(32), kr, z(32)], axis=1)
    blk_b = jnp.concatenate([z(64), -kr[:, 16:], kr[:, :16], z(32)], axis=1)
    small = jnp.concatenate([w[:, 0:512], blk_a, blk_b], axis=1)
    return main.astype(CD), small.astype(CD)


_FOX_Q, _FOX_K512, _FOX_V = 0, 1, 8
_NSA_Q256, _SWA_Q256 = 6, 8
_KVC_COL = 2560
_KSEL, _VSEL, _KWIN, _VWIN, _KSWA, _VSWA = 22, 23, 24, 25, 26, 27


def _layout_mla(w_uq, w_ukv):
    w3 = w_uq.astype(F32).reshape(MLA_Q_RANK, GROUP_HEADS, MLA_NOPE + MLA_ROPE)
    nope, rp = w3[:, :, :MLA_NOPE], w3[:, :, MLA_NOPE:]
    half = MLA_ROPE // 2
    sw = jnp.concatenate([-rp[:, :, half:], rp[:, :, :half]], axis=-1)
    z = lambda k: jnp.zeros((MLA_Q_RANK, GROUP_HEADS, k), F32)
    plain = jnp.concatenate([nope, rp, z(32)], axis=-1).reshape(MLA_Q_RANK, -1)
    swapped = jnp.concatenate([z(64), sw, z(32)], axis=-1).reshape(MLA_Q_RANK, -1)
    wq = jnp.concatenate([plain, swapped], axis=1)
    k3 = w_ukv.astype(F32).reshape(MLA_KV_RANK, GROUP_HEADS, 2 * HEAD_DIM)
    wk = jnp.concatenate([k3[:, :, :MLA_NOPE], jnp.zeros((MLA_KV_RANK, GROUP_HEADS, HEAD_DIM), F32)],
                         axis=-1).reshape(MLA_KV_RANK, -1)
    wv = k3[:, :, MLA_NOPE:].reshape(MLA_KV_RANK, -1)
    return wq.astype(CD), wk.astype(CD), wv.astype(CD)


def _rope_tables(t):
    inv = ROPE_THETA ** (-jnp.arange(0, MLA_ROPE, 2, dtype=F32) / MLA_ROPE)
    ang = jnp.arange(t).astype(F32)[:, None] * inv[None, :]
    cc = jnp.concatenate([jnp.cos(ang)] * 2, axis=1)
    ss = jnp.concatenate([jnp.sin(ang)] * 2, axis=1)
    one, z64, z32 = jnp.ones((t, 64), F32), jnp.zeros((t, 64), F32), jnp.zeros((t, 32), F32)
    return (jnp.concatenate([one, cc, z32], axis=1), jnp.concatenate([z64, ss, z32], axis=1),
            jnp.concatenate([z64, cc, z32], axis=1))


def kernel(x, norm_attn, w_in, mla_q_norm, mla_w_uq, mla_kv_norm, mla_w_ukv, fox_b_f, nsa_cmp_pos,
           nsa_cmp_w1, nsa_cmp_w2, swa_sinks, group_norm, w_out, norm_mlp, w_up, w_down, rel_bias,
           final_norm):
    b, t, d = x.shape
    n = b * t
    depth = w_in.shape[0]
    assert t % 1024 == 0 and d == N_GROUPS * GROUP_WIDTH
    tm = 1024 if n % 1024 == 0 else 512
    tq = 256
    nr = t // NSA_CMP_STRIDE

    cosq, sinq, cosk = _rope_tables(t)
    bm_win, bm_swa, nearbias, farq, pq = _bias_tables(rel_bias)
    overlap = _overlap_matrix(nr, t // NSA_SEL_LEN)

    h = x.reshape(n, d).astype(F32)
    for l in range(depth):
        w_main, w_small = _layout_w_in(w_in[l])
        main = _norm_matmul(h, norm_attn[l], w_main, CD, tm, 512)
        small = _norm_matmul(h, norm_attn[l], w_small, F32, tm, 768)
        main3 = main.reshape(b, t, -1)
        small3 = small.reshape(b, t, -1)

        wq, wk, wv = _layout_mla(mla_w_uq[l], mla_w_ukv[l])
        q_m, k_m, v_m = _mla_prep(small, mla_q_norm[l], mla_kv_norm[l], wq, wk, wv, cosq, sinq, cosk, t, 512)
        o_mla = _dense_attn(q_m.reshape(b, t, -1), k_m.reshape(b, t, -1), v_m.reshape(b, t, -1), None,
                            fox=False, q_col=0, v_col=0, scale=(MLA_NOPE + MLA_ROPE) ** -0.5, tq=tq)

        k_aug, fq = _fox_prep(small3, fox_b_f[l], main3, _FOX_K512, 512)
        o_fox = _dense_attn(main3, k_aug, main3, fq, fox=True, q_col=_FOX_Q, v_col=_FOX_V, scale=1.0, tq=tq)

        kv16 = main3[:, :, _KVC_COL:_KVC_COL + 256].reshape(b, nr, NSA_CMP_STRIDE * 256)
        cmp = _nsa_compress(kv16, nsa_cmp_pos[l], nsa_cmp_w1[l], nsa_cmp_w2[l])
        o_cmp, maskfeat = _cmp_select(main3, _NSA_Q256, cmp, pq, overlap)
        o_win = _banded_attn(main3, _NSA_Q256, _KWIN, _VWIN, bm_win, None, NSA_WINDOW)
        o_nsa = _sel_attn(main3, _NSA_Q256, _KSEL, _VSEL, maskfeat, farq, nearbias, o_cmp, o_win, small3)

        sink_rows = jnp.broadcast_to(
            swa_sinks[l].astype(F32).reshape(NSA_KV_HEADS, NSA_REP, 1, 1),
            (NSA_KV_HEADS, NSA_REP, Q_BLOCK, LANES)).reshape(NSA_KV_HEADS, NSA_REP * Q_BLOCK, LANES)
        o_swa = _banded_attn(main3, _SWA_Q256, _KSWA, _VSWA, bm_swa, sink_rows, SWA_WINDOW)

        parts = [o.reshape(n, GROUP_WIDTH) for o in (o_mla, o_fox, o_nsa, o_swa)]
        h = _outproj(parts, group_norm[l], w_out[l].astype(CD), h, tm, 512)
        gf = final_norm if l == depth - 1 else None
        h = _mlp(h, norm_mlp[l], w_up[l].astype(CD), w_down[l].astype(CD), gf, 512, 1024)
    return h.reshape(b, t, d).astype(x.dtype)
```

```python
import functools
import math

import numpy as np
import jax
import jax.numpy as jnp
from jax import lax
from jax.experimental import pallas as pl
from jax.experimental.pallas import tpu as pltpu

HEAD_DIM = 64
GROUP_HEADS = 8
GROUP_WIDTH = GROUP_HEADS * HEAD_DIM
N_GROUPS = 4
Q_BLOCK = 128
EPS = 1e-6
NEG_INF = -1e30
BIG = 1e9

MLA_Q_RANK = 384
MLA_KV_RANK = 128
MLA_NOPE = 64
MLA_ROPE = 32
ROPE_THETA = 10000.0

NSA_KV_HEADS = 2
NSA_REP = GROUP_HEADS // NSA_KV_HEADS
NSA_CMP_LEN = 32
NSA_CMP_STRIDE = 16
NSA_SEL_LEN = 64
NSA_TOP_N = 8
NSA_WINDOW = 256
SWA_WINDOW = 128

REL_BUCKETS = 32
REL_MAX_DIST = 128

V7X_VMEM_BYTES = 64 * 2**20
VMEM_LIMIT = (V7X_VMEM_BYTES * 7) // 8
LANES = 128
CD = jnp.bfloat16
F32 = jnp.float32
MASK_NEG = -(2.0 ** 80)
M_INIT = -(2.0 ** 100)
LOG2E = math.log2(math.e)
_TRANS_B = (((1,), (1,)), ((), ()))


def _cparams(*sem):
    return pltpu.CompilerParams(dimension_semantics=sem, vmem_limit_bytes=VMEM_LIMIT)


def _split_parts(x, n):
    parts, r = [], x
    for _ in range(n):
        p = r.astype(CD)
        parts.append(p)
        r = r - p.astype(F32)
    return parts


def _rms(x, g):
    return x * lax.rsqrt(jnp.mean(x * x, axis=-1, keepdims=True) + EPS) * g


def _norm_matmul_kernel(x_ref, g_ref, w_ref, o_ref, u_ref):
    @pl.when(pl.program_id(1) == 0)
    def _():
        u_ref[...] = _rms(x_ref[...], g_ref[...]).astype(u_ref.dtype)

    o_ref[...] = jnp.dot(u_ref[...], w_ref[...], preferred_element_type=F32).astype(o_ref.dtype)


def _norm_matmul(x, g, w, out_dtype, tm, tn):
    n, d = x.shape
    nc = w.shape[1]
    return pl.pallas_call(
        _norm_matmul_kernel,
        out_shape=jax.ShapeDtypeStruct((n, nc), out_dtype),
        grid=(n // tm, nc // tn),
        in_specs=[
            pl.BlockSpec((tm, d), lambda i, j: (i, 0)),
            pl.BlockSpec((1, d), lambda i, j: (0, 0)),
            pl.BlockSpec((d, tn), lambda i, j: (0, j)),
        ],
        out_specs=pl.BlockSpec((tm, tn), lambda i, j: (i, j)),
        scratch_shapes=[pltpu.VMEM((tm, d), CD)],
        compiler_params=_cparams("parallel", "arbitrary"),
        name="norm_matmul",
    )(x, g.reshape(1, d), w)


def _mla_prep_kernel(sm_ref, qn_ref, kvn_ref, wq_ref, wk_ref, wv_ref, cq_ref, sq_ref, ck_ref,
                     q_ref, k_ref, v_ref):
    sm = sm_ref[...]
    nq = _rms(sm[:, :MLA_Q_RANK], qn_ref[...]).astype(CD)
    nkv = _rms(sm[:, MLA_Q_RANK:MLA_Q_RANK + MLA_KV_RANK], kvn_ref[...]).astype(CD)
    blk_a = sm[:, 512:640]
    blk_b = sm[:, 640:768]
    cosq, sinq, cosk = cq_ref[...], sq_ref[...], ck_ref[...]
    hw = GROUP_HEADS * LANES
    qq = jnp.dot(nq, wq_ref[...], preferred_element_type=F32)
    kk = jnp.dot(nkv, wk_ref[...], preferred_element_type=F32)
    kpe = blk_a * cosk + blk_b * sinq
    for h in range(GROUP_HEADS):
        sl = slice(h * LANES, (h + 1) * LANES)
        q_ref[:, sl] = (qq[:, sl] * cosq + qq[:, hw + h * LANES:hw + (h + 1) * LANES] * sinq).astype(q_ref.dtype)
        k_ref[:, sl] = (kk[:, sl] + kpe).astype(k_ref.dtype)
    v_ref[...] = jnp.dot(nkv, wv_ref[...], preferred_element_type=F32).astype(v_ref.dtype)


def _mla_prep(small, qn, kvn, wq, wk, wv, cosq, sinq, cosk, t, tm):
    n = small.shape[0]
    tb = t // tm
    hw = GROUP_HEADS * LANES
    full = lambda a: pl.BlockSpec(a.shape, lambda i: (0,) * a.ndim)
    tab = pl.BlockSpec((tm, LANES), lambda i: (i % tb, 0))
    qn = qn.reshape(1, -1)
    kvn = kvn.reshape(1, -1)
    return pl.pallas_call(
        _mla_prep_kernel,
        out_shape=(jax.ShapeDtypeStruct((n, hw), CD), jax.ShapeDtypeStruct((n, hw), CD),
                   jax.ShapeDtypeStruct((n, GROUP_WIDTH), CD)),
        grid=(n // tm,),
        in_specs=[pl.BlockSpec((tm, small.shape[1]), lambda i: (i, 0)), full(qn), full(kvn),
                  full(wq), full(wk), full(wv), tab, tab, tab],
        out_specs=(pl.BlockSpec((tm, hw), lambda i: (i, 0)), pl.BlockSpec((tm, hw), lambda i: (i, 0)),
                   pl.BlockSpec((tm, GROUP_WIDTH), lambda i: (i, 0))),
        compiler_params=_cparams("parallel"),
        name="mla_prep",
    )(small, qn, kvn, wq, wk, wv, cosq, sinq, cosk)


def _fox_prep_kernel(fl_ref, bf_ref, k_ref, efq_ref, ekf_ref, kaug_ref, fq_ref, carry_ref, *, tc):
    @pl.when(pl.program_id(1) == 0)
    def _():
        carry_ref[...] = jnp.zeros_like(carry_ref)

    x = fl_ref[...] + bf_ref[...]
    logf = -(jnp.maximum(-x, 0.0) + jnp.log1p(jnp.exp(-jnp.abs(x))))
    lane = lax.broadcasted_iota(jnp.int32, logf.shape, 1)
    logf = jnp.where(lane < GROUP_HEADS, logf, 0.0)
    row = lax.broadcasted_iota(jnp.int32, (tc, tc), 0)
    col = lax.broadcasted_iota(jnp.int32, (tc, tc), 1)
    tri = jnp.where(row >= col, 1.0, 0.0).astype(CD)
    cs = jnp.zeros(logf.shape, F32)
    for part in _split_parts(logf, 3):
        cs = cs + jnp.dot(tri, part, preferred_element_type=F32)
    fc = cs + carry_ref[...]
    carry_ref[...] = fc[tc - 1:tc, :]
    parts = _split_parts(fc, 3)
    fq = jnp.zeros(fq_ref.shape, F32)
    kf = jnp.zeros(fq_ref.shape, F32)
    for i, part in enumerate(parts):
        fq = fq + jnp.dot(part, efq_ref[...], preferred_element_type=F32)
        kf = kf + jnp.dot(part, ekf_ref[i], preferred_element_type=F32)
    fq_ref[...] = fq
    k = k_ref[...]
    for p in range(GROUP_HEADS // 2):
        kaug_ref[:, 2 * p * LANES:(2 * p + 1) * LANES] = k[:, p * LANES:(p + 1) * LANES]
        kaug_ref[:, (2 * p + 1) * LANES:(2 * p + 2) * LANES] = kf[:, p * LANES:(p + 1) * LANES].astype(kaug_ref.dtype)


def _fox_prep(small3, b_f, main3, k_col, tc):
    b, t, _ = small3.shape
    pairs = GROUP_HEADS // 2
    efq = np.zeros((LANES, pairs * LANES), np.float32)
    ekf = np.zeros((3, LANES, pairs * LANES), np.float32)
    for h in range(GROUP_HEADS):
        p, a = divmod(h, 2)
        efq[h, p * LANES + a] = 1.0
        for i in range(3):
            ekf[i, h, p * LANES + 3 * a + i] = -1.0
    bf = jnp.zeros((1, LANES), F32).at[0, :GROUP_HEADS].set(b_f.astype(F32))
    return pl.pallas_call(
        functools.partial(_fox_prep_kernel, tc=tc),
        out_shape=(jax.ShapeDtypeStruct((b, t, 2 * GROUP_WIDTH), CD),
                   jax.ShapeDtypeStruct((b, t, GROUP_WIDTH), F32)),
        grid=(b, t // tc),
        in_specs=[pl.BlockSpec((None, tc, LANES), lambda i, j: (i, j, 4)),
                  pl.BlockSpec((1, LANES), lambda i, j: (0, 0)),
                  pl.BlockSpec((None, tc, GROUP_WIDTH), lambda i, j: (i, j, k_col)),
                  pl.BlockSpec(efq.shape, lambda i, j: (0, 0)),
                  pl.BlockSpec(ekf.shape, lambda i, j: (0, 0, 0))],
        out_specs=(pl.BlockSpec((None, tc, 2 * GROUP_WIDTH), lambda i, j: (i, j, 0)),
                   pl.BlockSpec((None, tc, GROUP_WIDTH), lambda i, j: (i, j, 0))),
        scratch_shapes=[pltpu.VMEM((1, LANES), F32)],
        compiler_params=_cparams("arbitrary", "arbitrary"),
        name="fox_prep",
    )(small3, bf, main3, jnp.asarray(efq, CD), jnp.asarray(ekf, CD))


def _dense_attn_kernel(*refs, fox, tq, c):
    if fox:
        q_ref, k_ref, v_ref, fq_ref, o_ref, s_ref, mx_ref, ls_ref, acc_ref = refs
    else:
        q_ref, k_ref, v_ref, o_ref, s_ref, mx_ref, ls_ref, acc_ref = refs
    qi = pl.program_id(2)
    q = q_ref[...]
    if fox:
        qs = (jnp.concatenate([q * _lane_mask(tq, 0, HEAD_DIM, q.dtype), _lane_mask(tq, 0, 3, q.dtype)], axis=1),
              jnp.concatenate([q * _lane_mask(tq, HEAD_DIM, LANES, q.dtype), _lane_mask(tq, 3, 6, q.dtype)], axis=1))
        fq = fq_ref[...]
        fqs = (fq[:, 0:1], fq[:, 1:2])
    else:
        qs = (q[:, :LANES], q[:, LANES:])
        fqs = (None, None)

    tk = s_ref.shape[3]
    nch = tk // LANES
    jd = (qi * tq) // tk

    def scores(h, j):
        off = pl.multiple_of(j * tk, tk)
        k = k_ref[pl.ds(off, tk), :] if fox else k_ref[pl.ds(off, tk), pl.ds(h * LANES, LANES)]
        return lax.dot_general(qs[h], k, _TRANS_B, preferred_element_type=F32)

    def keep(h, j, s):
        s_ref[h, j] = s
        mx = mx_ref[h]
        for i in range(nch):
            mx = jnp.maximum(mx, s[:, i * LANES:(i + 1) * LANES])
        mx_ref[h] = mx

    mx_ref[...] = jnp.full(mx_ref.shape, M_INIT, F32)

    def first(j, _):
        for h in range(2):
            keep(h, j, scores(h, j))
        return 0

    lax.fori_loop(0, jd, first, 0)
    row = lax.broadcasted_iota(jnp.int32, (tq, tk), 0)
    col = lax.broadcasted_iota(jnp.int32, (tq, tk), 1)
    causal = col <= row + (qi * tq - jd * tk)
    for h in range(2):
        keep(h, jd, jnp.where(causal, scores(h, jd), NEG_INF))
        m = jnp.max(mx_ref[h], axis=1, keepdims=True)
        if fox:
            m = (m + fqs[h]) - fqs[h]
        mx_ref[h] = jnp.broadcast_to(m * c, (tq, LANES))

    ls_ref[...] = jnp.zeros(ls_ref.shape, F32)
    acc_ref[...] = jnp.zeros(acc_ref.shape, F32)

    def second(j, _):
        off = pl.multiple_of(j * tk, tk)
        v = v_ref[pl.ds(off, tk), :]
        for h in range(2):
            s = s_ref[h, j]
            mc = mx_ref[h]
            ps = [jnp.exp2(s[:, i * LANES:(i + 1) * LANES] * c - mc) for i in range(nch)]
            ls = ls_ref[h]
            for p in ps:
                ls = ls + p
            ls_ref[h] = ls
            acc_ref[h] += jnp.dot(jnp.concatenate(ps, axis=1).astype(CD), v, preferred_element_type=F32)
        return 0

    lax.fori_loop(0, jd + 1, second, 0)
    outs = [acc_ref[h] * (1.0 / jnp.sum(ls_ref[h], axis=1, keepdims=True)) for h in range(2)]
    lane = lax.broadcasted_iota(jnp.int32, (tq, LANES), 1)
    o_ref[...] = jnp.where(lane < HEAD_DIM, outs[0], outs[1]).astype(o_ref.dtype)


def _dense_attn(q3, k3, v3, fq3, *, fox, q_col, v_col, scale, tq, tk):
    b, t, _ = q3.shape
    assert tk % tq == 0 and t % tk == 0
    pairs = GROUP_HEADS // 2
    qw = LANES if fox else 2 * LANES
    in_specs = [pl.BlockSpec((None, tq, qw), lambda i, p, j: (i, j, q_col + p)),
                pl.BlockSpec((None, t, 2 * LANES), lambda i, p, j: (i, 0, p)),
                pl.BlockSpec((None, t, LANES), lambda i, p, j: (i, 0, v_col + p))]
    args = [q3, k3, v3]
    if fox:
        in_specs.append(pl.BlockSpec((None, tq, LANES), lambda i, p, j: (i, j, p)))
        args.append(fq3)
    return pl.pallas_call(
        functools.partial(_dense_attn_kernel, fox=fox, tq=tq, c=scale * LOG2E),
        out_shape=jax.ShapeDtypeStruct((b, t, GROUP_WIDTH), F32),
        grid=(b, pairs, t // tq),
        in_specs=in_specs,
        out_specs=pl.BlockSpec((None, tq, LANES), lambda i, p, j: (i, j, p)),
        scratch_shapes=[pltpu.VMEM((2, t // tk, tq, tk), F32), pltpu.VMEM((2, tq, LANES), F32),
                        pltpu.VMEM((2, tq, LANES), F32), pltpu.VMEM((2, tq, LANES), F32)],
        compiler_params=_cparams("parallel", "parallel", "arbitrary"),
        name="fox_attn" if fox else "mla_attn",
    )(*args)


def _gelu_tanh(x):
    return 0.5 * x * (1.0 + jnp.tanh(math.sqrt(2.0 / math.pi) * (x + 0.044715 * (x * x * x))))


def _compress_kernel(r_ref, w1a_ref, w1b_ref, pa_ref, pb_ref, w2_ref, o_ref):
    r = r_ref[...]
    nr = r.shape[0]
    ya = jnp.dot(r, w1a_ref[...], preferred_element_type=F32)
    yb = jnp.dot(r, w1b_ref[...], preferred_element_type=F32)
    pc = (jnp.dot(pa_ref[...], w1a_ref[...], preferred_element_type=F32)
          + jnp.dot(pb_ref[...], w1b_ref[...], preferred_element_type=F32))[0:1, :]
    pre = ya + pltpu.roll(yb, nr - 1, 0) + pc
    h1 = _gelu_tanh(pre).astype(CD)
    rowi = lax.broadcasted_iota(jnp.int32, (nr, LANES), 0)
    for g in range(NSA_KV_HEADS):
        y = jnp.dot(h1, w2_ref[g], preferred_element_type=F32)
        o_ref[g] = jnp.where(rowi < nr - 1, y, 0.0).astype(o_ref.dtype)


def _nsa_compress(kv16, cmp_pos, cmp_w1, cmp_w2):
    b, nr, kw = kv16.shape
    half = NSA_CMP_LEN // 2
    eye2 = jnp.eye(2, dtype=F32)
    w1r = cmp_w1.astype(F32).reshape(2, NSA_CMP_LEN, HEAD_DIM, HEAD_DIM)
    expand = lambda w: jnp.einsum('klde,kw,gh->klwgdhe', w, eye2, eye2).reshape(2, kw, LANES).astype(CD)
    w1a, w1b = expand(w1r[:, :half]), expand(w1r[:, half:])
    posr = cmp_pos.astype(F32)
    tile = lambda p: jnp.broadcast_to(p[:, None, :, None, None, :], (2, 8, half, 2, 2, HEAD_DIM)).reshape(2, 8, kw).astype(CD)
    pa, pb = tile(posr[:, :half]), tile(posr[:, half:])
    w2 = cmp_w2.astype(F32)
    w2d = jnp.zeros((2, NSA_KV_HEADS, LANES, LANES), F32)
    for g in range(NSA_KV_HEADS):
        blk = jnp.concatenate([w2, w2], axis=2)
        w2d = w2d.at[:, g, g * HEAD_DIM:(g + 1) * HEAD_DIM, :].set(blk)
    w2d = w2d.astype(CD)
    return pl.pallas_call(
        _compress_kernel,
        out_shape=jax.ShapeDtypeStruct((b, 2, NSA_KV_HEADS, nr, LANES), CD),
        grid=(b, 2),
        in_specs=[pl.BlockSpec((None, nr, kw), lambda i, k: (i, 0, 0)),
                  pl.BlockSpec((None, kw, LANES), lambda i, k: (k, 0, 0)),
                  pl.BlockSpec((None, kw, LANES), lambda i, k: (k, 0, 0)),
                  pl.BlockSpec((None, 8, kw), lambda i, k: (k, 0, 0)),
                  pl.BlockSpec((None, 8, kw), lambda i, k: (k, 0, 0)),
                  pl.BlockSpec((None, NSA_KV_HEADS, LANES, LANES), lambda i, k: (k, 0, 0, 0))],
        out_specs=pl.BlockSpec((None, None, NSA_KV_HEADS, nr, LANES), lambda i, k: (i, k, 0, 0, 0)),
        compiler_params=_cparams("parallel", "arbitrary"),
        name="nsa_compress",
    )(kv16, w1a, w1b, pa, pb, w2d)


def _lane_mask(rows, lo, hi, dtype):
    lane = lax.broadcasted_iota(jnp.int32, (rows, LANES), 1)
    return jnp.where((lane >= lo) & (lane < hi), 1.0, 0.0).astype(dtype)


def _stack_heads_native(q):
    lo = _lane_mask(Q_BLOCK, 0, HEAD_DIM, q.dtype)
    hi = _lane_mask(Q_BLOCK, HEAD_DIM, LANES, q.dtype)
    c0, c1 = q[:, :LANES], q[:, LANES:]
    return jnp.concatenate([c0 * lo, c0 * hi, c1 * lo, c1 * hi], axis=0)


def _unstack_heads(o):
    lane = lax.broadcasted_iota(jnp.int32, (Q_BLOCK, LANES), 1)
    lo = lane < HEAD_DIM
    return jnp.concatenate([jnp.where(lo, o[0:128], o[128:256]), jnp.where(lo, o[256:384], o[384:512])], axis=1)


def _dup_matrix(g):
    i = lax.broadcasted_iota(jnp.int32, (LANES, LANES), 0)
    j = lax.broadcasted_iota(jnp.int32, (LANES, LANES), 1)
    return jnp.where(i == g * HEAD_DIM + (j & (HEAD_DIM - 1)), 1.0, 0.0).astype(CD)


def _fill_rows(dst_ref, row0, src_ref, mat, chunk=1024):
    n = src_ref.shape[0]

    def body(i, _):
        off = pl.multiple_of(i * chunk, chunk)
        dst_ref[pl.ds(row0 + off, chunk), :] = jnp.dot(
            src_ref[pl.ds(off, chunk), :], mat, preferred_element_type=F32).astype(dst_ref.dtype)
        return 0

    lax.fori_loop(0, n // chunk, body, 0)


def _cmp_select_kernel(q_ref, kc_ref, vc_ref, pq_ref, ov_ref, o_ref, mf_ref):
    n = pl.program_id(2)
    nc = kc_ref.shape[0]
    qaug = jnp.concatenate([_stack_heads_native(q_ref[...]), pq_ref[...]], axis=1)
    ci = lax.broadcasted_iota(jnp.int32, (nc, LANES), 0)
    f = lax.broadcasted_iota(jnp.int32, (nc, LANES), 1)
    uc = ci - 8 * n + 9
    one_if = lambda cond: jnp.where(cond, 1.0, 0.0)
    feat = jnp.where(f < 32, one_if(uc == (f & 15)),
                     jnp.where(f < 34, one_if(uc < 0), one_if((f == 34) & (uc > 15)))).astype(CD)
    kaug = jnp.concatenate([kc_ref[...], feat], axis=1)
    s = lax.dot_general(qaug, kaug, _TRANS_B, preferred_element_type=F32)
    m = jnp.max(s, axis=1, keepdims=True)
    p = jnp.exp(s - m)
    l = jnp.sum(p, axis=1, keepdims=True)
    pc = p * jnp.where(m > 0.5 * MASK_NEG, 1.0 / l, 0.0)
    o = jnp.dot(pc.astype(CD), vc_ref[...], preferred_element_type=F32)
    o_ref[...] = _unstack_heads(o)
    pcs = pc[0:128] + pc[128:256] + pc[256:384] + pc[384:512]
    imp = jnp.dot(pcs.astype(CD), ov_ref[...], preferred_element_type=F32)
    j = lax.broadcasted_iota(jnp.int32, (Q_BLOCK, LANES), 1)
    i = lax.broadcasted_iota(jnp.int32, (Q_BLOCK, LANES), 0)
    cur = 2 * n + jnp.where(i >= NSA_SEL_LEN, 1, 0)
    forced = (j == 0) | (j == cur) | (j == cur - 1)
    score = jnp.where(j <= cur, jnp.where(forced, BIG, imp), NEG_INF)
    sel = jnp.zeros((Q_BLOCK, LANES), F32)
    jf = j.astype(F32)
    for _ in range(NSA_TOP_N):
        mx = jnp.max(score, axis=1, keepdims=True)
        idx = jnp.min(jnp.where(score == mx, jf, float(LANES)), axis=1, keepdims=True)
        pick = jf == idx
        sel = jnp.where(pick, 1.0, sel)
        score = jnp.where(pick, -3.0e38, score)
    mf_ref[...] = jnp.where(sel > 0.5, 0.0, MASK_NEG).astype(mf_ref.dtype)


def _cmp_select(main3, q_col, cmp, pq, overlap):
    b, t, _ = main3.shape
    nc = cmp.shape[3]
    return pl.pallas_call(
        _cmp_select_kernel,
        out_shape=(jax.ShapeDtypeStruct((b, t, GROUP_WIDTH), F32),
                   jax.ShapeDtypeStruct((b, t, NSA_KV_HEADS * LANES), CD)),
        grid=(b, NSA_KV_HEADS, t // Q_BLOCK),
        in_specs=[pl.BlockSpec((None, Q_BLOCK, 2 * LANES), lambda i, g, n: (i, n, q_col + g)),
                  pl.BlockSpec((None, None, None, nc, LANES), lambda i, g, n: (i, 0, g, 0, 0)),
                  pl.BlockSpec((None, None, None, nc, LANES), lambda i, g, n: (i, 1, g, 0, 0)),
                  pl.BlockSpec((None, NSA_REP * Q_BLOCK, LANES), lambda i, g, n: (g, 0, 0)),
                  pl.BlockSpec(overlap.shape, lambda i, g, n: (0, 0))],
        out_specs=(pl.BlockSpec((None, Q_BLOCK, 2 * LANES), lambda i, g, n: (i, n, g)),
                   pl.BlockSpec((None, Q_BLOCK, LANES), lambda i, g, n: (i, n, g))),
        compiler_params=_cparams("parallel", "parallel", "arbitrary"),
        name="nsa_cmp_select",
    )(main3, cmp, cmp, pq, overlap)


def _banded_kernel(*refs, window, sinks):
    if sinks:
        q_ref, k_ref, v_ref, bm_ref, sk_ref, o_ref, kp_ref, vp_ref = refs
    else:
        q_ref, k_ref, v_ref, bm_ref, o_ref, kp_ref, vp_ref = refs
    g = pl.program_id(1)
    n = pl.program_id(2)
    kw = window + Q_BLOCK

    @pl.when(n == 0)
    def _():
        dup = _dup_matrix(g)
        kp_ref[0:window, :] = jnp.zeros((window, LANES), kp_ref.dtype)
        vp_ref[0:window, :] = jnp.zeros((window, LANES), vp_ref.dtype)
        _fill_rows(kp_ref, window, k_ref, dup)
        _fill_rows(vp_ref, window, v_ref, dup)

    start = pl.multiple_of(n * Q_BLOCK, Q_BLOCK)
    ks = kp_ref[pl.ds(start, kw), :]
    vs = vp_ref[pl.ds(start, kw), :]
    s = lax.dot_general(_stack_heads_native(q_ref[...]), ks, _TRANS_B, preferred_element_type=F32)
    s = s + bm_ref[...]
    kpos = start - window + lax.broadcasted_iota(jnp.int32, s.shape, 1)
    s = jnp.where(kpos >= 0, s, NEG_INF)
    m = jnp.max(s, axis=1, keepdims=True)
    if sinks:
        sink = sk_ref[:, 0:1]
        m = jnp.maximum(m, sink)
    p = jnp.exp(s - m)
    l = jnp.sum(p, axis=1, keepdims=True)
    if sinks:
        l = l + jnp.exp(sink - m)
    o = jnp.dot(p.astype(CD), vs, preferred_element_type=F32) * (1.0 / l)
    o_ref[...] = _unstack_heads(o)


def _banded_attn(main3, q_col, k_col, v_col, biasmask, sink_rows, window):
    b, t, _ = main3.shape
    kw = window + Q_BLOCK
    sinks = sink_rows is not None
    in_specs = [pl.BlockSpec((None, Q_BLOCK, 2 * LANES), lambda i, g, n: (i, n, q_col + g)),
                pl.BlockSpec((None, t, LANES), lambda i, g, n: (i, 0, k_col)),
                pl.BlockSpec((None, t, LANES), lambda i, g, n: (i, 0, v_col)),
                pl.BlockSpec((None, NSA_REP * Q_BLOCK, kw), lambda i, g, n: (g, 0, 0))]
    args = [main3, main3, main3, biasmask]
    if sinks:
        in_specs.append(pl.BlockSpec((None, NSA_REP * Q_BLOCK, LANES), lambda i, g, n: (g, 0, 0)))
        args.append(sink_rows)
    return pl.pallas_call(
        functools.partial(_banded_kernel, window=window, sinks=sinks),
        out_shape=jax.ShapeDtypeStruct((b, t, GROUP_WIDTH), F32),
        grid=(b, NSA_KV_HEADS, t // Q_BLOCK),
        in_specs=in_specs,
        out_specs=pl.BlockSpec((None, Q_BLOCK, 2 * LANES), lambda i, g, n: (i, n, g)),
        scratch_shapes=[pltpu.VMEM((window + t, LANES), CD), pltpu.VMEM((window + t, LANES), CD)],
        compiler_params=_cparams("arbitrary", "arbitrary", "arbitrary"),
        name="swa_attn" if sinks else "nsa_win_attn",
    )(*args)


def _sel_kernel(q_ref, k_ref, v_ref, mf_ref, fq_ref, nb_ref, oc_ref, ow_ref, gt_ref, o_ref,
                ka_ref, vd_ref, qa_ref, s_ref, sn_ref, mx_ref, ls_ref, acc_ref):
    g = pl.program_id(1)
    n = pl.program_id(2)
    t = k_ref.shape[0]

    @pl.when(n == 0)
    def _():
        i = lax.broadcasted_iota(jnp.int32, (LANES, LANES), 0)
        j = lax.broadcasted_iota(jnp.int32, (LANES, LANES), 1)
        pick = jnp.where((j < HEAD_DIM) & (i == g * HEAD_DIM + j), 1.0, 0.0).astype(CD)
        chunk = 1024

        def body(c, _):
            off = pl.multiple_of(c * chunk, chunk)
            kk = jnp.dot(k_ref[pl.ds(off, chunk), :], pick, preferred_element_type=F32)
            ln = lax.broadcasted_iota(jnp.int32, (chunk, LANES), 1)
            kk = jnp.where((ln == HEAD_DIM) | (ln == HEAD_DIM + 1), 1.0, kk)
            ka_ref[pl.ds(off, chunk), 0:LANES] = kk.astype(ka_ref.dtype)
            key = off + lax.broadcasted_iota(jnp.int32, (chunk, LANES), 0)
            ka_ref[pl.ds(off, chunk), LANES:2 * LANES] = jnp.where(
                (key >> 6) == ln, 1.0, 0.0).astype(ka_ref.dtype)
            return 0

        lax.fori_loop(0, t // chunk, body, 0)
        _fill_rows(vd_ref, 0, v_ref, _dup_matrix(g))

    q = q_ref[...]
    ii = lax.broadcasted_iota(jnp.int32, (LANES, LANES), 0)
    jj = lax.broadcasted_iota(jnp.int32, (LANES, LANES), 1)
    shift = jnp.where(ii == jj + HEAD_DIM, 1.0, 0.0).astype(CD)
    lom = _lane_mask(Q_BLOCK, 0, HEAD_DIM, q.dtype)
    c0, c1 = q[:, :LANES], q[:, LANES:]
    q0 = jnp.concatenate([
        c0 * lom, jnp.dot(c0, shift, preferred_element_type=F32).astype(q.dtype),
        c1 * lom, jnp.dot(c1, shift, preferred_element_type=F32).astype(q.dtype)], axis=0)
    q1 = q0 + fq_ref[...]
    mf = mf_ref[...]
    qa_ref[...] = jnp.concatenate([q1, jnp.concatenate([mf, mf, mf, mf], axis=0)], axis=1)

    rows = NSA_REP * Q_BLOCK
    tkf = s_ref.shape[2]
    tkn = sn_ref.shape[1]
    near_off = pl.multiple_of(jnp.maximum(n - 1, 0) * Q_BLOCK, Q_BLOCK)
    n_full = near_off // tkf
    rem = near_off - n_full * tkf

    def far_scores(j):
        off = pl.multiple_of(j * tkf, tkf)
        return lax.dot_general(qa_ref[...], ka_ref[pl.ds(off, tkf), :], _TRANS_B, preferred_element_type=F32)

    def fold_max(s):
        mx = mx_ref[...]
        for i in range(s.shape[1] // LANES):
            mx = jnp.maximum(mx, s[:, i * LANES:(i + 1) * LANES])
        mx_ref[...] = mx

    mx_ref[...] = jnp.full((rows, LANES), M_INIT, F32)

    def first(j, _):
        s = far_scores(j)
        s_ref[j] = s
        fold_max(s)
        return 0

    lax.fori_loop(0, n_full, first, 0)

    @pl.when(rem > 0)
    def _():
        col = lax.broadcasted_iota(jnp.int32, (rows, tkf), 1)
        s = jnp.where(col < rem, far_scores(n_full), NEG_INF)
        s_ref[n_full] = s
        fold_max(s)

    sn = lax.dot_general(qa_ref[...], ka_ref[pl.ds(near_off, tkn), :], _TRANS_B, preferred_element_type=F32)
    sn = sn + nb_ref[jnp.minimum(n, 1)]
    sn_ref[...] = sn
    fold_max(sn)
    mx_ref[...] = jnp.broadcast_to(jnp.max(mx_ref[...], axis=1, keepdims=True), (rows, LANES))

    def probs(s):
        mb = mx_ref[...]
        ps = [jnp.exp(s[:, i * LANES:(i + 1) * LANES] - mb) for i in range(s.shape[1] // LANES)]
        ls = ls_ref[...]
        for p in ps:
            ls = ls + p
        ls_ref[...] = ls
        return jnp.concatenate(ps, axis=1).astype(CD)

    ls_ref[...] = jnp.zeros((rows, LANES), F32)
    acc_ref[...] = jnp.dot(probs(sn_ref[...]), vd_ref[pl.ds(near_off, tkn), :], preferred_element_type=F32)

    def second(j, _):
        off = pl.multiple_of(j * tkf, tkf)
        acc_ref[...] += jnp.dot(probs(s_ref[j]), vd_ref[pl.ds(off, tkf), :], preferred_element_type=F32)
        return 0

    lax.fori_loop(0, n_full + jnp.where(rem > 0, 1, 0), second, 0)
    l = jnp.sum(ls_ref[...], axis=1, keepdims=True)
    o_sel = _unstack_heads(acc_ref[...] * (1.0 / l))

    gt = jax.nn.sigmoid(gt_ref[...])
    lane2 = lax.broadcasted_iota(jnp.int32, (Q_BLOCK, 2 * LANES), 1)

    def gate(branch):
        cols = []
        for r in range(NSA_REP):
            a = 8 + 3 * r + branch
            cols.append(jnp.where(g == 0, gt[:, a:a + 1], gt[:, a + 3 * NSA_REP:a + 3 * NSA_REP + 1]))
        w = jnp.where(lane2 < HEAD_DIM, cols[0], jnp.where(lane2 < 2 * HEAD_DIM, cols[1],
                      jnp.where(lane2 < 3 * HEAD_DIM, cols[2], cols[3])))
        return w

    o_ref[...] = gate(0) * oc_ref[...] + gate(1) * o_sel + gate(2) * ow_ref[...]


def _sel_attn(main3, q_col, k_col, v_col, maskfeat, farq, nearbias, o_cmp, o_win, small3, tkf):
    b, t, _ = main3.shape
    assert t % tkf == 0 and tkf % Q_BLOCK == 0
    rows = NSA_REP * Q_BLOCK
    blk = lambda w: pl.BlockSpec((None, Q_BLOCK, w), lambda i, g, n: (i, n, g))
    return pl.pallas_call(
        _sel_kernel,
        out_shape=jax.ShapeDtypeStruct((b, t, GROUP_WIDTH), F32),
        grid=(b, NSA_KV_HEADS, t // Q_BLOCK),
        in_specs=[pl.BlockSpec((None, Q_BLOCK, 2 * LANES), lambda i, g, n: (i, n, q_col + g)),
                  pl.BlockSpec((None, t, LANES), lambda i, g, n: (i, 0, k_col)),
                  pl.BlockSpec((None, t, LANES), lambda i, g, n: (i, 0, v_col)),
                  blk(LANES),
                  pl.BlockSpec((None, rows, LANES), lambda i, g, n: (g, 0, 0)),
                  pl.BlockSpec((None, 2, rows, 2 * Q_BLOCK), lambda i, g, n: (g, 0, 0, 0)),
                  blk(2 * LANES), blk(2 * LANES),
                  pl.BlockSpec((None, Q_BLOCK, LANES), lambda i, g, n: (i, n, 4))],
        out_specs=blk(2 * LANES),
        scratch_shapes=[pltpu.VMEM((t, 2 * LANES), CD), pltpu.VMEM((t, LANES), CD),
                        pltpu.VMEM((rows, 2 * LANES), CD), pltpu.VMEM((t // tkf, rows, tkf), F32),
                        pltpu.VMEM((rows, 2 * Q_BLOCK), F32),
                        pltpu.VMEM((rows, LANES), F32), pltpu.VMEM((rows, LANES), F32),
                        pltpu.VMEM((rows, LANES), F32)],
        compiler_params=_cparams("arbitrary", "arbitrary", "arbitrary"),
        name="nsa_sel_attn",
    )(main3, main3, main3, maskfeat, farq, nearbias, o_cmp, o_win, small3)


def _outproj_kernel(a_ref, b_ref, c_ref, d_ref, gn_ref, w_ref, h_ref, o_ref, u_ref):
    @pl.when(pl.program_id(1) == 0)
    def _():
        for k, r in enumerate((a_ref, b_ref, c_ref, d_ref)):
            sl = slice(k * GROUP_WIDTH, (k + 1) * GROUP_WIDTH)
            u_ref[:, sl] = _rms(r[...], gn_ref[:, sl]).astype(u_ref.dtype)

    o_ref[...] = h_ref[...] + jnp.dot(u_ref[...], w_ref[...], preferred_element_type=F32)


def _outproj(parts, gn, w, h, tm, tn):
    n, d = h.shape
    mix = N_GROUPS * GROUP_WIDTH
    part = pl.BlockSpec((tm, GROUP_WIDTH), lambda i, j: (i, 0))
    return pl.pallas_call(
        _outproj_kernel,
        out_shape=jax.ShapeDtypeStruct((n, d), F32),
        grid=(n // tm, d // tn),
        in_specs=[part, part, part, part,
                  pl.BlockSpec((1, mix), lambda i, j: (0, 0)),
                  pl.BlockSpec((mix, tn), lambda i, j: (0, j)),
                  pl.BlockSpec((tm, tn), lambda i, j: (i, j))],
        out_specs=pl.BlockSpec((tm, tn), lambda i, j: (i, j)),
        scratch_shapes=[pltpu.VMEM((tm, mix), CD)],
        compiler_params=_cparams("parallel", "arbitrary"),
        name="outproj",
    )(*parts, gn.reshape(1, mix), w, h)


def _mlp_kernel(*refs, final):
    if final:
        h_ref, g_ref, wu_ref, wd_ref, gf_ref, o_ref, u_ref = refs
    else:
        h_ref, g_ref, wu_ref, wd_ref, o_ref, u_ref = refs
    c = pl.program_id(1)

    @pl.when(c == 0)
    def _():
        x = h_ref[...]
        u_ref[...] = _rms(x, g_ref[...]).astype(u_ref.dtype)
        o_ref[...] = x

    m = jnp.dot(u_ref[...], wu_ref[...], preferred_element_type=F32)
    a = jnp.square(jnp.maximum(m, 0.0)).astype(CD)
    o_ref[...] += jnp.dot(a, wd_ref[...], preferred_element_type=F32)

    if final:
        @pl.when(c == pl.num_programs(1) - 1)
        def _():
            o_ref[...] = _rms(o_ref[...], gf_ref[...])


def _mlp(h, g, wu, wd, gf, tm, tf):
    n, d = h.shape
    dff = wu.shape[1]
    final = gf is not None
    in_specs = [pl.BlockSpec((tm, d), lambda i, c: (i, 0)),
                pl.BlockSpec((1, d), lambda i, c: (0, 0)),
                pl.BlockSpec((d, tf), lambda i, c: (0, c)),
                pl.BlockSpec((tf, d), lambda i, c: (c, 0))]
    args = [h, g.reshape(1, d), wu, wd]
    if final:
        in_specs.append(pl.BlockSpec((1, d), lambda i, c: (0, 0)))
        args.append(gf.reshape(1, d))
    return pl.pallas_call(
        functools.partial(_mlp_kernel, final=final),
        out_shape=jax.ShapeDtypeStruct((n, d), F32),
        grid=(n // tm, dff // tf),
        in_specs=in_specs,
        out_specs=pl.BlockSpec((tm, d), lambda i, c: (i, 0)),
        scratch_shapes=[pltpu.VMEM((tm, d), CD)],
        compiler_params=_cparams("parallel", "arbitrary"),
        name="mlp",
    )(*args)


def _t5_bucket_np(dist):
    max_exact = REL_BUCKETS // 2
    d = np.maximum(dist, 0)
    ratio = np.log(np.maximum(d, 1).astype(np.float32) / np.float32(max_exact)) / np.float32(
        math.log(REL_MAX_DIST / max_exact))
    large = max_exact + (ratio * np.float32(REL_BUCKETS - max_exact)).astype(np.int32)
    large = np.minimum(large, REL_BUCKETS - 1)
    return np.where(d < max_exact, d, large)


def _bias_tables(rel_bias):
    tbl = rel_bias.astype(F32).T.reshape(2, NSA_KV_HEADS, NSA_REP, REL_BUCKETS)
    tbl_nsa, tbl_swa = tbl[0], tbl[1]
    i = np.arange(Q_BLOCK)[:, None]
    far_bucket = REL_BUCKETS - 1

    def banded(tb, window):
        jk = np.arange(window + Q_BLOCK)[None, :]
        dist = i + window - jk
        ok = (dist >= 0) & (dist < window)
        vals = tb[:, :, _t5_bucket_np(dist)]
        vals = jnp.where(jnp.asarray(ok)[None, None], vals, NEG_INF)
        return vals.reshape(NSA_KV_HEADS, NSA_REP * Q_BLOCK, window + Q_BLOCK)

    bm_win = banded(tbl_nsa, NSA_WINDOW)
    bm_swa = banded(tbl_swa, SWA_WINDOW)

    far = tbl_nsa[:, :, far_bucket]
    jk = np.arange(2 * Q_BLOCK)[None, :]
    tiles = []
    for first_key_back in (0, Q_BLOCK):
        dist = i + first_key_back - jk
        delta = tbl_nsa[:, :, _t5_bucket_np(dist)] - far[:, :, None, None]
        tiles.append(jnp.where(jnp.asarray(dist >= 0)[None, None], delta, NEG_INF))
    near = jnp.stack(tiles, axis=1).reshape(NSA_KV_HEADS, 2, NSA_REP * Q_BLOCK, 2 * Q_BLOCK)
    far_rows = jnp.broadcast_to(far[:, :, None], (NSA_KV_HEADS, NSA_REP, Q_BLOCK)).reshape(NSA_KV_HEADS, -1)
    hi, lo_ = _split_parts(far_rows, 2)
    farq = jnp.zeros((NSA_KV_HEADS, NSA_REP * Q_BLOCK, LANES), CD)
    farq = farq.at[:, :, HEAD_DIM].set(hi).at[:, :, HEAD_DIM + 1].set(lo_)

    u = np.arange(16)[None, :]
    dist_c = i - 16 * u + 113
    band = tbl_nsa[:, :, _t5_bucket_np(dist_c)]
    band = jnp.where(jnp.asarray(dist_c >= 0)[None, None], band, MASK_NEG)
    band = band.reshape(NSA_KV_HEADS, NSA_REP * Q_BLOCK, 16)
    bh, bl = _split_parts(band, 2)
    pq = jnp.zeros((NSA_KV_HEADS, NSA_REP * Q_BLOCK, LANES), CD)
    pq = pq.at[:, :, 0:16].set(bh).at[:, :, 16:32].set(bl)
    pq = pq.at[:, :, 32].set(hi).at[:, :, 33].set(lo_).at[:, :, 34].set(jnp.asarray(MASK_NEG, CD))
    return bm_win, bm_swa, near, farq, pq


def _overlap_matrix(nc_pad, ns):
    ci = np.arange(nc_pad)[:, None]
    sj = np.arange(LANES)[None, :]
    ov = ((ci * NSA_CMP_STRIDE + NSA_CMP_LEN - 1 >= sj * NSA_SEL_LEN)
          & (ci * NSA_CMP_STRIDE <= sj * NSA_SEL_LEN + NSA_SEL_LEN - 1) & (sj < ns))
    return jnp.asarray(ov.astype(np.float32), CD)


def _layout_w_in(w):
    w = w.astype(F32)
    sc = HEAD_DIM ** -0.5
    z = lambda k: jnp.zeros((w.shape[0], k), F32)
    main = jnp.concatenate([w[:, 544:1056] * sc, w[:, 1056:2080], w[:, 2088:2600] * sc, w[:, 3392:3904] * sc,
                            w[:, 2600:3368], w[:, 3904:4160]], axis=1)
    kr = w[:, 512:544]
    blk_a = jnp.concatenate([w[:, 2080:2088], w[:, 3368:3392], z(32), kr, z(32)], axis=1)
    blk_b = jnp.concatenate([z(64), -kr[:, 16:], kr[:, :16], z(32)], axis=1)
    small = jnp.concatenate([w[:, 0:512], blk_a, blk_b], axis=1)
    return main.astype(CD), small.astype(CD)


_FOX_Q, _FOX_K512, _FOX_V = 0, 1, 8
_NSA_Q256, _SWA_Q256 = 6, 8
_KVC_COL = 2560
_KSEL, _VSEL, _KWIN, _VWIN, _KSWA, _VSWA = 22, 23, 24, 25, 26, 27


def _layout_mla(w_uq, w_ukv):
    w3 = w_uq.astype(F32).reshape(MLA_Q_RANK, GROUP_HEADS, MLA_NOPE + MLA_ROPE)
    nope, rp = w3[:, :, :MLA_NOPE], w3[:, :, MLA_NOPE:]
    half = MLA_ROPE // 2
    sw = jnp.concatenate([-rp[:, :, half:], rp[:, :, :half]], axis=-1)
    z = lambda k: jnp.zeros((MLA_Q_RANK, GROUP_HEADS, k), F32)
    plain = jnp.concatenate([nope, rp, z(32)], axis=-1).reshape(MLA_Q_RANK, -1)
    swapped = jnp.concatenate([z(64), sw, z(32)], axis=-1).reshape(MLA_Q_RANK, -1)
    wq = jnp.concatenate([plain, swapped], axis=1)
    k3 = w_ukv.astype(F32).reshape(MLA_KV_RANK, GROUP_HEADS, 2 * HEAD_DIM)
    wk = jnp.concatenate([k3[:, :, :MLA_NOPE], jnp.zeros((MLA_KV_RANK, GROUP_HEADS, HEAD_DIM), F32)],
                         axis=-1).reshape(MLA_KV_RANK, -1)
    wv = k3[:, :, MLA_NOPE:].reshape(MLA_KV_RANK, -1)
    return wq.astype(CD), wk.astype(CD), wv.astype(CD)


def _rope_tables(t):
    inv = ROPE_THETA ** (-jnp.arange(0, MLA_ROPE, 2, dtype=F32) / MLA_ROPE)
    ang = jnp.arange(t).astype(F32)[:, None] * inv[None, :]
    cc = jnp.concatenate([jnp.cos(ang)] * 2, axis=1)
    ss = jnp.concatenate([jnp.sin(ang)] * 2, axis=1)
    one, z64, z32 = jnp.ones((t, 64), F32), jnp.zeros((t, 64), F32), jnp.zeros((t, 32), F32)
    return (jnp.concatenate([one, cc, z32], axis=1), jnp.concatenate([z64, ss, z32], axis=1),
            jnp.concatenate([z64, cc, z32], axis=1))


def kernel(x, norm_attn, w_in, mla_q_norm, mla_w_uq, mla_kv_norm, mla_w_ukv, fox_b_f, nsa_cmp_pos,
           nsa_cmp_w1, nsa_cmp_w2, swa_sinks, group_norm, w_out, norm_mlp, w_up, w_down, rel_bias,
           final_norm):
    b, t, d = x.shape
    n = b * t
    depth = w_in.shape[0]
    assert t % 1024 == 0 and d == N_GROUPS * GROUP_WIDTH
    tm = 1024 if n % 1024 == 0 else 512
    tq, tk = 256, 512
    nr = t // NSA_CMP_STRIDE

    cosq, sinq, cosk = _rope_tables(t)
    bm_win, bm_swa, nearbias, farq, pq = _bias_tables(rel_bias)
    overlap = _overlap_matrix(nr, t // NSA_SEL_LEN)

    h = x.reshape(n, d).astype(F32)
    for l in range(depth):
        w_main, w_small = _layout_w_in(w_in[l])
        main = _norm_matmul(h, norm_attn[l], w_main, CD, tm, 512)
        small = _norm_matmul(h, norm_attn[l], w_small, F32, tm, 768)
        main3 = main.reshape(b, t, -1)
        small3 = small.reshape(b, t, -1)

        wq, wk, wv = _layout_mla(mla_w_uq[l], mla_w_ukv[l])
        q_m, k_m, v_m = _mla_prep(small, mla_q_norm[l], mla_kv_norm[l], wq, wk, wv, cosq, sinq, cosk, t, 512)
        o_mla = _dense_attn(q_m.reshape(b, t, -1), k_m.reshape(b, t, -1), v_m.reshape(b, t, -1), None,
                            fox=False, q_col=0, v_col=0, scale=(MLA_NOPE + MLA_ROPE) ** -0.5, tq=tq, tk=tk)

        k_aug, fq = _fox_prep(small3, fox_b_f[l], main3, _FOX_K512, 512)
        o_fox = _dense_attn(main3, k_aug, main3, fq, fox=True, q_col=_FOX_Q, v_col=_FOX_V, scale=1.0,
                            tq=tq, tk=tk)

        kv16 = main3[:, :, _KVC_COL:_KVC_COL + 256].reshape(b, nr, NSA_CMP_STRIDE * 256)
        cmp = _nsa_compress(kv16, nsa_cmp_pos[l], nsa_cmp_w1[l], nsa_cmp_w2[l])
        o_cmp, maskfeat = _cmp_select(main3, _NSA_Q256, cmp, pq, overlap)
        o_win = _banded_attn(main3, _NSA_Q256, _KWIN, _VWIN, bm_win, None, NSA_WINDOW)
        o_nsa = _sel_attn(main3, _NSA_Q256, _KSEL, _VSEL, maskfeat, farq, nearbias, o_cmp, o_win, small3, tk)

        sink_rows = jnp.broadcast_to(
            swa_sinks[l].astype(F32).reshape(NSA_KV_HEADS, NSA_REP, 1, 1),
            (NSA_KV_HEADS, NSA_REP, Q_BLOCK, LANES)).reshape(NSA_KV_HEADS, NSA_REP * Q_BLOCK, LANES)
        o_swa = _banded_attn(main3, _SWA_Q256, _KSWA, _VSWA, bm_swa, sink_rows, SWA_WINDOW)

        parts = [o.reshape(n, GROUP_WIDTH) for o in (o_mla, o_fox, o_nsa, o_swa)]
        h = _outproj(parts, group_norm[l], w_out[l].astype(CD), h, tm, 512)
        gf = final_norm if l == depth - 1 else None
        h = _mlp(h, norm_mlp[l], w_up[l].astype(CD), w_down[l].astype(CD), gf, 512, 1024)
    return h.reshape(b, t, d).astype(x.dtype)
```

```python
import functools
import math

import numpy as np
import jax
import jax.numpy as jnp
from jax import lax
from jax.experimental import pallas as pl
from jax.experimental.pallas import tpu as pltpu

HEAD_DIM = 64
GROUP_HEADS = 8
GROUP_WIDTH = GROUP_HEADS * HEAD_DIM
N_GROUPS = 4
Q_BLOCK = 128
EPS = 1e-6
NEG_INF = -1e30
BIG = 1e9

MLA_Q_RANK = 384
MLA_KV_RANK = 128
MLA_NOPE = 64
MLA_ROPE = 32
ROPE_THETA = 10000.0

NSA_KV_HEADS = 2
NSA_REP = GROUP_HEADS // NSA_KV_HEADS
NSA_CMP_LEN = 32
NSA_CMP_STRIDE = 16
NSA_SEL_LEN = 64
NSA_TOP_N = 8
NSA_WINDOW = 256
SWA_WINDOW = 128

REL_BUCKETS = 32
REL_MAX_DIST = 128

V7X_VMEM_BYTES = 64 * 2**20
VMEM_LIMIT = (V7X_VMEM_BYTES * 7) // 8
LANES = 128
CD = jnp.bfloat16
F32 = jnp.float32
MASK_NEG = -(2.0 ** 80)
M_INIT = -(2.0 ** 100)
TILE_GROUP = 4
LOG2E = math.log2(math.e)
MLA_SCORE_SCALE = (MLA_NOPE + MLA_ROPE) ** -0.5 * LOG2E
_TRANS_B = (((1,), (1,)), ((), ()))


def _cparams(*sem):
    return pltpu.CompilerParams(dimension_semantics=sem, vmem_limit_bytes=VMEM_LIMIT)


def _split_parts(x, n):
    parts, r = [], x
    for _ in range(n):
        p = r.astype(CD)
        parts.append(p)
        r = r - p.astype(F32)
    return parts


def _sweep_tiles(fn, n_tiles):
    groups = n_tiles // TILE_GROUP

    def grouped(i, _):
        fn(i * TILE_GROUP, TILE_GROUP)
        return 0

    lax.fori_loop(0, groups, grouped, 0)
    base = groups * TILE_GROUP
    rem = n_tiles - base

    @pl.when(rem >= 2)
    def _():
        fn(base, 2)

    @pl.when((rem & 1) == 1)
    def _():
        fn(base + (rem & 2), 1)


def _rms(x, g):
    return x * lax.rsqrt(jnp.mean(x * x, axis=-1, keepdims=True) + EPS) * g


def _norm_matmul_kernel(x_ref, g_ref, w_ref, o_ref, u_ref):
    @pl.when(pl.program_id(1) == 0)
    def _():
        u_ref[...] = _rms(x_ref[...], g_ref[...]).astype(u_ref.dtype)

    o_ref[...] = jnp.dot(u_ref[...], w_ref[...], preferred_element_type=F32).astype(o_ref.dtype)


def _norm_matmul(x, g, w, out_dtype, tm, tn):
    n, d = x.shape
    nc = w.shape[1]
    return pl.pallas_call(
        _norm_matmul_kernel,
        out_shape=jax.ShapeDtypeStruct((n, nc), out_dtype),
        grid=(n // tm, nc // tn),
        in_specs=[
            pl.BlockSpec((tm, d), lambda i, j: (i, 0)),
            pl.BlockSpec((1, d), lambda i, j: (0, 0)),
            pl.BlockSpec((d, tn), lambda i, j: (0, j)),
        ],
        out_specs=pl.BlockSpec((tm, tn), lambda i, j: (i, j)),
        scratch_shapes=[pltpu.VMEM((tm, d), CD)],
        compiler_params=_cparams("parallel", "arbitrary"),
        name="norm_matmul",
    )(x, g.reshape(1, d), w)


def _mla_prep_kernel(sm_ref, qn_ref, kvn_ref, wq_ref, wk_ref, wv_ref, cq_ref, sq_ref, ck_ref,
                     q_ref, k_ref, v_ref):
    sm = sm_ref[...]
    nq = _rms(sm[:, :MLA_Q_RANK], qn_ref[...]).astype(CD)
    nkv = _rms(sm[:, MLA_Q_RANK:MLA_Q_RANK + MLA_KV_RANK], kvn_ref[...]).astype(CD)
    blk_a = sm[:, 512:640]
    blk_b = sm[:, 640:768]
    cosq, sinq, cosk = cq_ref[...], sq_ref[...], ck_ref[...]
    hw = GROUP_HEADS * LANES
    qq = jnp.dot(nq, wq_ref[...], preferred_element_type=F32)
    kk = jnp.dot(nkv, wk_ref[...], preferred_element_type=F32)
    kpe = blk_a * cosk + blk_b * sinq
    for h in range(GROUP_HEADS):
        sl = slice(h * LANES, (h + 1) * LANES)
        qh = qq[:, sl] * cosq + qq[:, hw + h * LANES:hw + (h + 1) * LANES] * sinq
        q_ref[:, sl] = (qh * MLA_SCORE_SCALE).astype(q_ref.dtype)
        k_ref[:, sl] = (kk[:, sl] + kpe).astype(k_ref.dtype)
    v_ref[...] = jnp.dot(nkv, wv_ref[...], preferred_element_type=F32).astype(v_ref.dtype)


def _mla_prep(small, qn, kvn, wq, wk, wv, cosq, sinq, cosk, t, tm):
    n = small.shape[0]
    tb = t // tm
    hw = GROUP_HEADS * LANES
    full = lambda a: pl.BlockSpec(a.shape, lambda i: (0,) * a.ndim)
    tab = pl.BlockSpec((tm, LANES), lambda i: (i % tb, 0))
    qn = qn.reshape(1, -1)
    kvn = kvn.reshape(1, -1)
    return pl.pallas_call(
        _mla_prep_kernel,
        out_shape=(jax.ShapeDtypeStruct((n, hw), CD), jax.ShapeDtypeStruct((n, hw), CD),
                   jax.ShapeDtypeStruct((n, GROUP_WIDTH), CD)),
        grid=(n // tm,),
        in_specs=[pl.BlockSpec((tm, small.shape[1]), lambda i: (i, 0)), full(qn), full(kvn),
                  full(wq), full(wk), full(wv), tab, tab, tab],
        out_specs=(pl.BlockSpec((tm, hw), lambda i: (i, 0)), pl.BlockSpec((tm, hw), lambda i: (i, 0)),
                   pl.BlockSpec((tm, GROUP_WIDTH), lambda i: (i, 0))),
        compiler_params=_cparams("parallel"),
        name="mla_prep",
    )(small, qn, kvn, wq, wk, wv, cosq, sinq, cosk)


def _fox_prep_kernel(fl_ref, bf_ref, k_ref, efq_ref, ekf_ref, kaug_ref, fq_ref, carry_ref, *, tc):
    @pl.when(pl.program_id(1) == 0)
    def _():
        carry_ref[...] = jnp.zeros_like(carry_ref)

    x = fl_ref[...] + bf_ref[...]
    logf = -(jnp.maximum(-x, 0.0) + jnp.log1p(jnp.exp(-jnp.abs(x))))
    lane = lax.broadcasted_iota(jnp.int32, logf.shape, 1)
    logf = jnp.where(lane < GROUP_HEADS, logf, 0.0)
    row = lax.broadcasted_iota(jnp.int32, (tc, tc), 0)
    col = lax.broadcasted_iota(jnp.int32, (tc, tc), 1)
    tri = jnp.where(row >= col, 1.0, 0.0).astype(CD)
    cs = jnp.zeros(logf.shape, F32)
    for part in _split_parts(logf, 3):
        cs = cs + jnp.dot(tri, part, preferred_element_type=F32)
    fc = cs + carry_ref[...]
    carry_ref[...] = fc[tc - 1:tc, :]
    parts = _split_parts(fc * LOG2E, 3)
    fq = jnp.zeros(fq_ref.shape, F32)
    kf = jnp.zeros(fq_ref.shape, F32)
    for i, part in enumerate(parts):
        fq = fq + jnp.dot(part, efq_ref[...], preferred_element_type=F32)
        kf = kf + jnp.dot(part, ekf_ref[i], preferred_element_type=F32)
    fq_ref[...] = fq
    k = k_ref[...]
    for p in range(GROUP_HEADS // 2):
        kaug_ref[:, 2 * p * LANES:(2 * p + 1) * LANES] = k[:, p * LANES:(p + 1) * LANES]
        kaug_ref[:, (2 * p + 1) * LANES:(2 * p + 2) * LANES] = kf[:, p * LANES:(p + 1) * LANES].astype(kaug_ref.dtype)


def _fox_prep(small3, b_f, main3, k_col, tc):
    b, t, _ = small3.shape
    pairs = GROUP_HEADS // 2
    efq = np.zeros((LANES, pairs * LANES), np.float32)
    ekf = np.zeros((3, LANES, pairs * LANES), np.float32)
    for h in range(GROUP_HEADS):
        p, a = divmod(h, 2)
        efq[h, p * LANES + a] = 1.0
        for i in range(3):
            ekf[i, h, p * LANES + 3 * a + i] = -1.0
    bf = jnp.zeros((1, LANES), F32).at[0, :GROUP_HEADS].set(b_f.astype(F32))
    return pl.pallas_call(
        functools.partial(_fox_prep_kernel, tc=tc),
        out_shape=(jax.ShapeDtypeStruct((b, t, 2 * GROUP_WIDTH), CD),
                   jax.ShapeDtypeStruct((b, t, GROUP_WIDTH), F32)),
        grid=(b, t // tc),
        in_specs=[pl.BlockSpec((None, tc, LANES), lambda i, j: (i, j, 4)),
                  pl.BlockSpec((1, LANES), lambda i, j: (0, 0)),
                  pl.BlockSpec((None, tc, GROUP_WIDTH), lambda i, j: (i, j, k_col)),
                  pl.BlockSpec(efq.shape, lambda i, j: (0, 0)),
                  pl.BlockSpec(ekf.shape, lambda i, j: (0, 0, 0))],
        out_specs=(pl.BlockSpec((None, tc, 2 * GROUP_WIDTH), lambda i, j: (i, j, 0)),
                   pl.BlockSpec((None, tc, GROUP_WIDTH), lambda i, j: (i, j, 0))),
        scratch_shapes=[pltpu.VMEM((1, LANES), F32)],
        compiler_params=_cparams("arbitrary", "arbitrary"),
        name="fox_prep",
    )(small3, bf, main3, jnp.asarray(efq, CD), jnp.asarray(ekf, CD))


def _dense_attn_kernel(*refs, fox, tq):
    if fox:
        q_ref, k_ref, v_ref, fq_ref, o_ref, va_ref, qa_ref, s_ref, mx_ref, acc_ref = refs
    else:
        q_ref, k_ref, v_ref, o_ref, va_ref, qa_ref, s_ref, mx_ref, acc_ref = refs
    qi = pl.program_id(2)
    t = k_ref.shape[0]
    nch = tq // LANES

    @pl.when(qi == 0)
    def _():
        chunk = 1024

        def body(i, _):
            off = pl.multiple_of(i * chunk, chunk)
            va_ref[pl.ds(off, chunk), 0:LANES] = v_ref[pl.ds(off, chunk), :]
            va_ref[pl.ds(off, chunk), LANES:2 * LANES] = jnp.ones((chunk, LANES), va_ref.dtype)
            return 0

        lax.fori_loop(0, t // chunk, body, 0)

    half = tq // 2
    outs = []
    for h in range(2):
        q = q_ref[...]
        if fox:
            qa_ref[...] = jnp.concatenate(
                [q * _lane_mask(tq, h * HEAD_DIM, (h + 1) * HEAD_DIM, q.dtype), _lane_mask(tq, 3 * h, 3 * h + 3, q.dtype)],
                axis=1)
        else:
            qa_ref[...] = q[:, h * LANES:(h + 1) * LANES]

        def score_tiles(j0, count, diagonal, h=h):
            mx = mx_ref[...]
            for gi in range(count):
                off = pl.multiple_of((j0 + gi) * tq, tq)
                for hf in range(2):
                    rows = pl.ds(off + hf * half, half)
                    k = k_ref[rows, :] if fox else k_ref[rows, pl.ds(h * LANES, LANES)]
                    s = lax.dot_general(qa_ref[...], k, _TRANS_B, preferred_element_type=F32)
                    if diagonal:
                        col = hf * half + lax.broadcasted_iota(jnp.int32, (tq, half), 1)
                        s = jnp.where(col <= lax.broadcasted_iota(jnp.int32, (tq, half), 0), s, NEG_INF)
                    s_ref[j0 + gi, :, hf * half:(hf + 1) * half] = s
                    for i in range(half // LANES):
                        mx = jnp.maximum(mx, s[:, i * LANES:(i + 1) * LANES])
            mx_ref[...] = mx

        def value_tiles(j0, count):
            mc = mx_ref[...]
            ps = []
            for gi in range(count):
                s = s_ref[j0 + gi]
                ps += [jnp.exp2(s[:, i * LANES:(i + 1) * LANES] - mc) for i in range(nch)]
            off = pl.multiple_of(j0 * tq, tq)
            acc_ref[...] += jnp.dot(jnp.concatenate(ps, axis=1).astype(CD), va_ref[pl.ds(off, count * tq), :],
                                    preferred_element_type=F32)

        mx_ref[...] = jnp.full(mx_ref.shape, M_INIT, F32)
        _sweep_tiles(lambda j0, c: score_tiles(j0, c, False), qi)
        score_tiles(qi, 1, True)
        m = jnp.max(mx_ref[...], axis=1, keepdims=True)
        if fox:
            f = fq_ref[:, h:h + 1]
            m = (m + f) - f
        mx_ref[...] = jnp.broadcast_to(m, (tq, LANES))

        acc_ref[...] = jnp.zeros(acc_ref.shape, F32)
        _sweep_tiles(value_tiles, qi + 1)
        acc = acc_ref[...]
        outs.append(acc[:, :LANES] * (1.0 / acc[:, LANES:LANES + 1]))
    lane = lax.broadcasted_iota(jnp.int32, (tq, LANES), 1)
    o_ref[...] = jnp.where(lane < HEAD_DIM, outs[0], outs[1]).astype(o_ref.dtype)


def _dense_attn(q3, k3, v3, fq3, *, fox, q_col, v_col, tq):
    b, t, _ = q3.shape
    assert t % tq == 0 and t % 1024 == 0
    pairs = GROUP_HEADS // 2
    qw = LANES if fox else 2 * LANES
    kd = 2 * LANES if fox else LANES
    in_specs = [pl.BlockSpec((None, tq, qw), lambda i, p, j: (i, j, q_col + p)),
                pl.BlockSpec((None, t, 2 * LANES), lambda i, p, j: (i, 0, p)),
                pl.BlockSpec((None, t, LANES), lambda i, p, j: (i, 0, v_col + p))]
    args = [q3, k3, v3]
    if fox:
        in_specs.append(pl.BlockSpec((None, tq, LANES), lambda i, p, j: (i, j, p)))
        args.append(fq3)
    return pl.pallas_call(
        functools.partial(_dense_attn_kernel, fox=fox, tq=tq),
        out_shape=jax.ShapeDtypeStruct((b, t, GROUP_WIDTH), F32),
        grid=(b, pairs, t // tq),
        in_specs=in_specs,
        out_specs=pl.BlockSpec((None, tq, LANES), lambda i, p, j: (i, j, p)),
        scratch_shapes=[pltpu.VMEM((t, 2 * LANES), CD), pltpu.VMEM((tq, kd), CD),
                        pltpu.VMEM((t // tq, tq, tq), F32), pltpu.VMEM((tq, LANES), F32),
                        pltpu.VMEM((tq, 2 * LANES), F32)],
        compiler_params=_cparams("arbitrary", "arbitrary", "arbitrary"),
        name="fox_attn" if fox else "mla_attn",
    )(*args)


def _gelu_tanh(x):
    return 0.5 * x * (1.0 + jnp.tanh(math.sqrt(2.0 / math.pi) * (x + 0.044715 * (x * x * x))))


def _compress_kernel(r_ref, w1a_ref, w1b_ref, pa_ref, pb_ref, w2_ref, o_ref):
    r = r_ref[...]
    nr = r.shape[0]
    ya = jnp.dot(r, w1a_ref[...], preferred_element_type=F32)
    yb = jnp.dot(r, w1b_ref[...], preferred_element_type=F32)
    pc = (jnp.dot(pa_ref[...], w1a_ref[...], preferred_element_type=F32)
          + jnp.dot(pb_ref[...], w1b_ref[...], preferred_element_type=F32))[0:1, :]
    pre = ya + pltpu.roll(yb, nr - 1, 0) + pc
    h1 = _gelu_tanh(pre).astype(CD)
    rowi = lax.broadcasted_iota(jnp.int32, (nr, LANES), 0)
    for g in range(NSA_KV_HEADS):
        y = jnp.dot(h1, w2_ref[g], preferred_element_type=F32)
        o_ref[g] = jnp.where(rowi < nr - 1, y, 0.0).astype(o_ref.dtype)


def _nsa_compress(kv16, cmp_pos, cmp_w1, cmp_w2):
    b, nr, kw = kv16.shape
    half = NSA_CMP_LEN // 2
    eye2 = jnp.eye(2, dtype=F32)
    w1r = cmp_w1.astype(F32).reshape(2, NSA_CMP_LEN, HEAD_DIM, HEAD_DIM)
    expand = lambda w: jnp.einsum('klde,kw,gh->klwgdhe', w, eye2, eye2).reshape(2, kw, LANES).astype(CD)
    w1a, w1b = expand(w1r[:, :half]), expand(w1r[:, half:])
    posr = cmp_pos.astype(F32)
    tile = lambda p: jnp.broadcast_to(p[:, None, :, None, None, :], (2, 8, half, 2, 2, HEAD_DIM)).reshape(2, 8, kw).astype(CD)
    pa, pb = tile(posr[:, :half]), tile(posr[:, half:])
    w2 = cmp_w2.astype(F32)
    w2d = jnp.zeros((2, NSA_KV_HEADS, LANES, LANES), F32)
    for g in range(NSA_KV_HEADS):
        blk = jnp.concatenate([w2, w2], axis=2)
        w2d = w2d.at[:, g, g * HEAD_DIM:(g + 1) * HEAD_DIM, :].set(blk)
    w2d = w2d.astype(CD)
    return pl.pallas_call(
        _compress_kernel,
        out_shape=jax.ShapeDtypeStruct((b, 2, NSA_KV_HEADS, nr, LANES), CD),
        grid=(b, 2),
        in_specs=[pl.BlockSpec((None, nr, kw), lambda i, k: (i, 0, 0)),
                  pl.BlockSpec((None, kw, LANES), lambda i, k: (k, 0, 0)),
                  pl.BlockSpec((None, kw, LANES), lambda i, k: (k, 0, 0)),
                  pl.BlockSpec((None, 8, kw), lambda i, k: (k, 0, 0)),
                  pl.BlockSpec((None, 8, kw), lambda i, k: (k, 0, 0)),
                  pl.BlockSpec((None, NSA_KV_HEADS, LANES, LANES), lambda i, k: (k, 0, 0, 0))],
        out_specs=pl.BlockSpec((None, None, NSA_KV_HEADS, nr, LANES), lambda i, k: (i, k, 0, 0, 0)),
        compiler_params=_cparams("parallel", "arbitrary"),
        name="nsa_compress",
    )(kv16, w1a, w1b, pa, pb, w2d)


def _lane_mask(rows, lo, hi, dtype):
    lane = lax.broadcasted_iota(jnp.int32, (rows, LANES), 1)
    return jnp.where((lane >= lo) & (lane < hi), 1.0, 0.0).astype(dtype)


def _stack_heads_native(q):
    lo = _lane_mask(Q_BLOCK, 0, HEAD_DIM, q.dtype)
    hi = _lane_mask(Q_BLOCK, HEAD_DIM, LANES, q.dtype)
    c0, c1 = q[:, :LANES], q[:, LANES:]
    return jnp.concatenate([c0 * lo, c0 * hi, c1 * lo, c1 * hi], axis=0)


def _unstack_heads(o):
    lane = lax.broadcasted_iota(jnp.int32, (Q_BLOCK, LANES), 1)
    lo = lane < HEAD_DIM
    return jnp.concatenate([jnp.where(lo, o[0:128], o[128:256]), jnp.where(lo, o[256:384], o[384:512])], axis=1)


def _dup_matrix(g):
    i = lax.broadcasted_iota(jnp.int32, (LANES, LANES), 0)
    j = lax.broadcasted_iota(jnp.int32, (LANES, LANES), 1)
    return jnp.where(i == g * HEAD_DIM + (j & (HEAD_DIM - 1)), 1.0, 0.0).astype(CD)


def _fill_rows(dst_ref, row0, src_ref, mat, chunk=1024):
    n = src_ref.shape[0]

    def body(i, _):
        off = pl.multiple_of(i * chunk, chunk)
        dst_ref[pl.ds(row0 + off, chunk), 0:LANES] = jnp.dot(
            src_ref[pl.ds(off, chunk), :], mat, preferred_element_type=F32).astype(dst_ref.dtype)
        return 0

    lax.fori_loop(0, n // chunk, body, 0)


def _cmp_select_kernel(q_ref, kc_ref, vc_ref, pq_ref, ov_ref, o_ref, mf_ref, *, nq):
    nc = kc_ref.shape[0]
    ci = lax.broadcasted_iota(jnp.int32, (nc, LANES), 0)
    f = lax.broadcasted_iota(jnp.int32, (nc, LANES), 1)
    j = lax.broadcasted_iota(jnp.int32, (Q_BLOCK, LANES), 1)
    i = lax.broadcasted_iota(jnp.int32, (Q_BLOCK, LANES), 0)
    jf = j.astype(F32)
    one_if = lambda cond: jnp.where(cond, 1.0, 0.0)
    for u in range(nq):
        n = pl.program_id(2) * nq + u
        rs = slice(u * Q_BLOCK, (u + 1) * Q_BLOCK)
        qaug = jnp.concatenate([_stack_heads_native(q_ref[rs, :]), pq_ref[...]], axis=1)
        uc = ci - 8 * n + 9
        feat = jnp.where(f < 32, one_if(uc == (f & 15)),
                         jnp.where(f < 34, one_if(uc < 0), one_if((f == 34) & (uc > 15)))).astype(CD)
        kaug = jnp.concatenate([kc_ref[...], feat], axis=1)
        s = lax.dot_general(qaug, kaug, _TRANS_B, preferred_element_type=F32)
        m = jnp.max(s, axis=1, keepdims=True)
        p = jnp.exp2(s - m)
        l = jnp.sum(p, axis=1, keepdims=True)
        pc = p * jnp.where(m > 0.5 * MASK_NEG, 1.0 / l, 0.0)
        o = jnp.dot(pc.astype(CD), vc_ref[...], preferred_element_type=F32)
        o_ref[rs, :] = _unstack_heads(o)
        pcs = pc[0:128] + pc[128:256] + pc[256:384] + pc[384:512]
        imp = jnp.dot(pcs.astype(CD), ov_ref[...], preferred_element_type=F32)
        cur = 2 * n + jnp.where(i >= NSA_SEL_LEN, 1, 0)
        causal = j <= cur
        forced = (j == 0) | (j == cur) | (j == cur - 1)
        sel = one_if(forced & causal)
        score = jnp.where(causal & jnp.logical_not(forced), imp, -3.0e38)
        for _ in range(NSA_TOP_N - 3):
            mx = jnp.max(score, axis=1, keepdims=True)
            idx = jnp.min(jnp.where(score == mx, jf, float(LANES)), axis=1, keepdims=True)
            pick = jf == idx
            sel = jnp.where(pick, 1.0, sel)
            score = jnp.where(pick, -3.0e38, score)
        mf_ref[rs, :] = jnp.where(sel > 0.5, 0.0, MASK_NEG).astype(mf_ref.dtype)


def _cmp_select(main3, q_col, cmp, pq, overlap, nq):
    b, t, _ = main3.shape
    nc = cmp.shape[3]
    tq = nq * Q_BLOCK
    assert t % tq == 0
    return pl.pallas_call(
        functools.partial(_cmp_select_kernel, nq=nq),
        out_shape=(jax.ShapeDtypeStruct((b, t, GROUP_WIDTH), F32),
                   jax.ShapeDtypeStruct((b, t, NSA_KV_HEADS * LANES), CD)),
        grid=(b, NSA_KV_HEADS, t // tq),
        in_specs=[pl.BlockSpec((None, tq, 2 * LANES), lambda i, g, n: (i, n, q_col + g)),
                  pl.BlockSpec((None, None, None, nc, LANES), lambda i, g, n: (i, 0, g, 0, 0)),
                  pl.BlockSpec((None, None, None, nc, LANES), lambda i, g, n: (i, 1, g, 0, 0)),
                  pl.BlockSpec((None, NSA_REP * Q_BLOCK, LANES), lambda i, g, n: (g, 0, 0)),
                  pl.BlockSpec(overlap.shape, lambda i, g, n: (0, 0))],
        out_specs=(pl.BlockSpec((None, tq, 2 * LANES), lambda i, g, n: (i, n, g)),
                   pl.BlockSpec((None, tq, LANES), lambda i, g, n: (i, n, g))),
        compiler_params=_cparams("parallel", "parallel", "arbitrary"),
        name="nsa_cmp_select",
    )(main3, cmp, cmp, pq, overlap)


def _banded_kernel(*refs, window, sinks, nq):
    if sinks:
        q_ref, k_ref, v_ref, bm_ref, sk_ref, o_ref, kp_ref, vp_ref = refs
    else:
        q_ref, k_ref, v_ref, bm_ref, o_ref, kp_ref, vp_ref = refs
    g = pl.program_id(1)
    kw = window + Q_BLOCK

    @pl.when(pl.program_id(2) == 0)
    def _():
        dup = _dup_matrix(g)
        kp_ref[0:window, :] = jnp.zeros((window, LANES), kp_ref.dtype)
        vp_ref[0:window, :] = jnp.zeros((window, LANES), vp_ref.dtype)
        _fill_rows(kp_ref, window, k_ref, dup)
        _fill_rows(vp_ref, window, v_ref, dup)

    for u in range(nq):
        rs = slice(u * Q_BLOCK, (u + 1) * Q_BLOCK)
        start = pl.multiple_of((pl.program_id(2) * nq + u) * Q_BLOCK, Q_BLOCK)
        ks = kp_ref[pl.ds(start, kw), :]
        vs = vp_ref[pl.ds(start, kw), :]
        s = lax.dot_general(_stack_heads_native(q_ref[rs, :]), ks, _TRANS_B, preferred_element_type=F32)
        s = s + bm_ref[...]
        kpos = start - window + lax.broadcasted_iota(jnp.int32, s.shape, 1)
        s = jnp.where(kpos >= 0, s, NEG_INF)
        if sinks:
            s = jnp.concatenate([s, sk_ref[...]], axis=1)
        m = jnp.max(s, axis=1, keepdims=True)
        p = jnp.exp2(s - m)
        l = jnp.sum(p, axis=1, keepdims=True)
        o = jnp.dot(p[:, :kw].astype(CD), vs, preferred_element_type=F32) * (1.0 / l)
        o_ref[rs, :] = _unstack_heads(o)


def _banded_attn(main3, q_col, k_col, v_col, biasmask, sink_rows, window, nq):
    b, t, _ = main3.shape
    kw = window + Q_BLOCK
    tq = nq * Q_BLOCK
    assert t % tq == 0
    sinks = sink_rows is not None
    in_specs = [pl.BlockSpec((None, tq, 2 * LANES), lambda i, g, n: (i, n, q_col + g)),
                pl.BlockSpec((None, t, LANES), lambda i, g, n: (i, 0, k_col)),
                pl.BlockSpec((None, t, LANES), lambda i, g, n: (i, 0, v_col)),
                pl.BlockSpec((None, NSA_REP * Q_BLOCK, kw), lambda i, g, n: (g, 0, 0))]
    args = [main3, main3, main3, biasmask]
    if sinks:
        in_specs.append(pl.BlockSpec((None, NSA_REP * Q_BLOCK, LANES), lambda i, g, n: (g, 0, 0)))
        args.append(sink_rows)
    return pl.pallas_call(
        functools.partial(_banded_kernel, window=window, sinks=sinks, nq=nq),
        out_shape=jax.ShapeDtypeStruct((b, t, GROUP_WIDTH), F32),
        grid=(b, NSA_KV_HEADS, t // tq),
        in_specs=in_specs,
        out_specs=pl.BlockSpec((None, tq, 2 * LANES), lambda i, g, n: (i, n, g)),
        scratch_shapes=[pltpu.VMEM((window + t, LANES), CD), pltpu.VMEM((window + t, LANES), CD)],
        compiler_params=_cparams("arbitrary", "arbitrary", "arbitrary"),
        name="swa_attn" if sinks else "nsa_win_attn",
    )(*args)


def _sel_kernel(q_ref, k_ref, v_ref, mf_ref, fq_ref, nb_ref, oc_ref, ow_ref, gt_ref, o_ref,
                ka_ref, vd_ref, qa_ref, s_ref, sn_ref, mx_ref, acc_ref):
    g = pl.program_id(1)
    n = pl.program_id(2)
    t = k_ref.shape[0]

    @pl.when(n == 0)
    def _():
        i = lax.broadcasted_iota(jnp.int32, (LANES, LANES), 0)
        j = lax.broadcasted_iota(jnp.int32, (LANES, LANES), 1)
        pick = jnp.where((j < HEAD_DIM) & (i == g * HEAD_DIM + j), 1.0, 0.0).astype(CD)
        chunk = 1024

        def body(c, _):
            off = pl.multiple_of(c * chunk, chunk)
            kk = jnp.dot(k_ref[pl.ds(off, chunk), :], pick, preferred_element_type=F32)
            ln = lax.broadcasted_iota(jnp.int32, (chunk, LANES), 1)
            kk = jnp.where((ln == HEAD_DIM) | (ln == HEAD_DIM + 1), 1.0, kk)
            ka_ref[pl.ds(off, chunk), 0:LANES] = kk.astype(ka_ref.dtype)
            key = off + lax.broadcasted_iota(jnp.int32, (chunk, LANES), 0)
            ka_ref[pl.ds(off, chunk), LANES:2 * LANES] = jnp.where(
                (key >> 6) == ln, 1.0, 0.0).astype(ka_ref.dtype)
            vd_ref[pl.ds(off, chunk), LANES:2 * LANES] = jnp.ones((chunk, LANES), vd_ref.dtype)
            return 0

        lax.fori_loop(0, t // chunk, body, 0)
        _fill_rows(vd_ref, 0, v_ref, _dup_matrix(g))

    q = q_ref[...]
    ii = lax.broadcasted_iota(jnp.int32, (LANES, LANES), 0)
    jj = lax.broadcasted_iota(jnp.int32, (LANES, LANES), 1)
    shift = jnp.where(ii == jj + HEAD_DIM, 1.0, 0.0).astype(CD)
    lom = _lane_mask(Q_BLOCK, 0, HEAD_DIM, q.dtype)
    c0, c1 = q[:, :LANES], q[:, LANES:]
    q0 = jnp.concatenate([
        c0 * lom, jnp.dot(c0, shift, preferred_element_type=F32).astype(q.dtype),
        c1 * lom, jnp.dot(c1, shift, preferred_element_type=F32).astype(q.dtype)], axis=0)
    q1 = q0 + fq_ref[...]
    mf = mf_ref[...]
    qa_ref[...] = jnp.concatenate([q1, jnp.concatenate([mf, mf, mf, mf], axis=0)], axis=1)

    rows = NSA_REP * Q_BLOCK
    tkf = s_ref.shape[2]
    tkn = sn_ref.shape[1]
    near_off = pl.multiple_of(jnp.maximum(n - 1, 0) * Q_BLOCK, Q_BLOCK)
    n_full = near_off // tkf
    rem = near_off - n_full * tkf

    half = tkf // 2

    def fold_max(mx, s):
        for i in range(s.shape[1] // LANES):
            mx = jnp.maximum(mx, s[:, i * LANES:(i + 1) * LANES])
        return mx

    def score_tiles(j0, count, limit=None):
        mx = mx_ref[...]
        for gi in range(count):
            off = pl.multiple_of((j0 + gi) * tkf, tkf)
            for hf in range(2):
                s = lax.dot_general(qa_ref[...], ka_ref[pl.ds(off + hf * half, half), :], _TRANS_B,
                                    preferred_element_type=F32)
                if limit is not None:
                    col = hf * half + lax.broadcasted_iota(jnp.int32, (rows, half), 1)
                    s = jnp.where(col < limit, s, NEG_INF)
                s_ref[j0 + gi, :, hf * half:(hf + 1) * half] = s
                mx = fold_max(mx, s)
        mx_ref[...] = mx

    def value_tiles(j0, count):
        off = pl.multiple_of(j0 * tkf, tkf)
        p = jnp.concatenate([probs(s_ref[j0 + gi]) for gi in range(count)], axis=1)
        acc_ref[...] += jnp.dot(p, vd_ref[pl.ds(off, count * tkf), :], preferred_element_type=F32)

    def probs(s):
        mb = mx_ref[...]
        return jnp.concatenate([jnp.exp2(s[:, i * LANES:(i + 1) * LANES] - mb)
                                for i in range(s.shape[1] // LANES)], axis=1).astype(CD)

    mx_ref[...] = jnp.full((rows, LANES), M_INIT, F32)
    _sweep_tiles(score_tiles, n_full)

    @pl.when(rem > 0)
    def _():
        score_tiles(n_full, 1, limit=rem)

    sn = lax.dot_general(qa_ref[...], ka_ref[pl.ds(near_off, tkn), :], _TRANS_B, preferred_element_type=F32)
    sn = sn + nb_ref[jnp.minimum(n, 1)]
    sn_ref[...] = sn
    m = jnp.max(fold_max(mx_ref[...], sn), axis=1, keepdims=True)
    mx_ref[...] = jnp.broadcast_to(m, (rows, LANES))

    acc_ref[...] = jnp.dot(probs(sn_ref[...]), vd_ref[pl.ds(near_off, tkn), :], preferred_element_type=F32)
    _sweep_tiles(value_tiles, n_full + jnp.where(rem > 0, 1, 0))
    acc = acc_ref[...]
    o_sel = _unstack_heads(acc[:, :LANES] * (1.0 / acc[:, LANES:LANES + 1]))

    gt = jax.nn.sigmoid(gt_ref[...])
    lane2 = lax.broadcasted_iota(jnp.int32, (Q_BLOCK, 2 * LANES), 1)

    def gate(branch):
        cols = []
        for r in range(NSA_REP):
            a = 8 + 3 * r + branch
            cols.append(jnp.where(g == 0, gt[:, a:a + 1], gt[:, a + 3 * NSA_REP:a + 3 * NSA_REP + 1]))
        w = jnp.where(lane2 < HEAD_DIM, cols[0], jnp.where(lane2 < 2 * HEAD_DIM, cols[1],
                      jnp.where(lane2 < 3 * HEAD_DIM, cols[2], cols[3])))
        return w

    o_ref[...] = gate(0) * oc_ref[...] + gate(1) * o_sel + gate(2) * ow_ref[...]


def _sel_attn(main3, q_col, k_col, v_col, maskfeat, farq, nearbias, o_cmp, o_win, small3, tkf):
    b, t, _ = main3.shape
    assert t % tkf == 0 and tkf % Q_BLOCK == 0
    rows = NSA_REP * Q_BLOCK
    blk = lambda w: pl.BlockSpec((None, Q_BLOCK, w), lambda i, g, n: (i, n, g))
    return pl.pallas_call(
        _sel_kernel,
        out_shape=jax.ShapeDtypeStruct((b, t, GROUP_WIDTH), F32),
        grid=(b, NSA_KV_HEADS, t // Q_BLOCK),
        in_specs=[pl.BlockSpec((None, Q_BLOCK, 2 * LANES), lambda i, g, n: (i, n, q_col + g)),
                  pl.BlockSpec((None, t, LANES), lambda i, g, n: (i, 0, k_col)),
                  pl.BlockSpec((None, t, LANES), lambda i, g, n: (i, 0, v_col)),
                  blk(LANES),
                  pl.BlockSpec((None, rows, LANES), lambda i, g, n: (g, 0, 0)),
                  pl.BlockSpec((None, 2, rows, 2 * Q_BLOCK), lambda i, g, n: (g, 0, 0, 0)),
                  blk(2 * LANES), blk(2 * LANES),
                  pl.BlockSpec((None, Q_BLOCK, LANES), lambda i, g, n: (i, n, 4))],
        out_specs=blk(2 * LANES),
        scratch_shapes=[pltpu.VMEM((t, 2 * LANES), CD), pltpu.VMEM((t, 2 * LANES), CD),
                        pltpu.VMEM((rows, 2 * LANES), CD), pltpu.VMEM((t // tkf, rows, tkf), F32),
                        pltpu.VMEM((rows, 2 * Q_BLOCK), F32),
                        pltpu.VMEM((rows, LANES), F32), pltpu.VMEM((rows, 2 * LANES), F32)],
        compiler_params=_cparams("arbitrary", "arbitrary", "arbitrary"),
        name="nsa_sel_attn",
    )(main3, main3, main3, maskfeat, farq, nearbias, o_cmp, o_win, small3)


def _outproj_kernel(a_ref, b_ref, c_ref, d_ref, gn_ref, w_ref, h_ref, o_ref, u_ref):
    @pl.when(pl.program_id(1) == 0)
    def _():
        for k, r in enumerate((a_ref, b_ref, c_ref, d_ref)):
            sl = slice(k * GROUP_WIDTH, (k + 1) * GROUP_WIDTH)
            u_ref[:, sl] = _rms(r[...], gn_ref[:, sl]).astype(u_ref.dtype)

    o_ref[...] = h_ref[...] + jnp.dot(u_ref[...], w_ref[...], preferred_element_type=F32)


def _outproj(parts, gn, w, h, tm, tn):
    n, d = h.shape
    mix = N_GROUPS * GROUP_WIDTH
    part = pl.BlockSpec((tm, GROUP_WIDTH), lambda i, j: (i, 0))
    return pl.pallas_call(
        _outproj_kernel,
        out_shape=jax.ShapeDtypeStruct((n, d), F32),
        grid=(n // tm, d // tn),
        in_specs=[part, part, part, part,
                  pl.BlockSpec((1, mix), lambda i, j: (0, 0)),
                  pl.BlockSpec((mix, tn), lambda i, j: (0, j)),
                  pl.BlockSpec((tm, tn), lambda i, j: (i, j))],
        out_specs=pl.BlockSpec((tm, tn), lambda i, j: (i, j)),
        scratch_shapes=[pltpu.VMEM((tm, mix), CD)],
        compiler_params=_cparams("parallel", "arbitrary"),
        name="outproj",
    )(*parts, gn.reshape(1, mix), w, h)


def _mlp_kernel(*refs, final):
    if final:
        h_ref, g_ref, wu_ref, wd_ref, gf_ref, o_ref, u_ref = refs
    else:
        h_ref, g_ref, wu_ref, wd_ref, o_ref, u_ref = refs
    c = pl.program_id(1)

    @pl.when(c == 0)
    def _():
        x = h_ref[...]
        u_ref[...] = _rms(x, g_ref[...]).astype(u_ref.dtype)
        o_ref[...] = x

    m = jnp.dot(u_ref[...], wu_ref[...], preferred_element_type=F32)
    a = jnp.square(jnp.maximum(m, 0.0)).astype(CD)
    o_ref[...] += jnp.dot(a, wd_ref[...], preferred_element_type=F32)

    if final:
        @pl.when(c == pl.num_programs(1) - 1)
        def _():
            o_ref[...] = _rms(o_ref[...], gf_ref[...])


def _mlp(h, g, wu, wd, gf, tm, tf):
    n, d = h.shape
    dff = wu.shape[1]
    final = gf is not None
    in_specs = [pl.BlockSpec((tm, d), lambda i, c: (i, 0)),
                pl.BlockSpec((1, d), lambda i, c: (0, 0)),
                pl.BlockSpec((d, tf), lambda i, c: (0, c)),
                pl.BlockSpec((tf, d), lambda i, c: (c, 0))]
    args = [h, g.reshape(1, d), wu, wd]
    if final:
        in_specs.append(pl.BlockSpec((1, d), lambda i, c: (0, 0)))
        args.append(gf.reshape(1, d))
    return pl.pallas_call(
        functools.partial(_mlp_kernel, final=final),
        out_shape=jax.ShapeDtypeStruct((n, d), F32),
        grid=(n // tm, dff // tf),
        in_specs=in_specs,
        out_specs=pl.BlockSpec((tm, d), lambda i, c: (i, 0)),
        scratch_shapes=[pltpu.VMEM((tm, d), CD)],
        compiler_params=_cparams("parallel", "arbitrary"),
        name="mlp",
    )(*args)


def _t5_bucket_np(dist):
    max_exact = REL_BUCKETS // 2
    d = np.maximum(dist, 0)
    ratio = np.log(np.maximum(d, 1).astype(np.float32) / np.float32(max_exact)) / np.float32(
        math.log(REL_MAX_DIST / max_exact))
    large = max_exact + (ratio * np.float32(REL_BUCKETS - max_exact)).astype(np.int32)
    large = np.minimum(large, REL_BUCKETS - 1)
    return np.where(d < max_exact, d, large)


def _bias_tables(rel_bias):
    tbl = (rel_bias.astype(F32) * LOG2E).T.reshape(2, NSA_KV_HEADS, NSA_REP, REL_BUCKETS)
    tbl_nsa, tbl_swa = tbl[0], tbl[1]

    def lookup(tb, bucket):
        onehot = (jnp.asarray(bucket, jnp.int32)[None] == jnp.arange(REL_BUCKETS)[:, None, None]).astype(F32)
        return jnp.einsum('grb,bij->grij', tb, onehot, precision=lax.Precision.HIGHEST)
    i = np.arange(Q_BLOCK)[:, None]
    far_bucket = REL_BUCKETS - 1

    def banded(tb, window):
        jk = np.arange(window + Q_BLOCK)[None, :]
        dist = i + window - jk
        ok = (dist >= 0) & (dist < window)
        vals = lookup(tb, _t5_bucket_np(dist))
        vals = jnp.where(jnp.asarray(ok)[None, None], vals, NEG_INF)
        return vals.reshape(NSA_KV_HEADS, NSA_REP * Q_BLOCK, window + Q_BLOCK)

    bm_win = banded(tbl_nsa, NSA_WINDOW)
    bm_swa = banded(tbl_swa, SWA_WINDOW)

    far = tbl_nsa[:, :, far_bucket]
    jk = np.arange(2 * Q_BLOCK)[None, :]
    tiles = []
    for first_key_back in (0, Q_BLOCK):
        dist = i + first_key_back - jk
        delta = lookup(tbl_nsa, _t5_bucket_np(dist)) - far[:, :, None, None]
        tiles.append(jnp.where(jnp.asarray(dist >= 0)[None, None], delta, NEG_INF))
    near = jnp.stack(tiles, axis=1).reshape(NSA_KV_HEADS, 2, NSA_REP * Q_BLOCK, 2 * Q_BLOCK)
    far_rows = jnp.broadcast_to(far[:, :, None], (NSA_KV_HEADS, NSA_REP, Q_BLOCK)).reshape(NSA_KV_HEADS, -1)
    hi, lo_ = _split_parts(far_rows, 2)
    zeros = lambda k: jnp.zeros((NSA_KV_HEADS, NSA_REP * Q_BLOCK, k), CD)
    farq = jnp.concatenate([zeros(HEAD_DIM), hi[..., None], lo_[..., None], zeros(LANES - HEAD_DIM - 2)], axis=-1)

    u = np.arange(16)[None, :]
    dist_c = i - 16 * u + 113
    band = lookup(tbl_nsa, _t5_bucket_np(dist_c))
    band = jnp.where(jnp.asarray(dist_c >= 0)[None, None], band, MASK_NEG)
    band = band.reshape(NSA_KV_HEADS, NSA_REP * Q_BLOCK, 16)
    bh, bl = _split_parts(band, 2)
    neg = jnp.full((NSA_KV_HEADS, NSA_REP * Q_BLOCK, 1), MASK_NEG, CD)
    pq = jnp.concatenate([bh, bl, hi[..., None], lo_[..., None], neg, zeros(LANES - 35)], axis=-1)
    return bm_win, bm_swa, near, farq, pq


def _overlap_matrix(nc_pad, ns):
    ci = np.arange(nc_pad)[:, None]
    sj = np.arange(LANES)[None, :]
    ov = ((ci * NSA_CMP_STRIDE + NSA_CMP_LEN - 1 >= sj * NSA_SEL_LEN)
          & (ci * NSA_CMP_STRIDE <= sj * NSA_SEL_LEN + NSA_SEL_LEN - 1) & (sj < ns))
    return jnp.asarray(ov.astype(np.float32), CD)


def _layout_w_in(w):
    w = w.astype(F32)
    sc = HEAD_DIM ** -0.5 * LOG2E
    z = lambda k: jnp.zeros((w.shape[0], k), F32)
    main = jnp.concatenate([w[:, 544:1056] * sc, w[:, 1056:2080], w[:, 2088:2600] * sc, w[:, 3392:3904] * sc,
                            w[:, 2600:3368], w[:, 3904:4160]], axis=1)
    kr = w[:, 512:544]
    blk_a = jnp.concatenate([w[:, 2080:2088], w[:, 3368:3392], z(32), kr, z(32)], axis=1)
    blk_b = jnp.concatenate([z(64), -kr[:, 16:], kr[:, :16], z(32)], axis=1)
    small = jnp.concatenate([w[:, 0:512], blk_a, blk_b], axis=1)
    return main.astype(CD), small.astype(CD)


_FOX_Q, _FOX_K512, _FOX_V = 0, 1, 8
_NSA_Q256, _SWA_Q256 = 6, 8
_KVC_COL = 2560
_KSEL, _VSEL, _KWIN, _VWIN, _KSWA, _VSWA = 22, 23, 24, 25, 26, 27


def _layout_mla(w_uq, w_ukv):
    w3 = w_uq.astype(F32).reshape(MLA_Q_RANK, GROUP_HEADS, MLA_NOPE + MLA_ROPE)
    nope, rp = w3[:, :, :MLA_NOPE], w3[:, :, MLA_NOPE:]
    half = MLA_ROPE // 2
    sw = jnp.concatenate([-rp[:, :, half:], rp[:, :, :half]], axis=-1)
    z = lambda k: jnp.zeros((MLA_Q_RANK, GROUP_HEADS, k), F32)
    plain = jnp.concatenate([nope, rp, z(32)], axis=-1).reshape(MLA_Q_RANK, -1)
    swapped = jnp.concatenate([z(64), sw, z(32)], axis=-1).reshape(MLA_Q_RANK, -1)
    wq = jnp.concatenate([plain, swapped], axis=1)
    k3 = w_ukv.astype(F32).reshape(MLA_KV_RANK, GROUP_HEADS, 2 * HEAD_DIM)
    wk = jnp.concatenate([k3[:, :, :MLA_NOPE], jnp.zeros((MLA_KV_RANK, GROUP_HEADS, HEAD_DIM), F32)],
                         axis=-1).reshape(MLA_KV_RANK, -1)
    wv = k3[:, :, MLA_NOPE:].reshape(MLA_KV_RANK, -1)
    return wq.astype(CD), wk.astype(CD), wv.astype(CD)


def _rope_tables(t):
    inv = ROPE_THETA ** (-jnp.arange(0, MLA_ROPE, 2, dtype=F32) / MLA_ROPE)
    ang = jnp.arange(t).astype(F32)[:, None] * inv[None, :]
    cc = jnp.concatenate([jnp.cos(ang)] * 2, axis=1)
    ss = jnp.concatenate([jnp.sin(ang)] * 2, axis=1)
    one, z64, z32 = jnp.ones((t, 64), F32), jnp.zeros((t, 64), F32), jnp.zeros((t, 32), F32)
    return (jnp.concatenate([one, cc, z32], axis=1), jnp.concatenate([z64, ss, z32], axis=1),
            jnp.concatenate([z64, cc, z32], axis=1))


def kernel(x, norm_attn, w_in, mla_q_norm, mla_w_uq, mla_kv_norm, mla_w_ukv, fox_b_f, nsa_cmp_pos,
           nsa_cmp_w1, nsa_cmp_w2, swa_sinks, group_norm, w_out, norm_mlp, w_up, w_down, rel_bias,
           final_norm):
    b, t, d = x.shape
    n = b * t
    depth = w_in.shape[0]
    assert t % 1024 == 0 and d == N_GROUPS * GROUP_WIDTH
    tm = 1024 if n % 1024 == 0 else 512
    tq = 512
    nq = 4
    nr = t // NSA_CMP_STRIDE

    cosq, sinq, cosk = _rope_tables(t)
    bm_win, bm_swa, nearbias, farq, pq = _bias_tables(rel_bias)
    overlap = _overlap_matrix(nr, t // NSA_SEL_LEN)

    h = x.reshape(n, d).astype(F32)
    for l in range(depth):
        w_main, w_small = _layout_w_in(w_in[l])
        main = _norm_matmul(h, norm_attn[l], w_main, CD, tm, 512)
        small = _norm_matmul(h, norm_attn[l], w_small, F32, tm, 768)
        main3 = main.reshape(b, t, -1)
        small3 = small.reshape(b, t, -1)

        wq, wk, wv = _layout_mla(mla_w_uq[l], mla_w_ukv[l])
        q_m, k_m, v_m = _mla_prep(small, mla_q_norm[l], mla_kv_norm[l], wq, wk, wv, cosq, sinq, cosk, t, 512)
        o_mla = _dense_attn(q_m.reshape(b, t, -1), k_m.reshape(b, t, -1), v_m.reshape(b, t, -1), None,
                            fox=False, q_col=0, v_col=0, tq=tq)

        k_aug, fq = _fox_prep(small3, fox_b_f[l], main3, _FOX_K512, 512)
        o_fox = _dense_attn(main3, k_aug, main3, fq, fox=True, q_col=_FOX_Q, v_col=_FOX_V, tq=tq)

        kv16 = main3[:, :, _KVC_COL:_KVC_COL + 256].reshape(b, nr, NSA_CMP_STRIDE * 256)
        cmp = _nsa_compress(kv16, nsa_cmp_pos[l], nsa_cmp_w1[l], nsa_cmp_w2[l])
        o_cmp, maskfeat = _cmp_select(main3, _NSA_Q256, cmp, pq, overlap, nq)
        o_win = _banded_attn(main3, _NSA_Q256, _KWIN, _VWIN, bm_win, None, NSA_WINDOW, nq)
        o_nsa = _sel_attn(main3, _NSA_Q256, _KSEL, _VSEL, maskfeat, farq, nearbias, o_cmp, o_win, small3, tq)

        sink = (swa_sinks[l].astype(F32) * LOG2E).reshape(NSA_KV_HEADS, NSA_REP, 1, 1)
        sink_rows = jnp.concatenate(
            [jnp.broadcast_to(sink, (NSA_KV_HEADS, NSA_REP, Q_BLOCK, 1)),
             jnp.full((NSA_KV_HEADS, NSA_REP, Q_BLOCK, LANES - 1), NEG_INF, F32)],
            axis=-1).reshape(NSA_KV_HEADS, NSA_REP * Q_BLOCK, LANES)
        o_swa = _banded_attn(main3, _SWA_Q256, _KSWA, _VSWA, bm_swa, sink_rows, SWA_WINDOW, nq)

        parts = [o.reshape(n, GROUP_WIDTH) for o in (o_mla, o_fox, o_nsa, o_swa)]
        h = _outproj(parts, group_norm[l], w_out[l].astype(CD), h, tm, 512)
        gf = final_norm if l == depth - 1 else None
        h = _mlp(h, norm_mlp[l], w_up[l].astype(CD), w_down[l].astype(CD), gf, 512, 1024)
    return h.reshape(b, t, d).astype(x.dtype)
```

```python
import functools
import math

import numpy as np
import jax
import jax.numpy as jnp
from jax import lax
from jax.experimental import pallas as pl
from jax.experimental.pallas import tpu as pltpu

HEAD_DIM = 64
GROUP_HEADS = 8
GROUP_WIDTH = GROUP_HEADS * HEAD_DIM
N_GROUPS = 4
Q_BLOCK = 128
EPS = 1e-6
NEG_INF = -1e30
BIG = 1e9

MLA_Q_RANK = 384
MLA_KV_RANK = 128
MLA_NOPE = 64
MLA_ROPE = 32
ROPE_THETA = 10000.0

NSA_KV_HEADS = 2
NSA_REP = GROUP_HEADS // NSA_KV_HEADS
NSA_CMP_LEN = 32
NSA_CMP_STRIDE = 16
NSA_SEL_LEN = 64
NSA_TOP_N = 8
NSA_WINDOW = 256
SWA_WINDOW = 128

REL_BUCKETS = 32
REL_MAX_DIST = 128

V7X_VMEM_BYTES = 64 * 2**20
VMEM_LIMIT = (V7X_VMEM_BYTES * 7) // 8
LANES = 128
CD = jnp.bfloat16
F32 = jnp.float32
MASK_NEG = -(2.0 ** 80)
M_INIT = -(2.0 ** 100)
TILE_GROUP = 8
LOG2E = math.log2(math.e)
MLA_SCORE_SCALE = (MLA_NOPE + MLA_ROPE) ** -0.5 * LOG2E
_TRANS_B = (((1,), (1,)), ((), ()))


def _cparams(*sem):
    return pltpu.CompilerParams(dimension_semantics=sem, vmem_limit_bytes=VMEM_LIMIT)


def _split_parts(x, n):
    parts, r = [], x
    for _ in range(n):
        p = r.astype(CD)
        parts.append(p)
        r = r - p.astype(F32)
    return parts


def _sweep_tiles(fn, n_tiles):
    groups = n_tiles // TILE_GROUP

    def grouped(i, _):
        fn(i * TILE_GROUP, TILE_GROUP)
        return 0

    lax.fori_loop(0, groups, grouped, 0)
    base = groups * TILE_GROUP
    rem = n_tiles - base
    piece = TILE_GROUP // 2
    while piece >= 1:
        @pl.when((rem & piece) != 0)
        def _(piece=piece):
            fn(base + (rem & ~(2 * piece - 1)), piece)
        piece //= 2


def _rms(x, g):
    return x * lax.rsqrt(jnp.mean(x * x, axis=-1, keepdims=True) + EPS) * g


def _norm_matmul_kernel(x_ref, g_ref, w_ref, o_ref, u_ref):
    @pl.when(pl.program_id(1) == 0)
    def _():
        u_ref[...] = _rms(x_ref[...], g_ref[...]).astype(u_ref.dtype)

    o_ref[...] = jnp.dot(u_ref[...], w_ref[...], preferred_element_type=F32).astype(o_ref.dtype)


def _norm_matmul(x, g, w, out_dtype, tm, tn):
    n, d = x.shape
    nc = w.shape[1]
    return pl.pallas_call(
        _norm_matmul_kernel,
        out_shape=jax.ShapeDtypeStruct((n, nc), out_dtype),
        grid=(n // tm, nc // tn),
        in_specs=[
            pl.BlockSpec((tm, d), lambda i, j: (i, 0)),
            pl.BlockSpec((1, d), lambda i, j: (0, 0)),
            pl.BlockSpec((d, tn), lambda i, j: (0, j)),
        ],
        out_specs=pl.BlockSpec((tm, tn), lambda i, j: (i, j)),
        scratch_shapes=[pltpu.VMEM((tm, d), CD)],
        compiler_params=_cparams("parallel", "arbitrary"),
        name="norm_matmul",
    )(x, g.reshape(1, d), w)


def _mla_prep_kernel(sm_ref, qn_ref, kvn_ref, wq_ref, wk_ref, wv_ref, cq_ref, sq_ref, ck_ref,
                     q_ref, k_ref, v_ref):
    sm = sm_ref[...]
    nq = _rms(sm[:, :MLA_Q_RANK], qn_ref[...]).astype(CD)
    nkv = _rms(sm[:, MLA_Q_RANK:MLA_Q_RANK + MLA_KV_RANK], kvn_ref[...]).astype(CD)
    blk_a = sm[:, 512:640]
    blk_b = sm[:, 640:768]
    cosq, sinq, cosk = cq_ref[...], sq_ref[...], ck_ref[...]
    hw = GROUP_HEADS * LANES
    qq = jnp.dot(nq, wq_ref[...], preferred_element_type=F32)
    kk = jnp.dot(nkv, wk_ref[...], preferred_element_type=F32)
    kpe = blk_a * cosk + blk_b * sinq
    for h in range(GROUP_HEADS):
        sl = slice(h * LANES, (h + 1) * LANES)
        qh = qq[:, sl] * cosq + qq[:, hw + h * LANES:hw + (h + 1) * LANES] * sinq
        q_ref[:, sl] = (qh * MLA_SCORE_SCALE).astype(q_ref.dtype)
        k_ref[:, sl] = (kk[:, sl] + kpe).astype(k_ref.dtype)
    v_ref[...] = jnp.dot(nkv, wv_ref[...], preferred_element_type=F32).astype(v_ref.dtype)


def _mla_prep(small, qn, kvn, wq, wk, wv, cosq, sinq, cosk, t, tm):
    n = small.shape[0]
    tb = t // tm
    hw = GROUP_HEADS * LANES
    full = lambda a: pl.BlockSpec(a.shape, lambda i: (0,) * a.ndim)
    tab = pl.BlockSpec((tm, LANES), lambda i: (i % tb, 0))
    qn = qn.reshape(1, -1)
    kvn = kvn.reshape(1, -1)
    return pl.pallas_call(
        _mla_prep_kernel,
        out_shape=(jax.ShapeDtypeStruct((n, hw), CD), jax.ShapeDtypeStruct((n, hw), CD),
                   jax.ShapeDtypeStruct((n, GROUP_WIDTH), CD)),
        grid=(n // tm,),
        in_specs=[pl.BlockSpec((tm, small.shape[1]), lambda i: (i, 0)), full(qn), full(kvn),
                  full(wq), full(wk), full(wv), tab, tab, tab],
        out_specs=(pl.BlockSpec((tm, hw), lambda i: (i, 0)), pl.BlockSpec((tm, hw), lambda i: (i, 0)),
                   pl.BlockSpec((tm, GROUP_WIDTH), lambda i: (i, 0))),
        compiler_params=_cparams("parallel"),
        name="mla_prep",
    )(small, qn, kvn, wq, wk, wv, cosq, sinq, cosk)


def _fox_prep_kernel(fl_ref, bf_ref, k_ref, efq_ref, ekf_ref, kaug_ref, fq_ref, carry_ref, *, tc):
    @pl.when(pl.program_id(1) == 0)
    def _():
        carry_ref[...] = jnp.zeros_like(carry_ref)

    x = fl_ref[...] + bf_ref[...]
    logf = -(jnp.maximum(-x, 0.0) + jnp.log1p(jnp.exp(-jnp.abs(x))))
    lane = lax.broadcasted_iota(jnp.int32, logf.shape, 1)
    logf = jnp.where(lane < GROUP_HEADS, logf, 0.0)
    row = lax.broadcasted_iota(jnp.int32, (tc, tc), 0)
    col = lax.broadcasted_iota(jnp.int32, (tc, tc), 1)
    tri = jnp.where(row >= col, 1.0, 0.0).astype(CD)
    cs = jnp.zeros(logf.shape, F32)
    for part in _split_parts(logf, 3):
        cs = cs + jnp.dot(tri, part, preferred_element_type=F32)
    fc = cs + carry_ref[...]
    carry_ref[...] = fc[tc - 1:tc, :]
    parts = _split_parts(fc * LOG2E, 3)
    fq = jnp.zeros(fq_ref.shape, F32)
    kf = jnp.zeros(fq_ref.shape, F32)
    for i, part in enumerate(parts):
        fq = fq + jnp.dot(part, efq_ref[...], preferred_element_type=F32)
        kf = kf + jnp.dot(part, ekf_ref[i], preferred_element_type=F32)
    fq_ref[...] = fq
    k = k_ref[...]
    for p in range(GROUP_HEADS // 2):
        kaug_ref[:, 2 * p * LANES:(2 * p + 1) * LANES] = k[:, p * LANES:(p + 1) * LANES]
        kaug_ref[:, (2 * p + 1) * LANES:(2 * p + 2) * LANES] = kf[:, p * LANES:(p + 1) * LANES].astype(kaug_ref.dtype)


def _fox_prep(small3, b_f, main3, k_col, tc):
    b, t, _ = small3.shape
    pairs = GROUP_HEADS // 2
    efq = np.zeros((LANES, pairs * LANES), np.float32)
    ekf = np.zeros((3, LANES, pairs * LANES), np.float32)
    for h in range(GROUP_HEADS):
        p, a = divmod(h, 2)
        efq[h, p * LANES + a] = 1.0
        for i in range(3):
            ekf[i, h, p * LANES + 3 * a + i] = -1.0
    bf = jnp.zeros((1, LANES), F32).at[0, :GROUP_HEADS].set(b_f.astype(F32))
    return pl.pallas_call(
        functools.partial(_fox_prep_kernel, tc=tc),
        out_shape=(jax.ShapeDtypeStruct((b, t, 2 * GROUP_WIDTH), CD),
                   jax.ShapeDtypeStruct((b, t, GROUP_WIDTH), F32)),
        grid=(b, t // tc),
        in_specs=[pl.BlockSpec((None, tc, LANES), lambda i, j: (i, j, 4)),
                  pl.BlockSpec((1, LANES), lambda i, j: (0, 0)),
                  pl.BlockSpec((None, tc, GROUP_WIDTH), lambda i, j: (i, j, k_col)),
                  pl.BlockSpec(efq.shape, lambda i, j: (0, 0)),
                  pl.BlockSpec(ekf.shape, lambda i, j: (0, 0, 0))],
        out_specs=(pl.BlockSpec((None, tc, 2 * GROUP_WIDTH), lambda i, j: (i, j, 0)),
                   pl.BlockSpec((None, tc, GROUP_WIDTH), lambda i, j: (i, j, 0))),
        scratch_shapes=[pltpu.VMEM((1, LANES), F32)],
        compiler_params=_cparams("arbitrary", "arbitrary"),
        name="fox_prep",
    )(small3, bf, main3, jnp.asarray(efq, CD), jnp.asarray(ekf, CD))


def _dense_attn_kernel(*refs, fox, tq):
    if fox:
        q_ref, k_ref, v_ref, fq_ref, o_ref, va_ref, qa_ref, s_ref, mx_ref, acc_ref = refs
    else:
        q_ref, k_ref, v_ref, o_ref, va_ref, qa_ref, s_ref, mx_ref, acc_ref = refs
    qi = pl.program_id(2)
    t = k_ref.shape[0]
    nch = tq // LANES

    @pl.when(qi == 0)
    def _():
        chunk = 1024

        def body(i, _):
            off = pl.multiple_of(i * chunk, chunk)
            va_ref[pl.ds(off, chunk), 0:LANES] = v_ref[pl.ds(off, chunk), :]
            va_ref[pl.ds(off, chunk), LANES:2 * LANES] = jnp.ones((chunk, LANES), va_ref.dtype)
            return 0

        lax.fori_loop(0, t // chunk, body, 0)

    half = tq // 2
    outs = []
    for h in range(2):
        q = q_ref[...]
        if fox:
            qa_ref[...] = jnp.concatenate(
                [q * _lane_mask(tq, h * HEAD_DIM, (h + 1) * HEAD_DIM, q.dtype), _lane_mask(tq, 3 * h, 3 * h + 3, q.dtype)],
                axis=1)
        else:
            qa_ref[...] = q[:, h * LANES:(h + 1) * LANES]

        def score_tiles(j0, count, h=h):
            mx = mx_ref[...]
            for gi in range(count):
                off = pl.multiple_of((j0 + gi) * tq, tq)
                for hf in range(2):
                    rows = pl.ds(off + hf * half, half)
                    k = k_ref[rows, :] if fox else k_ref[rows, pl.ds(h * LANES, LANES)]
                    s = lax.dot_general(qa_ref[...], k, _TRANS_B, preferred_element_type=F32)
                    delta = (hf * half + lax.broadcasted_iota(jnp.int32, (tq, half), 1)
                             - lax.broadcasted_iota(jnp.int32, (tq, half), 0))
                    s = jnp.where(delta <= (qi - j0 - gi) * tq, s, NEG_INF)
                    s_ref[j0 + gi, :, hf * half:(hf + 1) * half] = s
                    for i in range(half // LANES):
                        mx = jnp.maximum(mx, s[:, i * LANES:(i + 1) * LANES])
            mx_ref[...] = mx

        def value_tiles(j0, count):
            mc = mx_ref[...]
            ps = []
            for gi in range(count):
                s = s_ref[j0 + gi]
                ps += [jnp.exp2(s[:, i * LANES:(i + 1) * LANES] - mc) for i in range(nch)]
            off = pl.multiple_of(j0 * tq, tq)
            acc_ref[...] += jnp.dot(jnp.concatenate(ps, axis=1).astype(CD), va_ref[pl.ds(off, count * tq), :],
                                    preferred_element_type=F32)

        mx_ref[...] = jnp.full(mx_ref.shape, M_INIT, F32)
        _sweep_tiles(score_tiles, qi + 1)
        m = jnp.max(mx_ref[...], axis=1, keepdims=True)
        if fox:
            f = fq_ref[:, h:h + 1]
            m = (m + f) - f
        mx_ref[...] = jnp.broadcast_to(m, (tq, LANES))

        acc_ref[...] = jnp.zeros(acc_ref.shape, F32)
        _sweep_tiles(value_tiles, qi + 1)
        acc = acc_ref[...]
        outs.append(acc[:, :LANES] * (1.0 / acc[:, LANES:LANES + 1]))
    lane = lax.broadcasted_iota(jnp.int32, (tq, LANES), 1)
    o_ref[...] = jnp.where(lane < HEAD_DIM, outs[0], outs[1]).astype(o_ref.dtype)


def _dense_attn(q3, k3, v3, fq3, *, fox, q_col, v_col, tq):
    b, t, _ = q3.shape
    assert t % tq == 0 and t % 1024 == 0
    pairs = GROUP_HEADS // 2
    qw = LANES if fox else 2 * LANES
    kd = 2 * LANES if fox else LANES
    in_specs = [pl.BlockSpec((None, tq, qw), lambda i, p, j: (i, j, q_col + p)),
                pl.BlockSpec((None, t, 2 * LANES), lambda i, p, j: (i, 0, p)),
                pl.BlockSpec((None, t, LANES), lambda i, p, j: (i, 0, v_col + p))]
    args = [q3, k3, v3]
    if fox:
        in_specs.append(pl.BlockSpec((None, tq, LANES), lambda i, p, j: (i, j, p)))
        args.append(fq3)
    return pl.pallas_call(
        functools.partial(_dense_attn_kernel, fox=fox, tq=tq),
        out_shape=jax.ShapeDtypeStruct((b, t, GROUP_WIDTH), F32),
        grid=(b, pairs, t // tq),
        in_specs=in_specs,
        out_specs=pl.BlockSpec((None, tq, LANES), lambda i, p, j: (i, j, p)),
        scratch_shapes=[pltpu.VMEM((t, 2 * LANES), CD), pltpu.VMEM((tq, kd), CD),
                        pltpu.VMEM((t // tq, tq, tq), F32), pltpu.VMEM((tq, LANES), F32),
                        pltpu.VMEM((tq, 2 * LANES), F32)],
        compiler_params=_cparams("arbitrary", "arbitrary", "arbitrary"),
        name="fox_attn" if fox else "mla_attn",
    )(*args)


def _gelu_tanh(x):
    return 0.5 * x * (1.0 + jnp.tanh(math.sqrt(2.0 / math.pi) * (x + 0.044715 * (x * x * x))))


def _compress_kernel(r_ref, w1a_ref, w1b_ref, pa_ref, pb_ref, w2_ref, o_ref):
    r = r_ref[...]
    nr = r.shape[0]
    ya = jnp.dot(r, w1a_ref[...], preferred_element_type=F32)
    yb = jnp.dot(r, w1b_ref[...], preferred_element_type=F32)
    pc = (jnp.dot(pa_ref[...], w1a_ref[...], preferred_element_type=F32)
          + jnp.dot(pb_ref[...], w1b_ref[...], preferred_element_type=F32))[0:1, :]
    pre = ya + pltpu.roll(yb, nr - 1, 0) + pc
    h1 = _gelu_tanh(pre).astype(CD)
    rowi = lax.broadcasted_iota(jnp.int32, (nr, LANES), 0)
    for g in range(NSA_KV_HEADS):
        y = jnp.dot(h1, w2_ref[g], preferred_element_type=F32)
        o_ref[g] = jnp.where(rowi < nr - 1, y, 0.0).astype(o_ref.dtype)


def _nsa_compress(kv16, cmp_pos, cmp_w1, cmp_w2):
    b, nr, kw = kv16.shape
    half = NSA_CMP_LEN // 2
    eye2 = jnp.eye(2, dtype=F32)
    w1r = cmp_w1.astype(F32).reshape(2, NSA_CMP_LEN, HEAD_DIM, HEAD_DIM)
    expand = lambda w: jnp.einsum('klde,kw,gh->klwgdhe', w, eye2, eye2).reshape(2, kw, LANES).astype(CD)
    w1a, w1b = expand(w1r[:, :half]), expand(w1r[:, half:])
    posr = cmp_pos.astype(F32)
    tile = lambda p: jnp.broadcast_to(p[:, None, :, None, None, :], (2, 8, half, 2, 2, HEAD_DIM)).reshape(2, 8, kw).astype(CD)
    pa, pb = tile(posr[:, :half]), tile(posr[:, half:])
    w2 = cmp_w2.astype(F32)
    w2d = jnp.zeros((2, NSA_KV_HEADS, LANES, LANES), F32)
    for g in range(NSA_KV_HEADS):
        blk = jnp.concatenate([w2, w2], axis=2)
        w2d = w2d.at[:, g, g * HEAD_DIM:(g + 1) * HEAD_DIM, :].set(blk)
    w2d = w2d.astype(CD)
    return pl.pallas_call(
        _compress_kernel,
        out_shape=jax.ShapeDtypeStruct((b, 2, NSA_KV_HEADS, nr, LANES), CD),
        grid=(b, 2),
        in_specs=[pl.BlockSpec((None, nr, kw), lambda i, k: (i, 0, 0)),
                  pl.BlockSpec((None, kw, LANES), lambda i, k: (k, 0, 0)),
                  pl.BlockSpec((None, kw, LANES), lambda i, k: (k, 0, 0)),
                  pl.BlockSpec((None, 8, kw), lambda i, k: (k, 0, 0)),
                  pl.BlockSpec((None, 8, kw), lambda i, k: (k, 0, 0)),
                  pl.BlockSpec((None, NSA_KV_HEADS, LANES, LANES), lambda i, k: (k, 0, 0, 0))],
        out_specs=pl.BlockSpec((None, None, NSA_KV_HEADS, nr, LANES), lambda i, k: (i, k, 0, 0, 0)),
        compiler_params=_cparams("parallel", "arbitrary"),
        name="nsa_compress",
    )(kv16, w1a, w1b, pa, pb, w2d)


def _lane_mask(rows, lo, hi, dtype):
    lane = lax.broadcasted_iota(jnp.int32, (rows, LANES), 1)
    return jnp.where((lane >= lo) & (lane < hi), 1.0, 0.0).astype(dtype)


def _stack_heads_native(q):
    lo = _lane_mask(Q_BLOCK, 0, HEAD_DIM, q.dtype)
    hi = _lane_mask(Q_BLOCK, HEAD_DIM, LANES, q.dtype)
    c0, c1 = q[:, :LANES], q[:, LANES:]
    return jnp.concatenate([c0 * lo, c0 * hi, c1 * lo, c1 * hi], axis=0)


def _unstack_heads(o):
    lane = lax.broadcasted_iota(jnp.int32, (Q_BLOCK, LANES), 1)
    lo = lane < HEAD_DIM
    return jnp.concatenate([jnp.where(lo, o[0:128], o[128:256]), jnp.where(lo, o[256:384], o[384:512])], axis=1)


def _dup_matrix(g):
    i = lax.broadcasted_iota(jnp.int32, (LANES, LANES), 0)
    j = lax.broadcasted_iota(jnp.int32, (LANES, LANES), 1)
    return jnp.where(i == g * HEAD_DIM + (j & (HEAD_DIM - 1)), 1.0, 0.0).astype(CD)


def _fill_rows(dst_ref, row0, src_ref, mat, chunk=1024):
    n = src_ref.shape[0]

    def body(i, _):
        off = pl.multiple_of(i * chunk, chunk)
        dst_ref[pl.ds(row0 + off, chunk), 0:LANES] = jnp.dot(
            src_ref[pl.ds(off, chunk), :], mat, preferred_element_type=F32).astype(dst_ref.dtype)
        return 0

    lax.fori_loop(0, n // chunk, body, 0)


def _cmp_select_kernel(q_ref, kc_ref, vc_ref, pq_ref, ov_ref, o_ref, mf_ref, *, nq):
    nc = kc_ref.shape[0]
    ci = lax.broadcasted_iota(jnp.int32, (nc, LANES), 0)
    f = lax.broadcasted_iota(jnp.int32, (nc, LANES), 1)
    j = lax.broadcasted_iota(jnp.int32, (Q_BLOCK, LANES), 1)
    i = lax.broadcasted_iota(jnp.int32, (Q_BLOCK, LANES), 0)
    jf = j.astype(F32)
    one_if = lambda cond: jnp.where(cond, 1.0, 0.0)
    for u in range(nq):
        n = pl.program_id(2) * nq + u
        rs = slice(u * Q_BLOCK, (u + 1) * Q_BLOCK)
        qaug = jnp.concatenate([_stack_heads_native(q_ref[rs, :]), pq_ref[...]], axis=1)
        uc = ci - 8 * n + 9
        feat = jnp.where(f < 32, one_if(uc == (f & 15)),
                         jnp.where(f < 34, one_if(uc < 0), one_if((f == 34) & (uc > 15)))).astype(CD)
        kaug = jnp.concatenate([kc_ref[...], feat], axis=1)
        s = lax.dot_general(qaug, kaug, _TRANS_B, preferred_element_type=F32)
        m = jnp.max(s, axis=1, keepdims=True)
        p = jnp.exp2(s - m)
        l = jnp.sum(p, axis=1, keepdims=True)
        pc = p * jnp.where(m > 0.5 * MASK_NEG, 1.0 / l, 0.0)
        o = jnp.dot(pc.astype(CD), vc_ref[...], preferred_element_type=F32)
        o_ref[rs, :] = _unstack_heads(o)
        pcs = pc[0:128] + pc[128:256] + pc[256:384] + pc[384:512]
        imp = jnp.dot(pcs.astype(CD), ov_ref[...], preferred_element_type=F32)
        cur = 2 * n + jnp.where(i >= NSA_SEL_LEN, 1, 0)
        causal = j <= cur
        forced = (j == 0) | (j == cur) | (j == cur - 1)
        sel = one_if(forced & causal)
        score = jnp.where(causal & jnp.logical_not(forced), imp, -3.0e38)
        for _ in range(NSA_TOP_N - 3):
            mx = jnp.max(score, axis=1, keepdims=True)
            idx = jnp.min(jnp.where(score == mx, jf, float(LANES)), axis=1, keepdims=True)
            pick = jf == idx
            sel = jnp.where(pick, 1.0, sel)
            score = jnp.where(pick, -3.0e38, score)
        mf_ref[rs, :] = jnp.where(sel > 0.5, 0.0, MASK_NEG).astype(mf_ref.dtype)


def _cmp_select(main3, q_col, cmp, pq, overlap, nq):
    b, t, _ = main3.shape
    nc = cmp.shape[3]
    tq = nq * Q_BLOCK
    assert t % tq == 0
    return pl.pallas_call(
        functools.partial(_cmp_select_kernel, nq=nq),
        out_shape=(jax.ShapeDtypeStruct((b, t, GROUP_WIDTH), F32),
                   jax.ShapeDtypeStruct((b, t, NSA_KV_HEADS * LANES), CD)),
        grid=(b, NSA_KV_HEADS, t // tq),
        in_specs=[pl.BlockSpec((None, tq, 2 * LANES), lambda i, g, n: (i, n, q_col + g)),
                  pl.BlockSpec((None, None, None, nc, LANES), lambda i, g, n: (i, 0, g, 0, 0)),
                  pl.BlockSpec((None, None, None, nc, LANES), lambda i, g, n: (i, 1, g, 0, 0)),
                  pl.BlockSpec((None, NSA_REP * Q_BLOCK, LANES), lambda i, g, n: (g, 0, 0)),
                  pl.BlockSpec(overlap.shape, lambda i, g, n: (0, 0))],
        out_specs=(pl.BlockSpec((None, tq, 2 * LANES), lambda i, g, n: (i, n, g)),
                   pl.BlockSpec((None, tq, LANES), lambda i, g, n: (i, n, g))),
        compiler_params=_cparams("parallel", "parallel", "arbitrary"),
        name="nsa_cmp_select",
    )(main3, cmp, cmp, pq, overlap)


def _banded_kernel(*refs, window, sinks, nq):
    if sinks:
        q_ref, k_ref, v_ref, bm_ref, sk_ref, o_ref, kp_ref, vp_ref = refs
    else:
        q_ref, k_ref, v_ref, bm_ref, o_ref, kp_ref, vp_ref = refs
    g = pl.program_id(1)
    kw = window + Q_BLOCK

    @pl.when(pl.program_id(2) == 0)
    def _():
        dup = _dup_matrix(g)
        kp_ref[0:window, :] = jnp.zeros((window, LANES), kp_ref.dtype)
        vp_ref[0:window, :] = jnp.zeros((window, LANES), vp_ref.dtype)
        _fill_rows(kp_ref, window, k_ref, dup)
        _fill_rows(vp_ref, window, v_ref, dup)

    for u in range(nq):
        rs = slice(u * Q_BLOCK, (u + 1) * Q_BLOCK)
        start = pl.multiple_of((pl.program_id(2) * nq + u) * Q_BLOCK, Q_BLOCK)
        ks = kp_ref[pl.ds(start, kw), :]
        vs = vp_ref[pl.ds(start, kw), :]
        s = lax.dot_general(_stack_heads_native(q_ref[rs, :]), ks, _TRANS_B, preferred_element_type=F32)
        s = s + bm_ref[...]
        kpos = start - window + lax.broadcasted_iota(jnp.int32, s.shape, 1)
        s = jnp.where(kpos >= 0, s, NEG_INF)
        if sinks:
            s = jnp.concatenate([s, sk_ref[...]], axis=1)
        m = jnp.max(s, axis=1, keepdims=True)
        p = jnp.exp2(s - m)
        l = jnp.sum(p, axis=1, keepdims=True)
        o = jnp.dot(p[:, :kw].astype(CD), vs, preferred_element_type=F32) * (1.0 / l)
        o_ref[rs, :] = _unstack_heads(o)


def _banded_attn(main3, q_col, k_col, v_col, biasmask, sink_rows, window, nq):
    b, t, _ = main3.shape
    kw = window + Q_BLOCK
    tq = nq * Q_BLOCK
    assert t % tq == 0
    sinks = sink_rows is not None
    in_specs = [pl.BlockSpec((None, tq, 2 * LANES), lambda i, g, n: (i, n, q_col + g)),
                pl.BlockSpec((None, t, LANES), lambda i, g, n: (i, 0, k_col)),
                pl.BlockSpec((None, t, LANES), lambda i, g, n: (i, 0, v_col)),
                pl.BlockSpec((None, NSA_REP * Q_BLOCK, kw), lambda i, g, n: (g, 0, 0))]
    args = [main3, main3, main3, biasmask]
    if sinks:
        in_specs.append(pl.BlockSpec((None, NSA_REP * Q_BLOCK, LANES), lambda i, g, n: (g, 0, 0)))
        args.append(sink_rows)
    return pl.pallas_call(
        functools.partial(_banded_kernel, window=window, sinks=sinks, nq=nq),
        out_shape=jax.ShapeDtypeStruct((b, t, GROUP_WIDTH), F32),
        grid=(b, NSA_KV_HEADS, t // tq),
        in_specs=in_specs,
        out_specs=pl.BlockSpec((None, tq, 2 * LANES), lambda i, g, n: (i, n, g)),
        scratch_shapes=[pltpu.VMEM((window + t, LANES), CD), pltpu.VMEM((window + t, LANES), CD)],
        compiler_params=_cparams("arbitrary", "arbitrary", "arbitrary"),
        name="swa_attn" if sinks else "nsa_win_attn",
    )(*args)


def _sel_kernel(q_ref, k_ref, v_ref, mf_ref, fq_ref, nb_ref, oc_ref, ow_ref, gt_ref, o_ref,
                ka_ref, vd_ref, qa_ref, s_ref, sn_ref, mx_ref, acc_ref):
    g = pl.program_id(1)
    n = pl.program_id(2)
    t = k_ref.shape[0]

    @pl.when(n == 0)
    def _():
        i = lax.broadcasted_iota(jnp.int32, (LANES, LANES), 0)
        j = lax.broadcasted_iota(jnp.int32, (LANES, LANES), 1)
        pick = jnp.where((j < HEAD_DIM) & (i == g * HEAD_DIM + j), 1.0, 0.0).astype(CD)
        chunk = 1024

        def body(c, _):
            off = pl.multiple_of(c * chunk, chunk)
            kk = jnp.dot(k_ref[pl.ds(off, chunk), :], pick, preferred_element_type=F32)
            ln = lax.broadcasted_iota(jnp.int32, (chunk, LANES), 1)
            kk = jnp.where((ln == HEAD_DIM) | (ln == HEAD_DIM + 1), 1.0, kk)
            ka_ref[pl.ds(off, chunk), 0:LANES] = kk.astype(ka_ref.dtype)
            key = off + lax.broadcasted_iota(jnp.int32, (chunk, LANES), 0)
            ka_ref[pl.ds(off, chunk), LANES:2 * LANES] = jnp.where(
                (key >> 6) == ln, 1.0, 0.0).astype(ka_ref.dtype)
            vd_ref[pl.ds(off, chunk), LANES:2 * LANES] = jnp.ones((chunk, LANES), vd_ref.dtype)
            return 0

        lax.fori_loop(0, t // chunk, body, 0)
        _fill_rows(vd_ref, 0, v_ref, _dup_matrix(g))

    q = q_ref[...]
    ii = lax.broadcasted_iota(jnp.int32, (LANES, LANES), 0)
    jj = lax.broadcasted_iota(jnp.int32, (LANES, LANES), 1)
    shift = jnp.where(ii == jj + HEAD_DIM, 1.0, 0.0).astype(CD)
    lom = _lane_mask(Q_BLOCK, 0, HEAD_DIM, q.dtype)
    c0, c1 = q[:, :LANES], q[:, LANES:]
    q0 = jnp.concatenate([
        c0 * lom, jnp.dot(c0, shift, preferred_element_type=F32).astype(q.dtype),
        c1 * lom, jnp.dot(c1, shift, preferred_element_type=F32).astype(q.dtype)], axis=0)
    q1 = q0 + fq_ref[...]
    mf = mf_ref[...]
    qa_ref[...] = jnp.concatenate([q1, jnp.concatenate([mf, mf, mf, mf], axis=0)], axis=1)

    rows = NSA_REP * Q_BLOCK
    tkf = s_ref.shape[2]
    tkn = sn_ref.shape[1]
    near_off = pl.multiple_of(jnp.maximum(n - 1, 0) * Q_BLOCK, Q_BLOCK)
    n_full = near_off // tkf
    rem = near_off - n_full * tkf

    half = tkf // 2

    def fold_max(mx, s):
        for i in range(s.shape[1] // LANES):
            mx = jnp.maximum(mx, s[:, i * LANES:(i + 1) * LANES])
        return mx

    def score_tiles(j0, count):
        mx = mx_ref[...]
        for gi in range(count):
            off = pl.multiple_of((j0 + gi) * tkf, tkf)
            for hf in range(2):
                s = lax.dot_general(qa_ref[...], ka_ref[pl.ds(off + hf * half, half), :], _TRANS_B,
                                    preferred_element_type=F32)
                col = hf * half + lax.broadcasted_iota(jnp.int32, (rows, half), 1)
                s = jnp.where(col < near_off - off, s, NEG_INF)
                s_ref[j0 + gi, :, hf * half:(hf + 1) * half] = s
                mx = fold_max(mx, s)
        mx_ref[...] = mx

    def value_tiles(j0, count):
        off = pl.multiple_of(j0 * tkf, tkf)
        p = jnp.concatenate([probs(s_ref[j0 + gi]) for gi in range(count)], axis=1)
        acc_ref[...] += jnp.dot(p, vd_ref[pl.ds(off, count * tkf), :], preferred_element_type=F32)

    def probs(s):
        mb = mx_ref[...]
        return jnp.concatenate([jnp.exp2(s[:, i * LANES:(i + 1) * LANES] - mb)
                                for i in range(s.shape[1] // LANES)], axis=1).astype(CD)

    mx_ref[...] = jnp.full((rows, LANES), M_INIT, F32)
    n_far = n_full + jnp.where(rem > 0, 1, 0)
    _sweep_tiles(score_tiles, n_far)
    sn = lax.dot_general(qa_ref[...], ka_ref[pl.ds(near_off, tkn), :], _TRANS_B, preferred_element_type=F32)
    sn = sn + nb_ref[jnp.minimum(n, 1)]
    sn_ref[...] = sn
    m = jnp.max(fold_max(mx_ref[...], sn), axis=1, keepdims=True)
    mx_ref[...] = jnp.broadcast_to(m, (rows, LANES))

    acc_ref[...] = jnp.dot(probs(sn_ref[...]), vd_ref[pl.ds(near_off, tkn), :], preferred_element_type=F32)
    _sweep_tiles(value_tiles, n_far)
    acc = acc_ref[...]
    o_sel = _unstack_heads(acc[:, :LANES] * (1.0 / acc[:, LANES:LANES + 1]))

    gt = jax.nn.sigmoid(gt_ref[...])
    lane2 = lax.broadcasted_iota(jnp.int32, (Q_BLOCK, 2 * LANES), 1)

    def gate(branch):
        cols = []
        for r in range(NSA_REP):
            a = 8 + 3 * r + branch
            cols.append(jnp.where(g == 0, gt[:, a:a + 1], gt[:, a + 3 * NSA_REP:a + 3 * NSA_REP + 1]))
        w = jnp.where(lane2 < HEAD_DIM, cols[0], jnp.where(lane2 < 2 * HEAD_DIM, cols[1],
                      jnp.where(lane2 < 3 * HEAD_DIM, cols[2], cols[3])))
        return w

    o_ref[...] = gate(0) * oc_ref[...] + gate(1) * o_sel + gate(2) * ow_ref[...]


def _sel_attn(main3, q_col, k_col, v_col, maskfeat, farq, nearbias, o_cmp, o_win, small3, tkf):
    b, t, _ = main3.shape
    assert t % tkf == 0 and tkf % Q_BLOCK == 0
    rows = NSA_REP * Q_BLOCK
    blk = lambda w: pl.BlockSpec((None, Q_BLOCK, w), lambda i, g, n: (i, n, g))
    return pl.pallas_call(
        _sel_kernel,
        out_shape=jax.ShapeDtypeStruct((b, t, GROUP_WIDTH), F32),
        grid=(b, NSA_KV_HEADS, t // Q_BLOCK),
        in_specs=[pl.BlockSpec((None, Q_BLOCK, 2 * LANES), lambda i, g, n: (i, n, q_col + g)),
                  pl.BlockSpec((None, t, LANES), lambda i, g, n: (i, 0, k_col)),
                  pl.BlockSpec((None, t, LANES), lambda i, g, n: (i, 0, v_col)),
                  blk(LANES),
                  pl.BlockSpec((None, rows, LANES), lambda i, g, n: (g, 0, 0)),
                  pl.BlockSpec((None, 2, rows, 2 * Q_BLOCK), lambda i, g, n: (g, 0, 0, 0)),
                  blk(2 * LANES), blk(2 * LANES),
                  pl.BlockSpec((None, Q_BLOCK, LANES), lambda i, g, n: (i, n, 4))],
        out_specs=blk(2 * LANES),
        scratch_shapes=[pltpu.VMEM((t, 2 * LANES), CD), pltpu.VMEM((t, 2 * LANES), CD),
                        pltpu.VMEM((rows, 2 * LANES), CD), pltpu.VMEM((t // tkf, rows, tkf), F32),
                        pltpu.VMEM((rows, 2 * Q_BLOCK), F32),
                        pltpu.VMEM((rows, LANES), F32), pltpu.VMEM((rows, 2 * LANES), F32)],
        compiler_params=_cparams("arbitrary", "arbitrary", "arbitrary"),
        name="nsa_sel_attn",
    )(main3, main3, main3, maskfeat, farq, nearbias, o_cmp, o_win, small3)


def _outproj_kernel(a_ref, b_ref, c_ref, d_ref, gn_ref, w_ref, h_ref, o_ref, u_ref):
    @pl.when(pl.program_id(1) == 0)
    def _():
        for k, r in enumerate((a_ref, b_ref, c_ref, d_ref)):
            sl = slice(k * GROUP_WIDTH, (k + 1) * GROUP_WIDTH)
            u_ref[:, sl] = _rms(r[...], gn_ref[:, sl]).astype(u_ref.dtype)

    o_ref[...] = h_ref[...] + jnp.dot(u_ref[...], w_ref[...], preferred_element_type=F32)


def _outproj(parts, gn, w, h, tm, tn):
    n, d = h.shape
    mix = N_GROUPS * GROUP_WIDTH
    part = pl.BlockSpec((tm, GROUP_WIDTH), lambda i, j: (i, 0))
    return pl.pallas_call(
        _outproj_kernel,
        out_shape=jax.ShapeDtypeStruct((n, d), F32),
        grid=(n // tm, d // tn),
        in_specs=[part, part, part, part,
                  pl.BlockSpec((1, mix), lambda i, j: (0, 0)),
                  pl.BlockSpec((mix, tn), lambda i, j: (0, j)),
                  pl.BlockSpec((tm, tn), lambda i, j: (i, j))],
        out_specs=pl.BlockSpec((tm, tn), lambda i, j: (i, j)),
        scratch_shapes=[pltpu.VMEM((tm, mix), CD)],
        compiler_params=_cparams("parallel", "arbitrary"),
        name="outproj",
    )(*parts, gn.reshape(1, mix), w, h)


def _mlp_kernel(*refs, final):
    if final:
        h_ref, g_ref, wu_ref, wd_ref, gf_ref, o_ref, u_ref = refs
    else:
        h_ref, g_ref, wu_ref, wd_ref, o_ref, u_ref = refs
    c = pl.program_id(1)

    @pl.when(c == 0)
    def _():
        x = h_ref[...]
        u_ref[...] = _rms(x, g_ref[...]).astype(u_ref.dtype)
        o_ref[...] = x

    m = jnp.dot(u_ref[...], wu_ref[...], preferred_element_type=F32)
    a = jnp.square(jnp.maximum(m, 0.0)).astype(CD)
    o_ref[...] += jnp.dot(a, wd_ref[...], preferred_element_type=F32)

    if final:
        @pl.when(c == pl.num_programs(1) - 1)
        def _():
            o_ref[...] = _rms(o_ref[...], gf_ref[...])


def _mlp(h, g, wu, wd, gf, tm, tf):
    n, d = h.shape
    dff = wu.shape[1]
    final = gf is not None
    in_specs = [pl.BlockSpec((tm, d), lambda i, c: (i, 0)),
                pl.BlockSpec((1, d), lambda i, c: (0, 0)),
                pl.BlockSpec((d, tf), lambda i, c: (0, c)),
                pl.BlockSpec((tf, d), lambda i, c: (c, 0))]
    args = [h, g.reshape(1, d), wu, wd]
    if final:
        in_specs.append(pl.BlockSpec((1, d), lambda i, c: (0, 0)))
        args.append(gf.reshape(1, d))
    return pl.pallas_call(
        functools.partial(_mlp_kernel, final=final),
        out_shape=jax.ShapeDtypeStruct((n, d), F32),
        grid=(n // tm, dff // tf),
        in_specs=in_specs,
        out_specs=pl.BlockSpec((tm, d), lambda i, c: (i, 0)),
        scratch_shapes=[pltpu.VMEM((tm, d), CD)],
        compiler_params=_cparams("parallel", "arbitrary"),
        name="mlp",
    )(*args)


def _t5_bucket_np(dist):
    max_exact = REL_BUCKETS // 2
    d = np.maximum(dist, 0)
    ratio = np.log(np.maximum(d, 1).astype(np.float32) / np.float32(max_exact)) / np.float32(
        math.log(REL_MAX_DIST / max_exact))
    large = max_exact + (ratio * np.float32(REL_BUCKETS - max_exact)).astype(np.int32)
    large = np.minimum(large, REL_BUCKETS - 1)
    return np.where(d < max_exact, d, large)


def _bias_tables(rel_bias):
    tbl = (rel_bias.astype(F32) * LOG2E).T.reshape(2, NSA_KV_HEADS, NSA_REP, REL_BUCKETS)
    tbl_nsa, tbl_swa = tbl[0], tbl[1]

    def lookup(tb, bucket):
        onehot = (jnp.asarray(bucket, jnp.int32)[None] == jnp.arange(REL_BUCKETS)[:, None, None]).astype(F32)
        return jnp.einsum('grb,bij->grij', tb, onehot, precision=lax.Precision.HIGHEST)
    i = np.arange(Q_BLOCK)[:, None]
    far_bucket = REL_BUCKETS - 1

    def banded(tb, window):
        jk = np.arange(window + Q_BLOCK)[None, :]
        dist = i + window - jk
        ok = (dist >= 0) & (dist < window)
        vals = lookup(tb, _t5_bucket_np(dist))
        vals = jnp.where(jnp.asarray(ok)[None, None], vals, NEG_INF)
        return vals.reshape(NSA_KV_HEADS, NSA_REP * Q_BLOCK, window + Q_BLOCK)

    bm_win = banded(tbl_nsa, NSA_WINDOW)
    bm_swa = banded(tbl_swa, SWA_WINDOW)

    far = tbl_nsa[:, :, far_bucket]
    jk = np.arange(2 * Q_BLOCK)[None, :]
    tiles = []
    for first_key_back in (0, Q_BLOCK):
        dist = i + first_key_back - jk
        delta = lookup(tbl_nsa, _t5_bucket_np(dist)) - far[:, :, None, None]
        tiles.append(jnp.where(jnp.asarray(dist >= 0)[None, None], delta, NEG_INF))
    near = jnp.stack(tiles, axis=1).reshape(NSA_KV_HEADS, 2, NSA_REP * Q_BLOCK, 2 * Q_BLOCK)
    far_rows = jnp.broadcast_to(far[:, :, None], (NSA_KV_HEADS, NSA_REP, Q_BLOCK)).reshape(NSA_KV_HEADS, -1)
    hi, lo_ = _split_parts(far_rows, 2)
    zeros = lambda k: jnp.zeros((NSA_KV_HEADS, NSA_REP * Q_BLOCK, k), CD)
    farq = jnp.concatenate([zeros(HEAD_DIM), hi[..., None], lo_[..., None], zeros(LANES - HEAD_DIM - 2)], axis=-1)

    u = np.arange(16)[None, :]
    dist_c = i - 16 * u + 113
    band = lookup(tbl_nsa, _t5_bucket_np(dist_c))
    band = jnp.where(jnp.asarray(dist_c >= 0)[None, None], band, MASK_NEG)
    band = band.reshape(NSA_KV_HEADS, NSA_REP * Q_BLOCK, 16)
    bh, bl = _split_parts(band, 2)
    neg = jnp.full((NSA_KV_HEADS, NSA_REP * Q_BLOCK, 1), MASK_NEG, CD)
    pq = jnp.concatenate([bh, bl, hi[..., None], lo_[..., None], neg, zeros(LANES - 35)], axis=-1)
    return bm_win, bm_swa, near, farq, pq


def _overlap_matrix(nc_pad, ns):
    ci = np.arange(nc_pad)[:, None]
    sj = np.arange(LANES)[None, :]
    ov = ((ci * NSA_CMP_STRIDE + NSA_CMP_LEN - 1 >= sj * NSA_SEL_LEN)
          & (ci * NSA_CMP_STRIDE <= sj * NSA_SEL_LEN + NSA_SEL_LEN - 1) & (sj < ns))
    return jnp.asarray(ov.astype(np.float32), CD)


def _layout_w_in(w):
    w = w.astype(F32)
    sc = HEAD_DIM ** -0.5 * LOG2E
    z = lambda k: jnp.zeros((w.shape[0], k), F32)
    main = jnp.concatenate([w[:, 544:1056] * sc, w[:, 1056:2080], w[:, 2088:2600] * sc, w[:, 3392:3904] * sc,
                            w[:, 2600:3368], w[:, 3904:4160]], axis=1)
    kr = w[:, 512:544]
    blk_a = jnp.concatenate([w[:, 2080:2088], w[:, 3368:3392], z(32), kr, z(32)], axis=1)
    blk_b = jnp.concatenate([z(64), -kr[:, 16:], kr[:, :16], z(32)], axis=1)
    small = jnp.concatenate([w[:, 0:512], blk_a, blk_b], axis=1)
    return main.astype(CD), small.astype(CD)


_FOX_Q, _FOX_K512, _FOX_V = 0, 1, 8
_NSA_Q256, _SWA_Q256 = 6, 8
_KVC_COL = 2560
_KSEL, _VSEL, _KWIN, _VWIN, _KSWA, _VSWA = 22, 23, 24, 25, 26, 27


def _layout_mla(w_uq, w_ukv):
    w3 = w_uq.astype(F32).reshape(MLA_Q_RANK, GROUP_HEADS, MLA_NOPE + MLA_ROPE)
    nope, rp = w3[:, :, :MLA_NOPE], w3[:, :, MLA_NOPE:]
    half = MLA_ROPE // 2
    sw = jnp.concatenate([-rp[:, :, half:], rp[:, :, :half]], axis=-1)
    z = lambda k: jnp.zeros((MLA_Q_RANK, GROUP_HEADS, k), F32)
    plain = jnp.concatenate([nope, rp, z(32)], axis=-1).reshape(MLA_Q_RANK, -1)
    swapped = jnp.concatenate([z(64), sw, z(32)], axis=-1).reshape(MLA_Q_RANK, -1)
    wq = jnp.concatenate([plain, swapped], axis=1)
    k3 = w_ukv.astype(F32).reshape(MLA_KV_RANK, GROUP_HEADS, 2 * HEAD_DIM)
    wk = jnp.concatenate([k3[:, :, :MLA_NOPE], jnp.zeros((MLA_KV_RANK, GROUP_HEADS, HEAD_DIM), F32)],
                         axis=-1).reshape(MLA_KV_RANK, -1)
    wv = k3[:, :, MLA_NOPE:].reshape(MLA_KV_RANK, -1)
    return wq.astype(CD), wk.astype(CD), wv.astype(CD)


def _rope_tables(t):
    inv = ROPE_THETA ** (-jnp.arange(0, MLA_ROPE, 2, dtype=F32) / MLA_ROPE)
    ang = jnp.arange(t).astype(F32)[:, None] * inv[None, :]
    cc = jnp.concatenate([jnp.cos(ang)] * 2, axis=1)
    ss = jnp.concatenate([jnp.sin(ang)] * 2, axis=1)
    one, z64, z32 = jnp.ones((t, 64), F32), jnp.zeros((t, 64), F32), jnp.zeros((t, 32), F32)
    return (jnp.concatenate([one, cc, z32], axis=1), jnp.concatenate([z64, ss, z32], axis=1),
            jnp.concatenate([z64, cc, z32], axis=1))


def kernel(x, norm_attn, w_in, mla_q_norm, mla_w_uq, mla_kv_norm, mla_w_ukv, fox_b_f, nsa_cmp_pos,
           nsa_cmp_w1, nsa_cmp_w2, swa_sinks, group_norm, w_out, norm_mlp, w_up, w_down, rel_bias,
           final_norm):
    b, t, d = x.shape
    n = b * t
    depth = w_in.shape[0]
    assert t % 1024 == 0 and d == N_GROUPS * GROUP_WIDTH
    tm = 1024 if n % 1024 == 0 else 512
    tq = 512
    nq = 4
    nr = t // NSA_CMP_STRIDE

    cosq, sinq, cosk = _rope_tables(t)
    bm_win, bm_swa, nearbias, farq, pq = _bias_tables(rel_bias)
    overlap = _overlap_matrix(nr, t // NSA_SEL_LEN)

    h = x.reshape(n, d).astype(F32)
    for l in range(depth):
        w_main, w_small = _layout_w_in(w_in[l])
        main = _norm_matmul(h, norm_attn[l], w_main, CD, tm, 512)
        small = _norm_matmul(h, norm_attn[l], w_small, F32, tm, 768)
        main3 = main.reshape(b, t, -1)
        small3 = small.reshape(b, t, -1)

        wq, wk, wv = _layout_mla(mla_w_uq[l], mla_w_ukv[l])
        q_m, k_m, v_m = _mla_prep(small, mla_q_norm[l], mla_kv_norm[l], wq, wk, wv, cosq, sinq, cosk, t, 512)
        o_mla = _dense_attn(q_m.reshape(b, t, -1), k_m.reshape(b, t, -1), v_m.reshape(b, t, -1), None,
                            fox=False, q_col=0, v_col=0, tq=tq)

        k_aug, fq = _fox_prep(small3, fox_b_f[l], main3, _FOX_K512, 512)
        o_fox = _dense_attn(main3, k_aug, main3, fq, fox=True, q_col=_FOX_Q, v_col=_FOX_V, tq=tq)

        kv16 = main3[:, :, _KVC_COL:_KVC_COL + 256].reshape(b, nr, NSA_CMP_STRIDE * 256)
        cmp = _nsa_compress(kv16, nsa_cmp_pos[l], nsa_cmp_w1[l], nsa_cmp_w2[l])
        o_cmp, maskfeat = _cmp_select(main3, _NSA_Q256, cmp, pq, overlap, nq)
        o_win = _banded_attn(main3, _NSA_Q256, _KWIN, _VWIN, bm_win, None, NSA_WINDOW, nq)
        o_nsa = _sel_attn(main3, _NSA_Q256, _KSEL, _VSEL, maskfeat, farq, nearbias, o_cmp, o_win, small3, tq)

        sink = (swa_sinks[l].astype(F32) * LOG2E).reshape(NSA_KV_HEADS, NSA_REP, 1, 1)
        sink_rows = jnp.concatenate(
            [jnp.broadcast_to(sink, (NSA_KV_HEADS, NSA_REP, Q_BLOCK, 1)),
             jnp.full((NSA_KV_HEADS, NSA_REP, Q_BLOCK, LANES - 1), NEG_INF, F32)],
            axis=-1).reshape(NSA_KV_HEADS, NSA_REP * Q_BLOCK, LANES)
        o_swa = _banded_attn(main3, _SWA_Q256, _KSWA, _VSWA, bm_swa, sink_rows, SWA_WINDOW, nq)

        parts = [o.reshape(n, GROUP_WIDTH) for o in (o_mla, o_fox, o_nsa, o_swa)]
        h = _outproj(parts, group_norm[l], w_out[l].astype(CD), h, 512, d)
        gf = final_norm if l == depth - 1 else None
        h = _mlp(h, norm_mlp[l], w_up[l].astype(CD), w_down[l].astype(CD), gf, 512, 1024)
    return h.reshape(b, t, d).astype(x.dtype)
```

```python
import functools
import math

import numpy as np
import jax
import jax.numpy as jnp
from jax import lax
from jax.experimental import pallas as pl
from jax.experimental.pallas import tpu as pltpu

HEAD_DIM = 64
GROUP_HEADS = 8
GROUP_WIDTH = GROUP_HEADS * HEAD_DIM
N_GROUPS = 4
Q_BLOCK = 128
EPS = 1e-6
NEG_INF = -1e30
BIG = 1e9

MLA_Q_RANK = 384
MLA_KV_RANK = 128
MLA_NOPE = 64
MLA_ROPE = 32
ROPE_THETA = 10000.0

NSA_KV_HEADS = 2
NSA_REP = GROUP_HEADS // NSA_KV_HEADS
NSA_CMP_LEN = 32
NSA_CMP_STRIDE = 16
NSA_SEL_LEN = 64
NSA_TOP_N = 8
NSA_WINDOW = 256
SWA_WINDOW = 128

REL_BUCKETS = 32
REL_MAX_DIST = 128

V7X_VMEM_BYTES = 64 * 2**20
VMEM_LIMIT = (V7X_VMEM_BYTES * 7) // 8
LANES = 128
CD = jnp.bfloat16
F32 = jnp.float32
MASK_NEG = -(2.0 ** 80)
M_INIT = -(2.0 ** 100)
TILE_GROUP = 8
LOG2E = math.log2(math.e)
MLA_SCORE_SCALE = (MLA_NOPE + MLA_ROPE) ** -0.5 * LOG2E
_TRANS_B = (((1,), (1,)), ((), ()))


def _cparams(*sem):
    return pltpu.CompilerParams(dimension_semantics=sem, vmem_limit_bytes=VMEM_LIMIT)


def _split_parts(x, n):
    parts, r = [], x
    for _ in range(n):
        p = r.astype(CD)
        parts.append(p)
        r = r - p.astype(F32)
    return parts


def _sweep_tiles(fn, n_tiles):
    groups = n_tiles // TILE_GROUP

    def grouped(i, _):
        fn(i * TILE_GROUP, TILE_GROUP)
        return 0

    lax.fori_loop(0, groups, grouped, 0)
    base = groups * TILE_GROUP
    rem = n_tiles - base
    piece = TILE_GROUP // 2
    while piece >= 1:
        @pl.when((rem & piece) != 0)
        def _(piece=piece):
            fn(base + (rem & ~(2 * piece - 1)), piece)
        piece //= 2


def _rms(x, g):
    return x * lax.rsqrt(jnp.mean(x * x, axis=-1, keepdims=True) + EPS) * g


def _norm_matmul_kernel(x_ref, g_ref, w_ref, o_ref, u_ref):
    @pl.when(pl.program_id(1) == 0)
    def _():
        u_ref[...] = _rms(x_ref[...], g_ref[...]).astype(u_ref.dtype)

    o_ref[...] = jnp.dot(u_ref[...], w_ref[...], preferred_element_type=F32).astype(o_ref.dtype)


def _norm_matmul(x, g, w, out_dtype, tm, tn):
    n, d = x.shape
    nc = w.shape[1]
    return pl.pallas_call(
        _norm_matmul_kernel,
        out_shape=jax.ShapeDtypeStruct((n, nc), out_dtype),
        grid=(n // tm, nc // tn),
        in_specs=[
            pl.BlockSpec((tm, d), lambda i, j: (i, 0)),
            pl.BlockSpec((1, d), lambda i, j: (0, 0)),
            pl.BlockSpec((d, tn), lambda i, j: (0, j)),
        ],
        out_specs=pl.BlockSpec((tm, tn), lambda i, j: (i, j)),
        scratch_shapes=[pltpu.VMEM((tm, d), CD)],
        compiler_params=_cparams("parallel", "arbitrary"),
        name="norm_matmul",
    )(x, g.reshape(1, d), w)


def _mla_prep_kernel(sm_ref, qn_ref, kvn_ref, wq_ref, wk_ref, wv_ref, cq_ref, sq_ref, ck_ref,
                     q_ref, k_ref, v_ref):
    sm = sm_ref[...]
    nq = _rms(sm[:, :MLA_Q_RANK], qn_ref[...]).astype(CD)
    nkv = _rms(sm[:, MLA_Q_RANK:MLA_Q_RANK + MLA_KV_RANK], kvn_ref[...]).astype(CD)
    blk_a = sm[:, 512:640]
    blk_b = sm[:, 640:768]
    cosq, sinq, cosk = cq_ref[...], sq_ref[...], ck_ref[...]
    hw = GROUP_HEADS * LANES
    qq = jnp.dot(nq, wq_ref[...], preferred_element_type=F32)
    kk = jnp.dot(nkv, wk_ref[...], preferred_element_type=F32)
    kpe = blk_a * cosk + blk_b * sinq
    for h in range(GROUP_HEADS):
        sl = slice(h * LANES, (h + 1) * LANES)
        qh = qq[:, sl] * cosq + qq[:, hw + h * LANES:hw + (h + 1) * LANES] * sinq
        q_ref[:, sl] = (qh * MLA_SCORE_SCALE).astype(q_ref.dtype)
        k_ref[:, sl] = (kk[:, sl] + kpe).astype(k_ref.dtype)
    v_ref[...] = jnp.dot(nkv, wv_ref[...], preferred_element_type=F32).astype(v_ref.dtype)


def _mla_prep(small, qn, kvn, wq, wk, wv, cosq, sinq, cosk, t, tm):
    n = small.shape[0]
    tb = t // tm
    hw = GROUP_HEADS * LANES
    full = lambda a: pl.BlockSpec(a.shape, lambda i: (0,) * a.ndim)
    tab = pl.BlockSpec((tm, LANES), lambda i: (i % tb, 0))
    qn = qn.reshape(1, -1)
    kvn = kvn.reshape(1, -1)
    return pl.pallas_call(
        _mla_prep_kernel,
        out_shape=(jax.ShapeDtypeStruct((n, hw), CD), jax.ShapeDtypeStruct((n, hw), CD),
                   jax.ShapeDtypeStruct((n, GROUP_WIDTH), CD)),
        grid=(n // tm,),
        in_specs=[pl.BlockSpec((tm, small.shape[1]), lambda i: (i, 0)), full(qn), full(kvn),
                  full(wq), full(wk), full(wv), tab, tab, tab],
        out_specs=(pl.BlockSpec((tm, hw), lambda i: (i, 0)), pl.BlockSpec((tm, hw), lambda i: (i, 0)),
                   pl.BlockSpec((tm, GROUP_WIDTH), lambda i: (i, 0))),
        compiler_params=_cparams("parallel"),
        name="mla_prep",
    )(small, qn, kvn, wq, wk, wv, cosq, sinq, cosk)


def _fox_prep_kernel(fl_ref, bf_ref, k_ref, efq_ref, ekf_ref, kaug_ref, fq_ref, carry_ref, *, tc):
    @pl.when(pl.program_id(1) == 0)
    def _():
        carry_ref[...] = jnp.zeros_like(carry_ref)

    x = fl_ref[...] + bf_ref[...]
    logf = -(jnp.maximum(-x, 0.0) + jnp.log1p(jnp.exp(-jnp.abs(x))))
    lane = lax.broadcasted_iota(jnp.int32, logf.shape, 1)
    logf = jnp.where(lane < GROUP_HEADS, logf, 0.0)
    row = lax.broadcasted_iota(jnp.int32, (tc, tc), 0)
    col = lax.broadcasted_iota(jnp.int32, (tc, tc), 1)
    tri = jnp.where(row >= col, 1.0, 0.0).astype(CD)
    cs = jnp.zeros(logf.shape, F32)
    for part in _split_parts(logf, 3):
        cs = cs + jnp.dot(tri, part, preferred_element_type=F32)
    fc = cs + carry_ref[...]
    carry_ref[...] = fc[tc - 1:tc, :]
    parts = _split_parts(fc * LOG2E, 3)
    fq = jnp.zeros(fq_ref.shape, F32)
    kf = jnp.zeros(fq_ref.shape, F32)
    for i, part in enumerate(parts):
        fq = fq + jnp.dot(part, efq_ref[...], preferred_element_type=F32)
        kf = kf + jnp.dot(part, ekf_ref[i], preferred_element_type=F32)
    fq_ref[...] = fq
    k = k_ref[...]
    for p in range(GROUP_HEADS // 2):
        kaug_ref[:, 2 * p * LANES:(2 * p + 1) * LANES] = k[:, p * LANES:(p + 1) * LANES]
        kaug_ref[:, (2 * p + 1) * LANES:(2 * p + 2) * LANES] = kf[:, p * LANES:(p + 1) * LANES].astype(kaug_ref.dtype)


def _fox_prep(small3, b_f, main3, k_col, tc):
    b, t, _ = small3.shape
    pairs = GROUP_HEADS // 2
    efq = np.zeros((LANES, pairs * LANES), np.float32)
    ekf = np.zeros((3, LANES, pairs * LANES), np.float32)
    for h in range(GROUP_HEADS):
        p, a = divmod(h, 2)
        efq[h, p * LANES + a] = 1.0
        for i in range(3):
            ekf[i, h, p * LANES + 3 * a + i] = -1.0
    bf = jnp.zeros((1, LANES), F32).at[0, :GROUP_HEADS].set(b_f.astype(F32))
    return pl.pallas_call(
        functools.partial(_fox_prep_kernel, tc=tc),
        out_shape=(jax.ShapeDtypeStruct((b, t, 2 * GROUP_WIDTH), CD),
                   jax.ShapeDtypeStruct((b, t, GROUP_WIDTH), F32)),
        grid=(b, t // tc),
        in_specs=[pl.BlockSpec((None, tc, LANES), lambda i, j: (i, j, 4)),
                  pl.BlockSpec((1, LANES), lambda i, j: (0, 0)),
                  pl.BlockSpec((None, tc, GROUP_WIDTH), lambda i, j: (i, j, k_col)),
                  pl.BlockSpec(efq.shape, lambda i, j: (0, 0)),
                  pl.BlockSpec(ekf.shape, lambda i, j: (0, 0, 0))],
        out_specs=(pl.BlockSpec((None, tc, 2 * GROUP_WIDTH), lambda i, j: (i, j, 0)),
                   pl.BlockSpec((None, tc, GROUP_WIDTH), lambda i, j: (i, j, 0))),
        scratch_shapes=[pltpu.VMEM((1, LANES), F32)],
        compiler_params=_cparams("arbitrary", "arbitrary"),
        name="fox_prep",
    )(small3, bf, main3, jnp.asarray(efq, CD), jnp.asarray(ekf, CD))


def _dense_attn_kernel(*refs, fox, tq):
    if fox:
        q_ref, k_ref, v_ref, fq_ref, o_ref, va_ref, qa_ref, s_ref, mx_ref, acc_ref = refs
    else:
        q_ref, k_ref, v_ref, o_ref, va_ref, qa_ref, s_ref, mx_ref, acc_ref = refs
    qi = pl.program_id(2)
    t = k_ref.shape[0]
    nch = tq // LANES

    @pl.when(qi == 0)
    def _():
        chunk = 1024

        def body(i, _):
            off = pl.multiple_of(i * chunk, chunk)
            va_ref[pl.ds(off, chunk), 0:LANES] = v_ref[pl.ds(off, chunk), :]
            va_ref[pl.ds(off, chunk), LANES:2 * LANES] = jnp.ones((chunk, LANES), va_ref.dtype)
            return 0

        lax.fori_loop(0, t // chunk, body, 0)

    half = tq // 2
    outs = []
    for h in range(2):
        q = q_ref[...]
        if fox:
            qa_ref[...] = jnp.concatenate(
                [q * _lane_mask(tq, h * HEAD_DIM, (h + 1) * HEAD_DIM, q.dtype), _lane_mask(tq, 3 * h, 3 * h + 3, q.dtype)],
                axis=1)
        else:
            qa_ref[...] = q[:, h * LANES:(h + 1) * LANES]

        def score_tiles(j0, count, h=h):
            mx = mx_ref[...]
            for gi in range(count):
                off = pl.multiple_of((j0 + gi) * tq, tq)
                for hf in range(2):
                    rows = pl.ds(off + hf * half, half)
                    k = k_ref[rows, :] if fox else k_ref[rows, pl.ds(h * LANES, LANES)]
                    s = lax.dot_general(qa_ref[...], k, _TRANS_B, preferred_element_type=F32)
                    delta = (hf * half + lax.broadcasted_iota(jnp.int32, (tq, half), 1)
                             - lax.broadcasted_iota(jnp.int32, (tq, half), 0))
                    s = jnp.where(delta <= (qi - j0 - gi) * tq, s, NEG_INF)
                    s_ref[j0 + gi, :, hf * half:(hf + 1) * half] = s
                    for i in range(half // LANES):
                        mx = jnp.maximum(mx, s[:, i * LANES:(i + 1) * LANES])
            mx_ref[...] = mx

        def value_tiles(j0, count):
            mc = mx_ref[...]
            ps = []
            for gi in range(count):
                s = s_ref[j0 + gi]
                ps += [jnp.exp2(s[:, i * LANES:(i + 1) * LANES] - mc) for i in range(nch)]
            off = pl.multiple_of(j0 * tq, tq)
            acc_ref[...] += jnp.dot(jnp.concatenate(ps, axis=1).astype(CD), va_ref[pl.ds(off, count * tq), :],
                                    preferred_element_type=F32)

        mx_ref[...] = jnp.full(mx_ref.shape, M_INIT, F32)
        _sweep_tiles(score_tiles, qi + 1)
        m = jnp.max(mx_ref[...], axis=1, keepdims=True)
        if fox:
            f = fq_ref[:, h:h + 1]
            m = (m + f) - f
        mx_ref[...] = jnp.broadcast_to(m, (tq, LANES))

        acc_ref[...] = jnp.zeros(acc_ref.shape, F32)
        _sweep_tiles(value_tiles, qi + 1)
        acc = acc_ref[...]
        outs.append(acc[:, :LANES] * (1.0 / acc[:, LANES:LANES + 1]))
    lane = lax.broadcasted_iota(jnp.int32, (tq, LANES), 1)
    o_ref[...] = jnp.where(lane < HEAD_DIM, outs[0], outs[1]).astype(o_ref.dtype)


def _dense_attn(q3, k3, v3, fq3, *, fox, q_col, v_col, tq):
    b, t, _ = q3.shape
    assert t % tq == 0 and t % 1024 == 0
    pairs = GROUP_HEADS // 2
    qw = LANES if fox else 2 * LANES
    kd = 2 * LANES if fox else LANES
    in_specs = [pl.BlockSpec((None, tq, qw), lambda i, p, j: (i, j, q_col + p)),
                pl.BlockSpec((None, t, 2 * LANES), lambda i, p, j: (i, 0, p)),
                pl.BlockSpec((None, t, LANES), lambda i, p, j: (i, 0, v_col + p))]
    args = [q3, k3, v3]
    if fox:
        in_specs.append(pl.BlockSpec((None, tq, LANES), lambda i, p, j: (i, j, p)))
        args.append(fq3)
    return pl.pallas_call(
        functools.partial(_dense_attn_kernel, fox=fox, tq=tq),
        out_shape=jax.ShapeDtypeStruct((b, t, GROUP_WIDTH), F32),
        grid=(b, pairs, t // tq),
        in_specs=in_specs,
        out_specs=pl.BlockSpec((None, tq, LANES), lambda i, p, j: (i, j, p)),
        scratch_shapes=[pltpu.VMEM((t, 2 * LANES), CD), pltpu.VMEM((tq, kd), CD),
                        pltpu.VMEM((t // tq, tq, tq), F32), pltpu.VMEM((tq, LANES), F32),
                        pltpu.VMEM((tq, 2 * LANES), F32)],
        compiler_params=_cparams("arbitrary", "arbitrary", "arbitrary"),
        name="fox_attn" if fox else "mla_attn",
    )(*args)


def _gelu_tanh(x):
    return 0.5 * x * (1.0 + jnp.tanh(math.sqrt(2.0 / math.pi) * (x + 0.044715 * (x * x * x))))


def _compress_kernel(r_ref, w1a_ref, w1b_ref, pa_ref, pb_ref, w2_ref, o_ref):
    r = r_ref[...]
    nr = r.shape[0]
    ya = jnp.dot(r, w1a_ref[...], preferred_element_type=F32)
    yb = jnp.dot(r, w1b_ref[...], preferred_element_type=F32)
    pc = (jnp.dot(pa_ref[...], w1a_ref[...], preferred_element_type=F32)
          + jnp.dot(pb_ref[...], w1b_ref[...], preferred_element_type=F32))[0:1, :]
    pre = ya + pltpu.roll(yb, nr - 1, 0) + pc
    h1 = _gelu_tanh(pre).astype(CD)
    rowi = lax.broadcasted_iota(jnp.int32, (nr, LANES), 0)
    for g in range(NSA_KV_HEADS):
        y = jnp.dot(h1, w2_ref[g], preferred_element_type=F32)
        o_ref[g] = jnp.where(rowi < nr - 1, y, 0.0).astype(o_ref.dtype)


def _nsa_compress(kv16, cmp_pos, cmp_w1, cmp_w2):
    b, nr, kw = kv16.shape
    half = NSA_CMP_LEN // 2
    eye2 = jnp.eye(2, dtype=F32)
    w1r = cmp_w1.astype(F32).reshape(2, NSA_CMP_LEN, HEAD_DIM, HEAD_DIM)
    expand = lambda w: jnp.einsum('klde,kw,gh->klwgdhe', w, eye2, eye2).reshape(2, kw, LANES).astype(CD)
    w1a, w1b = expand(w1r[:, :half]), expand(w1r[:, half:])
    posr = cmp_pos.astype(F32)
    tile = lambda p: jnp.broadcast_to(p[:, None, :, None, None, :], (2, 8, half, 2, 2, HEAD_DIM)).reshape(2, 8, kw).astype(CD)
    pa, pb = tile(posr[:, :half]), tile(posr[:, half:])
    w2 = cmp_w2.astype(F32)
    w2d = jnp.zeros((2, NSA_KV_HEADS, LANES, LANES), F32)
    for g in range(NSA_KV_HEADS):
        blk = jnp.concatenate([w2, w2], axis=2)
        w2d = w2d.at[:, g, g * HEAD_DIM:(g + 1) * HEAD_DIM, :].set(blk)
    w2d = w2d.astype(CD)
    return pl.pallas_call(
        _compress_kernel,
        out_shape=jax.ShapeDtypeStruct((b, 2, NSA_KV_HEADS, nr, LANES), CD),
        grid=(b, 2),
        in_specs=[pl.BlockSpec((None, nr, kw), lambda i, k: (i, 0, 0)),
                  pl.BlockSpec((None, kw, LANES), lambda i, k: (k, 0, 0)),
                  pl.BlockSpec((None, kw, LANES), lambda i, k: (k, 0, 0)),
                  pl.BlockSpec((None, 8, kw), lambda i, k: (k, 0, 0)),
                  pl.BlockSpec((None, 8, kw), lambda i, k: (k, 0, 0)),
                  pl.BlockSpec((None, NSA_KV_HEADS, LANES, LANES), lambda i, k: (k, 0, 0, 0))],
        out_specs=pl.BlockSpec((None, None, NSA_KV_HEADS, nr, LANES), lambda i, k: (i, k, 0, 0, 0)),
        compiler_params=_cparams("parallel", "arbitrary"),
        name="nsa_compress",
    )(kv16, w1a, w1b, pa, pb, w2d)


def _lane_mask(rows, lo, hi, dtype):
    lane = lax.broadcasted_iota(jnp.int32, (rows, LANES), 1)
    return jnp.where((lane >= lo) & (lane < hi), 1.0, 0.0).astype(dtype)


def _stack_heads_native(q):
    lo = _lane_mask(Q_BLOCK, 0, HEAD_DIM, q.dtype)
    hi = _lane_mask(Q_BLOCK, HEAD_DIM, LANES, q.dtype)
    c0, c1 = q[:, :LANES], q[:, LANES:]
    return jnp.concatenate([c0 * lo, c0 * hi, c1 * lo, c1 * hi], axis=0)


def _unstack_heads(o):
    lane = lax.broadcasted_iota(jnp.int32, (Q_BLOCK, LANES), 1)
    lo = lane < HEAD_DIM
    return jnp.concatenate([jnp.where(lo, o[0:128], o[128:256]), jnp.where(lo, o[256:384], o[384:512])], axis=1)


def _dup_matrix(g):
    i = lax.broadcasted_iota(jnp.int32, (LANES, LANES), 0)
    j = lax.broadcasted_iota(jnp.int32, (LANES, LANES), 1)
    return jnp.where(i == g * HEAD_DIM + (j & (HEAD_DIM - 1)), 1.0, 0.0).astype(CD)


def _fill_rows(dst_ref, row0, src_ref, mat, chunk=1024):
    n = src_ref.shape[0]

    def body(i, _):
        off = pl.multiple_of(i * chunk, chunk)
        dst_ref[pl.ds(row0 + off, chunk), 0:LANES] = jnp.dot(
            src_ref[pl.ds(off, chunk), :], mat, preferred_element_type=F32).astype(dst_ref.dtype)
        return 0

    lax.fori_loop(0, n // chunk, body, 0)


def _cmp_select_kernel(q_ref, kc_ref, vc_ref, pq_ref, ov_ref, o_ref, mf_ref, *, nq):
    nc = kc_ref.shape[0]
    ci = lax.broadcasted_iota(jnp.int32, (nc, LANES), 0)
    f = lax.broadcasted_iota(jnp.int32, (nc, LANES), 1)
    j = lax.broadcasted_iota(jnp.int32, (Q_BLOCK, LANES), 1)
    i = lax.broadcasted_iota(jnp.int32, (Q_BLOCK, LANES), 0)
    jf = j.astype(F32)
    one_if = lambda cond: jnp.where(cond, 1.0, 0.0)
    for u in range(nq):
        n = pl.program_id(2) * nq + u
        rs = slice(u * Q_BLOCK, (u + 1) * Q_BLOCK)
        qaug = jnp.concatenate([_stack_heads_native(q_ref[rs, :]), pq_ref[...]], axis=1)
        uc = ci - 8 * n + 9
        feat = jnp.where(f < 32, one_if(uc == (f & 15)),
                         jnp.where(f < 34, one_if(uc < 0), one_if((f == 34) & (uc > 15)))).astype(CD)
        kaug = jnp.concatenate([kc_ref[...], feat], axis=1)
        s = lax.dot_general(qaug, kaug, _TRANS_B, preferred_element_type=F32)
        m = jnp.max(s, axis=1, keepdims=True)
        p = jnp.exp2(s - m)
        l = jnp.sum(p, axis=1, keepdims=True)
        pc = p * jnp.where(m > 0.5 * MASK_NEG, 1.0 / l, 0.0)
        o = jnp.dot(pc.astype(CD), vc_ref[...], preferred_element_type=F32)
        o_ref[rs, :] = _unstack_heads(o)
        pcs = pc[0:128] + pc[128:256] + pc[256:384] + pc[384:512]
        imp = jnp.dot(pcs.astype(CD), ov_ref[...], preferred_element_type=F32)
        cur = 2 * n + jnp.where(i >= NSA_SEL_LEN, 1, 0)
        causal = j <= cur
        forced = (j == 0) | (j == cur) | (j == cur - 1)
        sel = one_if(forced & causal)
        score = jnp.where(causal & jnp.logical_not(forced), imp, -3.0e38)
        for _ in range(NSA_TOP_N - 3):
            mx = jnp.max(score, axis=1, keepdims=True)
            idx = jnp.min(jnp.where(score == mx, jf, float(LANES)), axis=1, keepdims=True)
            pick = jf == idx
            sel = jnp.where(pick, 1.0, sel)
            score = jnp.where(pick, -3.0e38, score)
        mf_ref[rs, :] = jnp.where(sel > 0.5, 0.0, MASK_NEG).astype(mf_ref.dtype)


def _cmp_select(main3, q_col, cmp, pq, overlap, nq):
    b, t, _ = main3.shape
    nc = cmp.shape[3]
    tq = nq * Q_BLOCK
    assert t % tq == 0
    return pl.pallas_call(
        functools.partial(_cmp_select_kernel, nq=nq),
        out_shape=(jax.ShapeDtypeStruct((b, t, GROUP_WIDTH), F32),
                   jax.ShapeDtypeStruct((b, t, NSA_KV_HEADS * LANES), CD)),
        grid=(b, NSA_KV_HEADS, t // tq),
        in_specs=[pl.BlockSpec((None, tq, 2 * LANES), lambda i, g, n: (i, n, q_col + g)),
                  pl.BlockSpec((None, None, None, nc, LANES), lambda i, g, n: (i, 0, g, 0, 0)),
                  pl.BlockSpec((None, None, None, nc, LANES), lambda i, g, n: (i, 1, g, 0, 0)),
                  pl.BlockSpec((None, NSA_REP * Q_BLOCK, LANES), lambda i, g, n: (g, 0, 0)),
                  pl.BlockSpec(overlap.shape, lambda i, g, n: (0, 0))],
        out_specs=(pl.BlockSpec((None, tq, 2 * LANES), lambda i, g, n: (i, n, g)),
                   pl.BlockSpec((None, tq, LANES), lambda i, g, n: (i, n, g))),
        compiler_params=_cparams("parallel", "parallel", "arbitrary"),
        name="nsa_cmp_select",
    )(main3, cmp, cmp, pq, overlap)


def _banded_kernel(*refs, window, sinks, nq):
    if sinks:
        q_ref, k_ref, v_ref, bm_ref, sk_ref, o_ref, kp_ref, vp_ref = refs
    else:
        q_ref, k_ref, v_ref, bm_ref, o_ref, kp_ref, vp_ref = refs
    g = pl.program_id(1)
    kw = window + Q_BLOCK

    @pl.when(pl.program_id(2) == 0)
    def _():
        dup = _dup_matrix(g)
        kp_ref[0:window, :] = jnp.zeros((window, LANES), kp_ref.dtype)
        vp_ref[0:window, :] = jnp.zeros((window, LANES), vp_ref.dtype)
        _fill_rows(kp_ref, window, k_ref, dup)
        _fill_rows(vp_ref, window, v_ref, dup)

    for u in range(nq):
        rs = slice(u * Q_BLOCK, (u + 1) * Q_BLOCK)
        start = pl.multiple_of((pl.program_id(2) * nq + u) * Q_BLOCK, Q_BLOCK)
        ks = kp_ref[pl.ds(start, kw), :]
        vs = vp_ref[pl.ds(start, kw), :]
        s = lax.dot_general(_stack_heads_native(q_ref[rs, :]), ks, _TRANS_B, preferred_element_type=F32)
        s = s + bm_ref[...]
        kpos = start - window + lax.broadcasted_iota(jnp.int32, s.shape, 1)
        s = jnp.where(kpos >= 0, s, NEG_INF)
        if sinks:
            s = jnp.concatenate([s, sk_ref[...]], axis=1)
        m = jnp.max(s, axis=1, keepdims=True)
        p = jnp.exp2(s - m)
        l = jnp.sum(p, axis=1, keepdims=True)
        o = jnp.dot(p[:, :kw].astype(CD), vs, preferred_element_type=F32) * (1.0 / l)
        o_ref[rs, :] = _unstack_heads(o)


def _banded_attn(main3, q_col, k_col, v_col, biasmask, sink_rows, window, nq):
    b, t, _ = main3.shape
    kw = window + Q_BLOCK
    tq = nq * Q_BLOCK
    assert t % tq == 0
    sinks = sink_rows is not None
    in_specs = [pl.BlockSpec((None, tq, 2 * LANES), lambda i, g, n: (i, n, q_col + g)),
                pl.BlockSpec((None, t, LANES), lambda i, g, n: (i, 0, k_col)),
                pl.BlockSpec((None, t, LANES), lambda i, g, n: (i, 0, v_col)),
                pl.BlockSpec((None, NSA_REP * Q_BLOCK, kw), lambda i, g, n: (g, 0, 0))]
    args = [main3, main3, main3, biasmask]
    if sinks:
        in_specs.append(pl.BlockSpec((None, NSA_REP * Q_BLOCK, LANES), lambda i, g, n: (g, 0, 0)))
        args.append(sink_rows)
    return pl.pallas_call(
        functools.partial(_banded_kernel, window=window, sinks=sinks, nq=nq),
        out_shape=jax.ShapeDtypeStruct((b, t, GROUP_WIDTH), F32),
        grid=(b, NSA_KV_HEADS, t // tq),
        in_specs=in_specs,
        out_specs=pl.BlockSpec((None, tq, 2 * LANES), lambda i, g, n: (i, n, g)),
        scratch_shapes=[pltpu.VMEM((window + t, LANES), CD), pltpu.VMEM((window + t, LANES), CD)],
        compiler_params=_cparams("arbitrary", "arbitrary", "arbitrary"),
        name="swa_attn" if sinks else "nsa_win_attn",
    )(*args)


def _sel_kernel(q_ref, k_ref, v_ref, mf_ref, fq_ref, nb_ref, o_ref,
                ka_ref, vd_ref, qa_ref, s_ref, sn_ref, mx_ref, acc_ref):
    g = pl.program_id(1)
    n = pl.program_id(2)
    t = k_ref.shape[0]

    @pl.when(n == 0)
    def _():
        i = lax.broadcasted_iota(jnp.int32, (LANES, LANES), 0)
        j = lax.broadcasted_iota(jnp.int32, (LANES, LANES), 1)
        pick = jnp.where((j < HEAD_DIM) & (i == g * HEAD_DIM + j), 1.0, 0.0).astype(CD)
        chunk = 1024

        def body(c, _):
            off = pl.multiple_of(c * chunk, chunk)
            kk = jnp.dot(k_ref[pl.ds(off, chunk), :], pick, preferred_element_type=F32)
            ln = lax.broadcasted_iota(jnp.int32, (chunk, LANES), 1)
            kk = jnp.where((ln == HEAD_DIM) | (ln == HEAD_DIM + 1), 1.0, kk)
            ka_ref[pl.ds(off, chunk), 0:LANES] = kk.astype(ka_ref.dtype)
            key = off + lax.broadcasted_iota(jnp.int32, (chunk, LANES), 0)
            ka_ref[pl.ds(off, chunk), LANES:2 * LANES] = jnp.where(
                (key >> 6) == ln, 1.0, 0.0).astype(ka_ref.dtype)
            vd_ref[pl.ds(off, chunk), LANES:2 * LANES] = jnp.ones((chunk, LANES), vd_ref.dtype)
            return 0

        lax.fori_loop(0, t // chunk, body, 0)
        _fill_rows(vd_ref, 0, v_ref, _dup_matrix(g))

    q = q_ref[...]
    ii = lax.broadcasted_iota(jnp.int32, (LANES, LANES), 0)
    jj = lax.broadcasted_iota(jnp.int32, (LANES, LANES), 1)
    shift = jnp.where(ii == jj + HEAD_DIM, 1.0, 0.0).astype(CD)
    lom = _lane_mask(Q_BLOCK, 0, HEAD_DIM, q.dtype)
    c0, c1 = q[:, :LANES], q[:, LANES:]
    q0 = jnp.concatenate([
        c0 * lom, jnp.dot(c0, shift, preferred_element_type=F32).astype(q.dtype),
        c1 * lom, jnp.dot(c1, shift, preferred_element_type=F32).astype(q.dtype)], axis=0)
    q1 = q0 + fq_ref[...]
    mf = mf_ref[...]
    qa_ref[...] = jnp.concatenate([q1, jnp.concatenate([mf, mf, mf, mf], axis=0)], axis=1)

    rows = NSA_REP * Q_BLOCK
    tkf = s_ref.shape[2]
    tkn = sn_ref.shape[1]
    near_off = pl.multiple_of(jnp.maximum(n - 1, 0) * Q_BLOCK, Q_BLOCK)
    n_full = near_off // tkf
    rem = near_off - n_full * tkf

    half = tkf // 2

    def fold_max(mx, s):
        for i in range(s.shape[1] // LANES):
            mx = jnp.maximum(mx, s[:, i * LANES:(i + 1) * LANES])
        return mx

    def score_tiles(j0, count):
        mx = mx_ref[...]
        for gi in range(count):
            off = pl.multiple_of((j0 + gi) * tkf, tkf)
            for hf in range(2):
                s = lax.dot_general(qa_ref[...], ka_ref[pl.ds(off + hf * half, half), :], _TRANS_B,
                                    preferred_element_type=F32)
                col = hf * half + lax.broadcasted_iota(jnp.int32, (rows, half), 1)
                s = jnp.where(col < near_off - off, s, NEG_INF)
                s_ref[j0 + gi, :, hf * half:(hf + 1) * half] = s
                mx = fold_max(mx, s)
        mx_ref[...] = mx

    def value_tiles(j0, count):
        off = pl.multiple_of(j0 * tkf, tkf)
        p = jnp.concatenate([probs(s_ref[j0 + gi]) for gi in range(count)], axis=1)
        acc_ref[...] += jnp.dot(p, vd_ref[pl.ds(off, count * tkf), :], preferred_element_type=F32)

    def probs(s):
        mb = mx_ref[...]
        return jnp.concatenate([jnp.exp2(s[:, i * LANES:(i + 1) * LANES] - mb)
                                for i in range(s.shape[1] // LANES)], axis=1).astype(CD)

    mx_ref[...] = jnp.full((rows, LANES), M_INIT, F32)
    n_far = n_full + jnp.where(rem > 0, 1, 0)
    _sweep_tiles(score_tiles, n_far)
    sn = lax.dot_general(qa_ref[...], ka_ref[pl.ds(near_off, tkn), :], _TRANS_B, preferred_element_type=F32)
    sn = sn + nb_ref[jnp.minimum(n, 1)]
    sn_ref[...] = sn
    m = jnp.max(fold_max(mx_ref[...], sn), axis=1, keepdims=True)
    mx_ref[...] = jnp.broadcast_to(m, (rows, LANES))

    acc_ref[...] = jnp.dot(probs(sn_ref[...]), vd_ref[pl.ds(near_off, tkn), :], preferred_element_type=F32)
    _sweep_tiles(value_tiles, n_far)
    acc = acc_ref[...]
    o_ref[...] = _unstack_heads(acc[:, :LANES] * (1.0 / acc[:, LANES:LANES + 1]))


def _sel_attn(main3, q_col, k_col, v_col, maskfeat, farq, nearbias, tkf):
    b, t, _ = main3.shape
    assert t % tkf == 0 and tkf % Q_BLOCK == 0
    rows = NSA_REP * Q_BLOCK
    blk = lambda w: pl.BlockSpec((None, Q_BLOCK, w), lambda i, g, n: (i, n, g))
    return pl.pallas_call(
        _sel_kernel,
        out_shape=jax.ShapeDtypeStruct((b, t, GROUP_WIDTH), F32),
        grid=(b, NSA_KV_HEADS, t // Q_BLOCK),
        in_specs=[pl.BlockSpec((None, Q_BLOCK, 2 * LANES), lambda i, g, n: (i, n, q_col + g)),
                  pl.BlockSpec((None, t, LANES), lambda i, g, n: (i, 0, k_col)),
                  pl.BlockSpec((None, t, LANES), lambda i, g, n: (i, 0, v_col)),
                  blk(LANES),
                  pl.BlockSpec((None, rows, LANES), lambda i, g, n: (g, 0, 0)),
                  pl.BlockSpec((None, 2, rows, 2 * Q_BLOCK), lambda i, g, n: (g, 0, 0, 0))],
        out_specs=blk(2 * LANES),
        scratch_shapes=[pltpu.VMEM((t, 2 * LANES), CD), pltpu.VMEM((t, 2 * LANES), CD),
                        pltpu.VMEM((rows, 2 * LANES), CD), pltpu.VMEM((t // tkf, rows, tkf), F32),
                        pltpu.VMEM((rows, 2 * Q_BLOCK), F32),
                        pltpu.VMEM((rows, LANES), F32), pltpu.VMEM((rows, 2 * LANES), F32)],
        compiler_params=_cparams("arbitrary", "arbitrary", "arbitrary"),
        name="nsa_sel_attn",
    )(main3, main3, main3, maskfeat, farq, nearbias)


def _outproj_kernel(mla_ref, fox_ref, cmp_ref, sel_ref, win_ref, swa_ref, gt_ref, ex_ref, gn_ref, w_ref, h_ref,
                    o_ref, u_ref):
    @pl.when(pl.program_id(1) == 0)
    def _():
        gate = jnp.zeros((gt_ref.shape[0], 3 * GROUP_WIDTH), F32)
        for part in _split_parts(jax.nn.sigmoid(gt_ref[...]), 2):
            gate = gate + jnp.dot(part, ex_ref[...], preferred_element_type=F32)
        nsa = (gate[:, :GROUP_WIDTH] * cmp_ref[...] + gate[:, GROUP_WIDTH:2 * GROUP_WIDTH] * sel_ref[...]
               + gate[:, 2 * GROUP_WIDTH:] * win_ref[...])
        for k, x in enumerate((mla_ref[...], fox_ref[...], nsa, swa_ref[...])):
            sl = slice(k * GROUP_WIDTH, (k + 1) * GROUP_WIDTH)
            u_ref[:, sl] = _rms(x, gn_ref[:, sl]).astype(u_ref.dtype)

    o_ref[...] = h_ref[...] + jnp.dot(u_ref[...], w_ref[...], preferred_element_type=F32)


def _gate_expansion():
    ex = np.zeros((LANES, 3 * GROUP_WIDTH), np.float32)
    for head in range(GROUP_HEADS):
        for branch in range(3):
            lo = branch * GROUP_WIDTH + head * HEAD_DIM
            ex[8 + 3 * head + branch, lo:lo + HEAD_DIM] = 1.0
    return jnp.asarray(ex, CD)


def _outproj(o_mla, o_fox, o_cmp, o_sel, o_win, o_swa, small, gn, w, h, tm, tn):
    n, d = h.shape
    mix = N_GROUPS * GROUP_WIDTH
    part = pl.BlockSpec((tm, GROUP_WIDTH), lambda i, j: (i, 0))
    ex = _gate_expansion()
    return pl.pallas_call(
        _outproj_kernel,
        out_shape=jax.ShapeDtypeStruct((n, d), F32),
        grid=(n // tm, d // tn),
        in_specs=[part, part, part, part, part, part,
                  pl.BlockSpec((tm, LANES), lambda i, j: (i, 4)),
                  pl.BlockSpec(ex.shape, lambda i, j: (0, 0)),
                  pl.BlockSpec((1, mix), lambda i, j: (0, 0)),
                  pl.BlockSpec((mix, tn), lambda i, j: (0, j)),
                  pl.BlockSpec((tm, tn), lambda i, j: (i, j))],
        out_specs=pl.BlockSpec((tm, tn), lambda i, j: (i, j)),
        scratch_shapes=[pltpu.VMEM((tm, mix), CD)],
        compiler_params=_cparams("parallel", "arbitrary"),
        name="outproj",
    )(o_mla, o_fox, o_cmp, o_sel, o_win, o_swa, small, ex, gn.reshape(1, mix), w, h)


def _mlp_kernel(*refs, final):
    if final:
        h_ref, g_ref, wu_ref, wd_ref, gf_ref, o_ref, u_ref = refs
    else:
        h_ref, g_ref, wu_ref, wd_ref, o_ref, u_ref = refs
    c = pl.program_id(1)

    @pl.when(c == 0)
    def _():
        x = h_ref[...]
        u_ref[...] = _rms(x, g_ref[...]).astype(u_ref.dtype)
        o_ref[...] = x

    m = jnp.dot(u_ref[...], wu_ref[...], preferred_element_type=F32)
    a = jnp.square(jnp.maximum(m, 0.0)).astype(CD)
    o_ref[...] += jnp.dot(a, wd_ref[...], preferred_element_type=F32)

    if final:
        @pl.when(c == pl.num_programs(1) - 1)
        def _():
            o_ref[...] = _rms(o_ref[...], gf_ref[...])


def _mlp(h, g, wu, wd, gf, tm, tf):
    n, d = h.shape
    dff = wu.shape[1]
    final = gf is not None
    in_specs = [pl.BlockSpec((tm, d), lambda i, c: (i, 0)),
                pl.BlockSpec((1, d), lambda i, c: (0, 0)),
                pl.BlockSpec((d, tf), lambda i, c: (0, c)),
                pl.BlockSpec((tf, d), lambda i, c: (c, 0))]
    args = [h, g.reshape(1, d), wu, wd]
    if final:
        in_specs.append(pl.BlockSpec((1, d), lambda i, c: (0, 0)))
        args.append(gf.reshape(1, d))
    return pl.pallas_call(
        functools.partial(_mlp_kernel, final=final),
        out_shape=jax.ShapeDtypeStruct((n, d), F32),
        grid=(n // tm, dff // tf),
        in_specs=in_specs,
        out_specs=pl.BlockSpec((tm, d), lambda i, c: (i, 0)),
        scratch_shapes=[pltpu.VMEM((tm, d), CD)],
        compiler_params=_cparams("parallel", "arbitrary"),
        name="mlp",
    )(*args)


def _t5_bucket_np(dist):
    max_exact = REL_BUCKETS // 2
    d = np.maximum(dist, 0)
    ratio = np.log(np.maximum(d, 1).astype(np.float32) / np.float32(max_exact)) / np.float32(
        math.log(REL_MAX_DIST / max_exact))
    large = max_exact + (ratio * np.float32(REL_BUCKETS - max_exact)).astype(np.int32)
    large = np.minimum(large, REL_BUCKETS - 1)
    return np.where(d < max_exact, d, large)


def _bias_tables(rel_bias):
    tbl = (rel_bias.astype(F32) * LOG2E).T.reshape(2, NSA_KV_HEADS, NSA_REP, REL_BUCKETS)
    tbl_nsa, tbl_swa = tbl[0], tbl[1]

    def lookup(tb, bucket):
        onehot = (jnp.asarray(bucket, jnp.int32)[None] == jnp.arange(REL_BUCKETS)[:, None, None]).astype(F32)
        return jnp.einsum('grb,bij->grij', tb, onehot, precision=lax.Precision.HIGHEST)
    i = np.arange(Q_BLOCK)[:, None]
    far_bucket = REL_BUCKETS - 1

    def banded(tb, window):
        jk = np.arange(window + Q_BLOCK)[None, :]
        dist = i + window - jk
        ok = (dist >= 0) & (dist < window)
        vals = lookup(tb, _t5_bucket_np(dist))
        vals = jnp.where(jnp.asarray(ok)[None, None], vals, NEG_INF)
        return vals.reshape(NSA_KV_HEADS, NSA_REP * Q_BLOCK, window + Q_BLOCK)

    bm_win = banded(tbl_nsa, NSA_WINDOW)
    bm_swa = banded(tbl_swa, SWA_WINDOW)

    far = tbl_nsa[:, :, far_bucket]
    jk = np.arange(2 * Q_BLOCK)[None, :]
    tiles = []
    for first_key_back in (0, Q_BLOCK):
        dist = i + first_key_back - jk
        delta = lookup(tbl_nsa, _t5_bucket_np(dist)) - far[:, :, None, None]
        tiles.append(jnp.where(jnp.asarray(dist >= 0)[None, None], delta, NEG_INF))
    near = jnp.stack(tiles, axis=1).reshape(NSA_KV_HEADS, 2, NSA_REP * Q_BLOCK, 2 * Q_BLOCK)
    far_rows = jnp.broadcast_to(far[:, :, None], (NSA_KV_HEADS, NSA_REP, Q_BLOCK)).reshape(NSA_KV_HEADS, -1)
    hi, lo_ = _split_parts(far_rows, 2)
    zeros = lambda k: jnp.zeros((NSA_KV_HEADS, NSA_REP * Q_BLOCK, k), CD)
    farq = jnp.concatenate([zeros(HEAD_DIM), hi[..., None], lo_[..., None], zeros(LANES - HEAD_DIM - 2)], axis=-1)

    u = np.arange(16)[None, :]
    dist_c = i - 16 * u + 113
    band = lookup(tbl_nsa, _t5_bucket_np(dist_c))
    band = jnp.where(jnp.asarray(dist_c >= 0)[None, None], band, MASK_NEG)
    band = band.reshape(NSA_KV_HEADS, NSA_REP * Q_BLOCK, 16)
    bh, bl = _split_parts(band, 2)
    neg = jnp.full((NSA_KV_HEADS, NSA_REP * Q_BLOCK, 1), MASK_NEG, CD)
    pq = jnp.concatenate([bh, bl, hi[..., None], lo_[..., None], neg, zeros(LANES - 35)], axis=-1)
    return bm_win, bm_swa, near, farq, pq


def _overlap_matrix(nc_pad, ns):
    ci = np.arange(nc_pad)[:, None]
    sj = np.arange(LANES)[None, :]
    ov = ((ci * NSA_CMP_STRIDE + NSA_CMP_LEN - 1 >= sj * NSA_SEL_LEN)
          & (ci * NSA_CMP_STRIDE <= sj * NSA_SEL_LEN + NSA_SEL_LEN - 1) & (sj < ns))
    return jnp.asarray(ov.astype(np.float32), CD)


def _layout_w_in(w):
    w = w.astype(F32)
    sc = HEAD_DIM ** -0.5 * LOG2E
    z = lambda k: jnp.zeros((w.shape[0], k), F32)
    main = jnp.concatenate([w[:, 544:1056] * sc, w[:, 1056:2080], w[:, 2088:2600] * sc, w[:, 3392:3904] * sc,
                            w[:, 2600:3368], w[:, 3904:4160]], axis=1)
    kr = w[:, 512:544]
    blk_a = jnp.concatenate([w[:, 2080:2088], w[:, 3368:3392], z(32), kr, z(32)], axis=1)
    blk_b = jnp.concatenate([z(64), -kr[:, 16:], kr[:, :16], z(32)], axis=1)
    small = jnp.concatenate([w[:, 0:512], blk_a, blk_b], axis=1)
    return main.astype(CD), small.astype(CD)


_FOX_Q, _FOX_K512, _FOX_V = 0, 1, 8
_NSA_Q256, _SWA_Q256 = 6, 8
_KVC_COL = 2560
_KSEL, _VSEL, _KWIN, _VWIN, _KSWA, _VSWA = 22, 23, 24, 25, 26, 27


def _layout_mla(w_uq, w_ukv):
    w3 = w_uq.astype(F32).reshape(MLA_Q_RANK, GROUP_HEADS, MLA_NOPE + MLA_ROPE)
    nope, rp = w3[:, :, :MLA_NOPE], w3[:, :, MLA_NOPE:]
    half = MLA_ROPE // 2
    sw = jnp.concatenate([-rp[:, :, half:], rp[:, :, :half]], axis=-1)
    z = lambda k: jnp.zeros((MLA_Q_RANK, GROUP_HEADS, k), F32)
    plain = jnp.concatenate([nope, rp, z(32)], axis=-1).reshape(MLA_Q_RANK, -1)
    swapped = jnp.concatenate([z(64), sw, z(32)], axis=-1).reshape(MLA_Q_RANK, -1)
    wq = jnp.concatenate([plain, swapped], axis=1)
    k3 = w_ukv.astype(F32).reshape(MLA_KV_RANK, GROUP_HEADS, 2 * HEAD_DIM)
    wk = jnp.concatenate([k3[:, :, :MLA_NOPE], jnp.zeros((MLA_KV_RANK, GROUP_HEADS, HEAD_DIM), F32)],
                         axis=-1).reshape(MLA_KV_RANK, -1)
    wv = k3[:, :, MLA_NOPE:].reshape(MLA_KV_RANK, -1)
    return wq.astype(CD), wk.astype(CD), wv.astype(CD)


def _rope_tables(t):
    inv = ROPE_THETA ** (-jnp.arange(0, MLA_ROPE, 2, dtype=F32) / MLA_ROPE)
    ang = jnp.arange(t).astype(F32)[:, None] * inv[None, :]
    cc = jnp.concatenate([jnp.cos(ang)] * 2, axis=1)
    ss = jnp.concatenate([jnp.sin(ang)] * 2, axis=1)
    one, z64, z32 = jnp.ones((t, 64), F32), jnp.zeros((t, 64), F32), jnp.zeros((t, 32), F32)
    return (jnp.concatenate([one, cc, z32], axis=1), jnp.concatenate([z64, ss, z32], axis=1),
            jnp.concatenate([z64, cc, z32], axis=1))


def kernel(x, norm_attn, w_in, mla_q_norm, mla_w_uq, mla_kv_norm, mla_w_ukv, fox_b_f, nsa_cmp_pos,
           nsa_cmp_w1, nsa_cmp_w2, swa_sinks, group_norm, w_out, norm_mlp, w_up, w_down, rel_bias,
           final_norm):
    b, t, d = x.shape
    n = b * t
    depth = w_in.shape[0]
    assert t % 1024 == 0 and d == N_GROUPS * GROUP_WIDTH
    tm = 1024 if n % 1024 == 0 else 512
    tq = 512
    nq = 4
    nr = t // NSA_CMP_STRIDE

    cosq, sinq, cosk = _rope_tables(t)
    bm_win, bm_swa, nearbias, farq, pq = _bias_tables(rel_bias)
    overlap = _overlap_matrix(nr, t // NSA_SEL_LEN)

    h = x.reshape(n, d).astype(F32)
    for l in range(depth):
        w_main, w_small = _layout_w_in(w_in[l])
        main = _norm_matmul(h, norm_attn[l], w_main, CD, tm, 896)
        small = _norm_matmul(h, norm_attn[l], w_small, F32, tm, 768)
        main3 = main.reshape(b, t, -1)
        small3 = small.reshape(b, t, -1)

        wq, wk, wv = _layout_mla(mla_w_uq[l], mla_w_ukv[l])
        q_m, k_m, v_m = _mla_prep(small, mla_q_norm[l], mla_kv_norm[l], wq, wk, wv, cosq, sinq, cosk, t, 512)
        o_mla = _dense_attn(q_m.reshape(b, t, -1), k_m.reshape(b, t, -1), v_m.reshape(b, t, -1), None,
                            fox=False, q_col=0, v_col=0, tq=tq)

        k_aug, fq = _fox_prep(small3, fox_b_f[l], main3, _FOX_K512, 512)
        o_fox = _dense_attn(main3, k_aug, main3, fq, fox=True, q_col=_FOX_Q, v_col=_FOX_V, tq=tq)

        kv16 = main3[:, :, _KVC_COL:_KVC_COL + 256].reshape(b, nr, NSA_CMP_STRIDE * 256)
        cmp = _nsa_compress(kv16, nsa_cmp_pos[l], nsa_cmp_w1[l], nsa_cmp_w2[l])
        o_cmp, maskfeat = _cmp_select(main3, _NSA_Q256, cmp, pq, overlap, nq)
        o_win = _banded_attn(main3, _NSA_Q256, _KWIN, _VWIN, bm_win, None, NSA_WINDOW, 2 * nq)
        o_sel = _sel_attn(main3, _NSA_Q256, _KSEL, _VSEL, maskfeat, farq, nearbias, tq)

        sink = (swa_sinks[l].astype(F32) * LOG2E).reshape(NSA_KV_HEADS, NSA_REP, 1, 1)
        sink_rows = jnp.concatenate(
            [jnp.broadcast_to(sink, (NSA_KV_HEADS, NSA_REP, Q_BLOCK, 1)),
             jnp.full((NSA_KV_HEADS, NSA_REP, Q_BLOCK, LANES - 1), NEG_INF, F32)],
            axis=-1).reshape(NSA_KV_HEADS, NSA_REP * Q_BLOCK, LANES)
        o_swa = _banded_attn(main3, _SWA_Q256, _KSWA, _VSWA, bm_swa, sink_rows, SWA_WINDOW, 2 * nq)

        parts = [o.reshape(n, GROUP_WIDTH) for o in (o_mla, o_fox, o_cmp, o_sel, o_win, o_swa)]
        h = _outproj(*parts, small, group_norm[l], w_out[l].astype(CD), h, 512, d)
        gf = final_norm if l == depth - 1 else None
        h = _mlp(h, norm_mlp[l], w_up[l].astype(CD), w_down[l].astype(CD), gf, 512, 1024)
    return h.reshape(b, t, d).astype(x.dtype)
```

```python
import functools
import math

import numpy as np
import jax
import jax.numpy as jnp
from jax import lax
from jax.experimental import pallas as pl
from jax.experimental.pallas import tpu as pltpu

HEAD_DIM = 64
GROUP_HEADS = 8
GROUP_WIDTH = GROUP_HEADS * HEAD_DIM
N_GROUPS = 4
Q_BLOCK = 128
EPS = 1e-6
NEG_INF = -1e30
BIG = 1e9

MLA_Q_RANK = 384
MLA_KV_RANK = 128
MLA_NOPE = 64
MLA_ROPE = 32
ROPE_THETA = 10000.0

NSA_KV_HEADS = 2
NSA_REP = GROUP_HEADS // NSA_KV_HEADS
NSA_CMP_LEN = 32
NSA_CMP_STRIDE = 16
NSA_SEL_LEN = 64
NSA_TOP_N = 8
NSA_WINDOW = 256
SWA_WINDOW = 128

REL_BUCKETS = 32
REL_MAX_DIST = 128

V7X_VMEM_BYTES = 64 * 2**20
VMEM_LIMIT = (V7X_VMEM_BYTES * 7) // 8
LANES = 128
CD = jnp.bfloat16
F32 = jnp.float32
MASK_NEG = -(2.0 ** 80)
M_INIT = -(2.0 ** 100)
TILE_GROUP = 8
LOG2E = math.log2(math.e)
MLA_SCORE_SCALE = (MLA_NOPE + MLA_ROPE) ** -0.5 * LOG2E
_TRANS_B = (((1,), (1,)), ((), ()))


def _cparams(*sem):
    return pltpu.CompilerParams(dimension_semantics=sem, vmem_limit_bytes=VMEM_LIMIT)


def _split_parts(x, n):
    parts, r = [], x
    for _ in range(n):
        p = r.astype(CD)
        parts.append(p)
        r = r - p.astype(F32)
    return parts


def _sweep_tiles(fn, n_tiles):
    groups = n_tiles // TILE_GROUP

    def grouped(i, _):
        fn(i * TILE_GROUP, TILE_GROUP)
        return 0

    lax.fori_loop(0, groups, grouped, 0)
    base = groups * TILE_GROUP
    rem = n_tiles - base
    piece = TILE_GROUP // 2
    while piece >= 1:
        @pl.when((rem & piece) != 0)
        def _(piece=piece):
            fn(base + (rem & ~(2 * piece - 1)), piece)
        piece //= 2


def _rms(x, g):
    return x * lax.rsqrt(jnp.mean(x * x, axis=-1, keepdims=True) + EPS) * g


def _norm_matmul_kernel(x_ref, g_ref, w_ref, o_ref, u_ref):
    @pl.when(pl.program_id(1) == 0)
    def _():
        u_ref[...] = _rms(x_ref[...], g_ref[...]).astype(u_ref.dtype)

    o_ref[...] = jnp.dot(u_ref[...], w_ref[...], preferred_element_type=F32).astype(o_ref.dtype)


def _norm_matmul(x, g, w, out_dtype, tm, tn):
    n, d = x.shape
    nc = w.shape[1]
    return pl.pallas_call(
        _norm_matmul_kernel,
        out_shape=jax.ShapeDtypeStruct((n, nc), out_dtype),
        grid=(n // tm, nc // tn),
        in_specs=[
            pl.BlockSpec((tm, d), lambda i, j: (i, 0)),
            pl.BlockSpec((1, d), lambda i, j: (0, 0)),
            pl.BlockSpec((d, tn), lambda i, j: (0, j)),
        ],
        out_specs=pl.BlockSpec((tm, tn), lambda i, j: (i, j)),
        scratch_shapes=[pltpu.VMEM((tm, d), CD)],
        compiler_params=_cparams("parallel", "arbitrary"),
        name="norm_matmul",
    )(x, g.reshape(1, d), w)


def _mla_prep_kernel(sm_ref, qn_ref, kvn_ref, wq_ref, wk_ref, wv_ref, cq_ref, sq_ref, ck_ref,
                     q_ref, k_ref, v_ref):
    sm = sm_ref[...]
    nq = _rms(sm[:, :MLA_Q_RANK], qn_ref[...]).astype(CD)
    nkv = _rms(sm[:, MLA_Q_RANK:MLA_Q_RANK + MLA_KV_RANK], kvn_ref[...]).astype(CD)
    blk_a = sm[:, 512:640]
    blk_b = sm[:, 640:768]
    cosq, sinq, cosk = cq_ref[...], sq_ref[...], ck_ref[...]
    hw = GROUP_HEADS * LANES
    qq = jnp.dot(nq, wq_ref[...], preferred_element_type=F32)
    kk = jnp.dot(nkv, wk_ref[...], preferred_element_type=F32)
    kpe = blk_a * cosk + blk_b * sinq
    for h in range(GROUP_HEADS):
        sl = slice(h * LANES, (h + 1) * LANES)
        qh = qq[:, sl] * cosq + qq[:, hw + h * LANES:hw + (h + 1) * LANES] * sinq
        q_ref[:, sl] = (qh * MLA_SCORE_SCALE).astype(q_ref.dtype)
        k_ref[:, sl] = (kk[:, sl] + kpe).astype(k_ref.dtype)
    v_ref[...] = jnp.dot(nkv, wv_ref[...], preferred_element_type=F32).astype(v_ref.dtype)


def _mla_prep(small, qn, kvn, wq, wk, wv, cosq, sinq, cosk, t, tm):
    n = small.shape[0]
    tb = t // tm
    hw = GROUP_HEADS * LANES
    full = lambda a: pl.BlockSpec(a.shape, lambda i: (0,) * a.ndim)
    tab = pl.BlockSpec((tm, LANES), lambda i: (i % tb, 0))
    qn = qn.reshape(1, -1)
    kvn = kvn.reshape(1, -1)
    return pl.pallas_call(
        _mla_prep_kernel,
        out_shape=(jax.ShapeDtypeStruct((n, hw), CD), jax.ShapeDtypeStruct((n, hw), CD),
                   jax.ShapeDtypeStruct((n, GROUP_WIDTH), CD)),
        grid=(n // tm,),
        in_specs=[pl.BlockSpec((tm, small.shape[1]), lambda i: (i, 0)), full(qn), full(kvn),
                  full(wq), full(wk), full(wv), tab, tab, tab],
        out_specs=(pl.BlockSpec((tm, hw), lambda i: (i, 0)), pl.BlockSpec((tm, hw), lambda i: (i, 0)),
                   pl.BlockSpec((tm, GROUP_WIDTH), lambda i: (i, 0))),
        compiler_params=_cparams("parallel"),
        name="mla_prep",
    )(small, qn, kvn, wq, wk, wv, cosq, sinq, cosk)


def _fox_prep_kernel(fl_ref, bf_ref, k_ref, efq_ref, ekf_ref, kaug_ref, fq_ref, carry_ref, *, tc):
    @pl.when(pl.program_id(1) == 0)
    def _():
        carry_ref[...] = jnp.zeros_like(carry_ref)

    x = fl_ref[...] + bf_ref[...]
    logf = -(jnp.maximum(-x, 0.0) + jnp.log1p(jnp.exp(-jnp.abs(x))))
    lane = lax.broadcasted_iota(jnp.int32, logf.shape, 1)
    logf = jnp.where(lane < GROUP_HEADS, logf, 0.0)
    row = lax.broadcasted_iota(jnp.int32, (tc, tc), 0)
    col = lax.broadcasted_iota(jnp.int32, (tc, tc), 1)
    tri = jnp.where(row >= col, 1.0, 0.0).astype(CD)
    cs = jnp.zeros(logf.shape, F32)
    for part in _split_parts(logf, 3):
        cs = cs + jnp.dot(tri, part, preferred_element_type=F32)
    fc = cs + carry_ref[...]
    carry_ref[...] = fc[tc - 1:tc, :]
    parts = _split_parts(fc * LOG2E, 3)
    fq = jnp.zeros(fq_ref.shape, F32)
    kf = jnp.zeros(fq_ref.shape, F32)
    for i, part in enumerate(parts):
        fq = fq + jnp.dot(part, efq_ref[...], preferred_element_type=F32)
        kf = kf + jnp.dot(part, ekf_ref[i], preferred_element_type=F32)
    fq_ref[...] = fq
    k = k_ref[...]
    for p in range(GROUP_HEADS // 2):
        kaug_ref[:, 2 * p * LANES:(2 * p + 1) * LANES] = k[:, p * LANES:(p + 1) * LANES]
        kaug_ref[:, (2 * p + 1) * LANES:(2 * p + 2) * LANES] = kf[:, p * LANES:(p + 1) * LANES].astype(kaug_ref.dtype)


def _fox_prep(small3, b_f, main3, k_col, tc):
    b, t, _ = small3.shape
    pairs = GROUP_HEADS // 2
    efq = np.zeros((LANES, pairs * LANES), np.float32)
    ekf = np.zeros((3, LANES, pairs * LANES), np.float32)
    for h in range(GROUP_HEADS):
        p, a = divmod(h, 2)
        efq[h, p * LANES + a] = 1.0
        for i in range(3):
            ekf[i, h, p * LANES + 3 * a + i] = -1.0
    bf = jnp.zeros((1, LANES), F32).at[0, :GROUP_HEADS].set(b_f.astype(F32))
    return pl.pallas_call(
        functools.partial(_fox_prep_kernel, tc=tc),
        out_shape=(jax.ShapeDtypeStruct((b, t, 2 * GROUP_WIDTH), CD),
                   jax.ShapeDtypeStruct((b, t, GROUP_WIDTH), F32)),
        grid=(b, t // tc),
        in_specs=[pl.BlockSpec((None, tc, LANES), lambda i, j: (i, j, 4)),
                  pl.BlockSpec((1, LANES), lambda i, j: (0, 0)),
                  pl.BlockSpec((None, tc, GROUP_WIDTH), lambda i, j: (i, j, k_col)),
                  pl.BlockSpec(efq.shape, lambda i, j: (0, 0)),
                  pl.BlockSpec(ekf.shape, lambda i, j: (0, 0, 0))],
        out_specs=(pl.BlockSpec((None, tc, 2 * GROUP_WIDTH), lambda i, j: (i, j, 0)),
                   pl.BlockSpec((None, tc, GROUP_WIDTH), lambda i, j: (i, j, 0))),
        scratch_shapes=[pltpu.VMEM((1, LANES), F32)],
        compiler_params=_cparams("arbitrary", "arbitrary"),
        name="fox_prep",
    )(small3, bf, main3, jnp.asarray(efq, CD), jnp.asarray(ekf, CD))


def _dense_attn_kernel(*refs, fox, tq):
    if fox:
        q_ref, k_ref, v_ref, fq_ref, o_ref, va_ref, qa_ref, s_ref, mx_ref, acc_ref = refs
    else:
        q_ref, k_ref, v_ref, o_ref, va_ref, qa_ref, s_ref, mx_ref, acc_ref = refs
    qi = pl.program_id(2)
    t = k_ref.shape[0]
    nch = tq // LANES

    @pl.when(qi == 0)
    def _():
        chunk = 1024

        def body(i, _):
            off = pl.multiple_of(i * chunk, chunk)
            va_ref[pl.ds(off, chunk), 0:LANES] = v_ref[pl.ds(off, chunk), :]
            va_ref[pl.ds(off, chunk), LANES:2 * LANES] = jnp.ones((chunk, LANES), va_ref.dtype)
            return 0

        lax.fori_loop(0, t // chunk, body, 0)

    half = tq // 2
    outs = []
    for h in range(2):
        q = q_ref[...]
        if fox:
            qa_ref[...] = jnp.concatenate(
                [q * _lane_mask(tq, h * HEAD_DIM, (h + 1) * HEAD_DIM, q.dtype), _lane_mask(tq, 3 * h, 3 * h + 3, q.dtype)],
                axis=1)
        else:
            qa_ref[...] = q[:, h * LANES:(h + 1) * LANES]

        def score_tiles(j0, count, h=h):
            mx = mx_ref[...]
            for gi in range(count):
                off = pl.multiple_of((j0 + gi) * tq, tq)
                for hf in range(2):
                    rows = pl.ds(off + hf * half, half)
                    k = k_ref[rows, :] if fox else k_ref[rows, pl.ds(h * LANES, LANES)]
                    s = lax.dot_general(qa_ref[...], k, _TRANS_B, preferred_element_type=F32)
                    delta = (hf * half + lax.broadcasted_iota(jnp.int32, (tq, half), 1)
                             - lax.broadcasted_iota(jnp.int32, (tq, half), 0))
                    s = jnp.where(delta <= (qi - j0 - gi) * tq, s, NEG_INF)
                    s_ref[j0 + gi, :, hf * half:(hf + 1) * half] = s
                    for i in range(half // LANES):
                        mx = jnp.maximum(mx, s[:, i * LANES:(i + 1) * LANES])
            mx_ref[...] = mx

        def value_tiles(j0, count):
            mc = mx_ref[...]
            ps = []
            for gi in range(count):
                s = s_ref[j0 + gi]
                ps += [jnp.exp2(s[:, i * LANES:(i + 1) * LANES] - mc) for i in range(nch)]
            off = pl.multiple_of(j0 * tq, tq)
            acc_ref[...] += jnp.dot(jnp.concatenate(ps, axis=1).astype(CD), va_ref[pl.ds(off, count * tq), :],
                                    preferred_element_type=F32)

        mx_ref[...] = jnp.full(mx_ref.shape, M_INIT, F32)
        _sweep_tiles(score_tiles, qi + 1)
        m = jnp.max(mx_ref[...], axis=1, keepdims=True)
        if fox:
            f = fq_ref[:, h:h + 1]
            m = (m + f) - f
        mx_ref[...] = jnp.broadcast_to(m, (tq, LANES))

        acc_ref[...] = jnp.zeros(acc_ref.shape, F32)
        _sweep_tiles(value_tiles, qi + 1)
        acc = acc_ref[...]
        outs.append(acc[:, :LANES] * (1.0 / acc[:, LANES:LANES + 1]))
    lane = lax.broadcasted_iota(jnp.int32, (tq, LANES), 1)
    o_ref[...] = jnp.where(lane < HEAD_DIM, outs[0], outs[1]).astype(o_ref.dtype)


def _dense_attn(q3, k3, v3, fq3, *, fox, q_col, v_col, tq):
    b, t, _ = q3.shape
    assert t % tq == 0 and t % 1024 == 0
    pairs = GROUP_HEADS // 2
    qw = LANES if fox else 2 * LANES
    kd = 2 * LANES if fox else LANES
    in_specs = [pl.BlockSpec((None, tq, qw), lambda i, p, j: (i, j, q_col + p)),
                pl.BlockSpec((None, t, 2 * LANES), lambda i, p, j: (i, 0, p)),
                pl.BlockSpec((None, t, LANES), lambda i, p, j: (i, 0, v_col + p))]
    args = [q3, k3, v3]
    if fox:
        in_specs.append(pl.BlockSpec((None, tq, LANES), lambda i, p, j: (i, j, p)))
        args.append(fq3)
    return pl.pallas_call(
        functools.partial(_dense_attn_kernel, fox=fox, tq=tq),
        out_shape=jax.ShapeDtypeStruct((b, t, GROUP_WIDTH), F32),
        grid=(b, pairs, t // tq),
        in_specs=in_specs,
        out_specs=pl.BlockSpec((None, tq, LANES), lambda i, p, j: (i, j, p)),
        scratch_shapes=[pltpu.VMEM((t, 2 * LANES), CD), pltpu.VMEM((tq, kd), CD),
                        pltpu.VMEM((t // tq, tq, tq), F32), pltpu.VMEM((tq, LANES), F32),
                        pltpu.VMEM((tq, 2 * LANES), F32)],
        compiler_params=_cparams("arbitrary", "arbitrary", "arbitrary"),
        name="fox_attn" if fox else "mla_attn",
    )(*args)


def _gelu_tanh(x):
    return 0.5 * x * (1.0 + jnp.tanh(math.sqrt(2.0 / math.pi) * (x + 0.044715 * (x * x * x))))


def _compress_kernel(r_ref, w1a_ref, w1b_ref, pa_ref, pb_ref, w2_ref, o_ref):
    r = r_ref[...]
    nr = r.shape[0]
    ya = jnp.dot(r, w1a_ref[...], preferred_element_type=F32)
    yb = jnp.dot(r, w1b_ref[...], preferred_element_type=F32)
    pc = (jnp.dot(pa_ref[...], w1a_ref[...], preferred_element_type=F32)
          + jnp.dot(pb_ref[...], w1b_ref[...], preferred_element_type=F32))[0:1, :]
    pre = ya + pltpu.roll(yb, nr - 1, 0) + pc
    h1 = _gelu_tanh(pre).astype(CD)
    rowi = lax.broadcasted_iota(jnp.int32, (nr, LANES), 0)
    for g in range(NSA_KV_HEADS):
        y = jnp.dot(h1, w2_ref[g], preferred_element_type=F32)
        o_ref[g] = jnp.where(rowi < nr - 1, y, 0.0).astype(o_ref.dtype)


def _nsa_compress(kv16, cmp_pos, cmp_w1, cmp_w2):
    b, nr, kw = kv16.shape
    half = NSA_CMP_LEN // 2
    eye2 = jnp.eye(2, dtype=F32)
    w1r = cmp_w1.astype(F32).reshape(2, NSA_CMP_LEN, HEAD_DIM, HEAD_DIM)
    expand = lambda w: jnp.einsum('klde,kw,gh->klwgdhe', w, eye2, eye2).reshape(2, kw, LANES).astype(CD)
    w1a, w1b = expand(w1r[:, :half]), expand(w1r[:, half:])
    posr = cmp_pos.astype(F32)
    tile = lambda p: jnp.broadcast_to(p[:, None, :, None, None, :], (2, 8, half, 2, 2, HEAD_DIM)).reshape(2, 8, kw).astype(CD)
    pa, pb = tile(posr[:, :half]), tile(posr[:, half:])
    w2 = cmp_w2.astype(F32)
    w2d = jnp.zeros((2, NSA_KV_HEADS, LANES, LANES), F32)
    for g in range(NSA_KV_HEADS):
        blk = jnp.concatenate([w2, w2], axis=2)
        w2d = w2d.at[:, g, g * HEAD_DIM:(g + 1) * HEAD_DIM, :].set(blk)
    w2d = w2d.astype(CD)
    return pl.pallas_call(
        _compress_kernel,
        out_shape=jax.ShapeDtypeStruct((b, 2, NSA_KV_HEADS, nr, LANES), CD),
        grid=(b, 2),
        in_specs=[pl.BlockSpec((None, nr, kw), lambda i, k: (i, 0, 0)),
                  pl.BlockSpec((None, kw, LANES), lambda i, k: (k, 0, 0)),
                  pl.BlockSpec((None, kw, LANES), lambda i, k: (k, 0, 0)),
                  pl.BlockSpec((None, 8, kw), lambda i, k: (k, 0, 0)),
                  pl.BlockSpec((None, 8, kw), lambda i, k: (k, 0, 0)),
                  pl.BlockSpec((None, NSA_KV_HEADS, LANES, LANES), lambda i, k: (k, 0, 0, 0))],
        out_specs=pl.BlockSpec((None, None, NSA_KV_HEADS, nr, LANES), lambda i, k: (i, k, 0, 0, 0)),
        compiler_params=_cparams("parallel", "arbitrary"),
        name="nsa_compress",
    )(kv16, w1a, w1b, pa, pb, w2d)


def _lane_mask(rows, lo, hi, dtype):
    lane = lax.broadcasted_iota(jnp.int32, (rows, LANES), 1)
    return jnp.where((lane >= lo) & (lane < hi), 1.0, 0.0).astype(dtype)


def _stack_heads_native(q):
    lo = _lane_mask(Q_BLOCK, 0, HEAD_DIM, q.dtype)
    hi = _lane_mask(Q_BLOCK, HEAD_DIM, LANES, q.dtype)
    c0, c1 = q[:, :LANES], q[:, LANES:]
    return jnp.concatenate([c0 * lo, c0 * hi, c1 * lo, c1 * hi], axis=0)


def _unstack_heads(o):
    lane = lax.broadcasted_iota(jnp.int32, (Q_BLOCK, LANES), 1)
    lo = lane < HEAD_DIM
    return jnp.concatenate([jnp.where(lo, o[0:128], o[128:256]), jnp.where(lo, o[256:384], o[384:512])], axis=1)


def _dup_matrix(g):
    i = lax.broadcasted_iota(jnp.int32, (LANES, LANES), 0)
    j = lax.broadcasted_iota(jnp.int32, (LANES, LANES), 1)
    return jnp.where(i == g * HEAD_DIM + (j & (HEAD_DIM - 1)), 1.0, 0.0).astype(CD)


def _fill_rows(dst_ref, row0, src_ref, mat, chunk=1024):
    n = src_ref.shape[0]

    def body(i, _):
        off = pl.multiple_of(i * chunk, chunk)
        dst_ref[pl.ds(row0 + off, chunk), 0:LANES] = jnp.dot(
            src_ref[pl.ds(off, chunk), :], mat, preferred_element_type=F32).astype(dst_ref.dtype)
        return 0

    lax.fori_loop(0, n // chunk, body, 0)


def _cmp_select_kernel(q_ref, kc_ref, vc_ref, pq_ref, ov_ref, o_ref, mf_ref, *, nq):
    nc = kc_ref.shape[0]
    ci = lax.broadcasted_iota(jnp.int32, (nc, LANES), 0)
    f = lax.broadcasted_iota(jnp.int32, (nc, LANES), 1)
    j = lax.broadcasted_iota(jnp.int32, (LANES, Q_BLOCK), 1)
    i = lax.broadcasted_iota(jnp.int32, (LANES, Q_BLOCK), 0)
    bf = i.astype(F32)
    one_if = lambda cond: jnp.where(cond, 1.0, 0.0)
    for u in range(nq):
        n = pl.program_id(2) * nq + u
        rs = slice(u * Q_BLOCK, (u + 1) * Q_BLOCK)
        qaug = jnp.concatenate([_stack_heads_native(q_ref[rs, :]), pq_ref[...]], axis=1)
        uc = ci - 8 * n + 9
        feat = jnp.where(f < 32, one_if(uc == (f & 15)),
                         jnp.where(f < 34, one_if(uc < 0), one_if((f == 34) & (uc > 15)))).astype(CD)
        kaug = jnp.concatenate([kc_ref[...], feat], axis=1)
        s = lax.dot_general(qaug, kaug, _TRANS_B, preferred_element_type=F32)
        m = jnp.max(s, axis=1, keepdims=True)
        p = jnp.exp2(s - m)
        l = jnp.sum(p, axis=1, keepdims=True)
        pc = p * jnp.where(m > 0.5 * MASK_NEG, 1.0 / l, 0.0)
        o = jnp.dot(pc.astype(CD), vc_ref[...], preferred_element_type=F32)
        o_ref[rs, :] = _unstack_heads(o)
        pcs = pc[0:128] + pc[128:256] + pc[256:384] + pc[384:512]
        imp = lax.dot_general(ov_ref[...], pcs.astype(CD), _TRANS_B, preferred_element_type=F32)
        cur = 2 * n + jnp.where(j >= NSA_SEL_LEN, 1, 0)
        causal = i <= cur
        forced = (i == 0) | (i == cur) | (i == cur - 1)
        sel = one_if(forced & causal)
        score = jnp.where(causal & jnp.logical_not(forced), imp, -3.0e38)
        for _ in range(NSA_TOP_N - 3):
            mx = jnp.max(score, axis=0, keepdims=True)
            idx = jnp.min(jnp.where(score == mx, bf, float(LANES)), axis=0, keepdims=True)
            pick = bf == idx
            sel = jnp.where(pick, 1.0, sel)
            score = jnp.where(pick, -3.0e38, score)
        mf_ref[rs, :] = jnp.where(sel > 0.5, 0.0, MASK_NEG).T.astype(mf_ref.dtype)


def _cmp_select(main3, q_col, cmp, pq, overlap, nq):
    b, t, _ = main3.shape
    nc = cmp.shape[3]
    tq = nq * Q_BLOCK
    assert t % tq == 0
    return pl.pallas_call(
        functools.partial(_cmp_select_kernel, nq=nq),
        out_shape=(jax.ShapeDtypeStruct((b, t, GROUP_WIDTH), F32),
                   jax.ShapeDtypeStruct((b, t, NSA_KV_HEADS * LANES), CD)),
        grid=(b, NSA_KV_HEADS, t // tq),
        in_specs=[pl.BlockSpec((None, tq, 2 * LANES), lambda i, g, n: (i, n, q_col + g)),
                  pl.BlockSpec((None, None, None, nc, LANES), lambda i, g, n: (i, 0, g, 0, 0)),
                  pl.BlockSpec((None, None, None, nc, LANES), lambda i, g, n: (i, 1, g, 0, 0)),
                  pl.BlockSpec((None, NSA_REP * Q_BLOCK, LANES), lambda i, g, n: (g, 0, 0)),
                  pl.BlockSpec(overlap.shape, lambda i, g, n: (0, 0))],
        out_specs=(pl.BlockSpec((None, tq, 2 * LANES), lambda i, g, n: (i, n, g)),
                   pl.BlockSpec((None, tq, LANES), lambda i, g, n: (i, n, g))),
        compiler_params=_cparams("parallel", "parallel", "arbitrary"),
        name="nsa_cmp_select",
    )(main3, cmp, cmp, pq, overlap)


def _banded_kernel(*refs, window, sinks, nq):
    if sinks:
        q_ref, k_ref, v_ref, bm_ref, sk_ref, o_ref, kp_ref, vp_ref = refs
    else:
        q_ref, k_ref, v_ref, bm_ref, o_ref, kp_ref, vp_ref = refs
    g = pl.program_id(1)
    kw = window + Q_BLOCK

    @pl.when(pl.program_id(2) == 0)
    def _():
        dup = _dup_matrix(g)
        kp_ref[0:window, :] = jnp.zeros((window, LANES), kp_ref.dtype)
        vp_ref[0:window, :] = jnp.zeros((window, LANES), vp_ref.dtype)
        _fill_rows(kp_ref, window, k_ref, dup)
        _fill_rows(vp_ref, window, v_ref, dup)

    for u in range(nq):
        rs = slice(u * Q_BLOCK, (u + 1) * Q_BLOCK)
        start = pl.multiple_of((pl.program_id(2) * nq + u) * Q_BLOCK, Q_BLOCK)
        ks = kp_ref[pl.ds(start, kw), :]
        vs = vp_ref[pl.ds(start, kw), :]
        s = lax.dot_general(_stack_heads_native(q_ref[rs, :]), ks, _TRANS_B, preferred_element_type=F32)
        s = s + bm_ref[...]
        kpos = start - window + lax.broadcasted_iota(jnp.int32, s.shape, 1)
        s = jnp.where(kpos >= 0, s, NEG_INF)
        if sinks:
            s = jnp.concatenate([s, sk_ref[...]], axis=1)
        m = jnp.max(s, axis=1, keepdims=True)
        p = jnp.exp2(s - m)
        l = jnp.sum(p, axis=1, keepdims=True)
        o = jnp.dot(p[:, :kw].astype(CD), vs, preferred_element_type=F32) * (1.0 / l)
        o_ref[rs, :] = _unstack_heads(o)


def _banded_attn(main3, q_col, k_col, v_col, biasmask, sink_rows, window, nq):
    b, t, _ = main3.shape
    kw = window + Q_BLOCK
    tq = nq * Q_BLOCK
    assert t % tq == 0
    sinks = sink_rows is not None
    in_specs = [pl.BlockSpec((None, tq, 2 * LANES), lambda i, g, n: (i, n, q_col + g)),
                pl.BlockSpec((None, t, LANES), lambda i, g, n: (i, 0, k_col)),
                pl.BlockSpec((None, t, LANES), lambda i, g, n: (i, 0, v_col)),
                pl.BlockSpec((None, NSA_REP * Q_BLOCK, kw), lambda i, g, n: (g, 0, 0))]
    args = [main3, main3, main3, biasmask]
    if sinks:
        in_specs.append(pl.BlockSpec((None, NSA_REP * Q_BLOCK, LANES), lambda i, g, n: (g, 0, 0)))
        args.append(sink_rows)
    return pl.pallas_call(
        functools.partial(_banded_kernel, window=window, sinks=sinks, nq=nq),
        out_shape=jax.ShapeDtypeStruct((b, t, GROUP_WIDTH), F32),
        grid=(b, NSA_KV_HEADS, t // tq),
        in_specs=in_specs,
        out_specs=pl.BlockSpec((None, tq, 2 * LANES), lambda i, g, n: (i, n, g)),
        scratch_shapes=[pltpu.VMEM((window + t, LANES), CD), pltpu.VMEM((window + t, LANES), CD)],
        compiler_params=_cparams("arbitrary", "arbitrary", "arbitrary"),
        name="swa_attn" if sinks else "nsa_win_attn",
    )(*args)


def _sel_kernel(q_ref, k_ref, v_ref, mf_ref, fq_ref, nb_ref, o_ref,
                ka_ref, vd_ref, qa_ref, s_ref, sn_ref, mx_ref, acc_ref):
    g = pl.program_id(1)
    n = pl.program_id(2)
    t = k_ref.shape[0]

    @pl.when(n == 0)
    def _():
        i = lax.broadcasted_iota(jnp.int32, (LANES, LANES), 0)
        j = lax.broadcasted_iota(jnp.int32, (LANES, LANES), 1)
        pick = jnp.where((j < HEAD_DIM) & (i == g * HEAD_DIM + j), 1.0, 0.0).astype(CD)
        chunk = 1024

        def body(c, _):
            off = pl.multiple_of(c * chunk, chunk)
            kk = jnp.dot(k_ref[pl.ds(off, chunk), :], pick, preferred_element_type=F32)
            ln = lax.broadcasted_iota(jnp.int32, (chunk, LANES), 1)
            kk = jnp.where((ln == HEAD_DIM) | (ln == HEAD_DIM + 1), 1.0, kk)
            ka_ref[pl.ds(off, chunk), 0:LANES] = kk.astype(ka_ref.dtype)
            key = off + lax.broadcasted_iota(jnp.int32, (chunk, LANES), 0)
            ka_ref[pl.ds(off, chunk), LANES:2 * LANES] = jnp.where(
                (key >> 6) == ln, 1.0, 0.0).astype(ka_ref.dtype)
            vd_ref[pl.ds(off, chunk), LANES:2 * LANES] = jnp.ones((chunk, LANES), vd_ref.dtype)
            return 0

        lax.fori_loop(0, t // chunk, body, 0)
        _fill_rows(vd_ref, 0, v_ref, _dup_matrix(g))

    q = q_ref[...]
    ii = lax.broadcasted_iota(jnp.int32, (LANES, LANES), 0)
    jj = lax.broadcasted_iota(jnp.int32, (LANES, LANES), 1)
    shift = jnp.where(ii == jj + HEAD_DIM, 1.0, 0.0).astype(CD)
    lom = _lane_mask(Q_BLOCK, 0, HEAD_DIM, q.dtype)
    c0, c1 = q[:, :LANES], q[:, LANES:]
    q0 = jnp.concatenate([
        c0 * lom, jnp.dot(c0, shift, preferred_element_type=F32).astype(q.dtype),
        c1 * lom, jnp.dot(c1, shift, preferred_element_type=F32).astype(q.dtype)], axis=0)
    q1 = q0 + fq_ref[...]
    mf = mf_ref[...]
    qa_ref[...] = jnp.concatenate([q1, jnp.concatenate([mf, mf, mf, mf], axis=0)], axis=1)

    rows = NSA_REP * Q_BLOCK
    tkf = s_ref.shape[2]
    tkn = sn_ref.shape[1]
    near_off = pl.multiple_of(jnp.maximum(n - 1, 0) * Q_BLOCK, Q_BLOCK)
    n_full = near_off // tkf
    rem = near_off - n_full * tkf

    half = tkf // 2

    def fold_max(mx, s):
        for i in range(s.shape[1] // LANES):
            mx = jnp.maximum(mx, s[:, i * LANES:(i + 1) * LANES])
        return mx

    def score_tiles(j0, count):
        mx = mx_ref[...]
        for gi in range(count):
            off = pl.multiple_of((j0 + gi) * tkf, tkf)
            for hf in range(2):
                s = lax.dot_general(qa_ref[...], ka_ref[pl.ds(off + hf * half, half), :], _TRANS_B,
                                    preferred_element_type=F32)
                col = hf * half + lax.broadcasted_iota(jnp.int32, (rows, half), 1)
                s = jnp.where(col < near_off - off, s, NEG_INF)
                s_ref[j0 + gi, :, hf * half:(hf + 1) * half] = s
                mx = fold_max(mx, s)
        mx_ref[...] = mx

    def value_tiles(j0, count):
        off = pl.multiple_of(j0 * tkf, tkf)
        p = jnp.concatenate([probs(s_ref[j0 + gi]) for gi in range(count)], axis=1)
        acc_ref[...] += jnp.dot(p, vd_ref[pl.ds(off, count * tkf), :], preferred_element_type=F32)

    def probs(s):
        mb = mx_ref[...]
        return jnp.concatenate([jnp.exp2(s[:, i * LANES:(i + 1) * LANES] - mb)
                                for i in range(s.shape[1] // LANES)], axis=1).astype(CD)

    mx_ref[...] = jnp.full((rows, LANES), M_INIT, F32)
    n_far = n_full + jnp.where(rem > 0, 1, 0)
    _sweep_tiles(score_tiles, n_far)
    sn = lax.dot_general(qa_ref[...], ka_ref[pl.ds(near_off, tkn), :], _TRANS_B, preferred_element_type=F32)
    sn = sn + nb_ref[jnp.minimum(n, 1)]
    sn_ref[...] = sn
    m = jnp.max(fold_max(mx_ref[...], sn), axis=1, keepdims=True)
    mx_ref[...] = jnp.broadcast_to(m, (rows, LANES))

    acc_ref[...] = jnp.dot(probs(sn_ref[...]), vd_ref[pl.ds(near_off, tkn), :], preferred_element_type=F32)
    _sweep_tiles(value_tiles, n_far)
    acc = acc_ref[...]
    o_ref[...] = _unstack_heads(acc[:, :LANES] * (1.0 / acc[:, LANES:LANES + 1]))


def _sel_attn(main3, q_col, k_col, v_col, maskfeat, farq, nearbias, tkf):
    b, t, _ = main3.shape
    assert t % tkf == 0 and tkf % Q_BLOCK == 0
    rows = NSA_REP * Q_BLOCK
    blk = lambda w: pl.BlockSpec((None, Q_BLOCK, w), lambda i, g, n: (i, n, g))
    return pl.pallas_call(
        _sel_kernel,
        out_shape=jax.ShapeDtypeStruct((b, t, GROUP_WIDTH), F32),
        grid=(b, NSA_KV_HEADS, t // Q_BLOCK),
        in_specs=[pl.BlockSpec((None, Q_BLOCK, 2 * LANES), lambda i, g, n: (i, n, q_col + g)),
                  pl.BlockSpec((None, t, LANES), lambda i, g, n: (i, 0, k_col)),
                  pl.BlockSpec((None, t, LANES), lambda i, g, n: (i, 0, v_col)),
                  blk(LANES),
                  pl.BlockSpec((None, rows, LANES), lambda i, g, n: (g, 0, 0)),
                  pl.BlockSpec((None, 2, rows, 2 * Q_BLOCK), lambda i, g, n: (g, 0, 0, 0))],
        out_specs=blk(2 * LANES),
        scratch_shapes=[pltpu.VMEM((t, 2 * LANES), CD), pltpu.VMEM((t, 2 * LANES), CD),
                        pltpu.VMEM((rows, 2 * LANES), CD), pltpu.VMEM((t // tkf, rows, tkf), F32),
                        pltpu.VMEM((rows, 2 * Q_BLOCK), F32),
                        pltpu.VMEM((rows, LANES), F32), pltpu.VMEM((rows, 2 * LANES), F32)],
        compiler_params=_cparams("arbitrary", "arbitrary", "arbitrary"),
        name="nsa_sel_attn",
    )(main3, main3, main3, maskfeat, farq, nearbias)


def _outproj_kernel(mla_ref, fox_ref, cmp_ref, sel_ref, win_ref, swa_ref, gt_ref, ex_ref, gn_ref, w_ref, h_ref,
                    o_ref, u_ref):
    @pl.when(pl.program_id(1) == 0)
    def _():
        gate = jnp.zeros((gt_ref.shape[0], 3 * GROUP_WIDTH), F32)
        for part in _split_parts(jax.nn.sigmoid(gt_ref[...]), 2):
            gate = gate + jnp.dot(part, ex_ref[...], preferred_element_type=F32)
        nsa = (gate[:, :GROUP_WIDTH] * cmp_ref[...] + gate[:, GROUP_WIDTH:2 * GROUP_WIDTH] * sel_ref[...]
               + gate[:, 2 * GROUP_WIDTH:] * win_ref[...])
        for k, x in enumerate((mla_ref[...], fox_ref[...], nsa, swa_ref[...])):
            sl = slice(k * GROUP_WIDTH, (k + 1) * GROUP_WIDTH)
            u_ref[:, sl] = _rms(x, gn_ref[:, sl]).astype(u_ref.dtype)

    o_ref[...] = h_ref[...] + jnp.dot(u_ref[...], w_ref[...], preferred_element_type=F32)


def _gate_expansion():
    ex = np.zeros((LANES, 3 * GROUP_WIDTH), np.float32)
    for head in range(GROUP_HEADS):
        for branch in range(3):
            lo = branch * GROUP_WIDTH + head * HEAD_DIM
            ex[8 + 3 * head + branch, lo:lo + HEAD_DIM] = 1.0
    return jnp.asarray(ex, CD)


def _outproj(o_mla, o_fox, o_cmp, o_sel, o_win, o_swa, small, gn, w, h, tm, tn):
    n, d = h.shape
    mix = N_GROUPS * GROUP_WIDTH
    part = pl.BlockSpec((tm, GROUP_WIDTH), lambda i, j: (i, 0))
    ex = _gate_expansion()
    return pl.pallas_call(
        _outproj_kernel,
        out_shape=jax.ShapeDtypeStruct((n, d), F32),
        grid=(n // tm, d // tn),
        in_specs=[part, part, part, part, part, part,
                  pl.BlockSpec((tm, LANES), lambda i, j: (i, 4)),
                  pl.BlockSpec(ex.shape, lambda i, j: (0, 0)),
                  pl.BlockSpec((1, mix), lambda i, j: (0, 0)),
                  pl.BlockSpec((mix, tn), lambda i, j: (0, j)),
                  pl.BlockSpec((tm, tn), lambda i, j: (i, j))],
        out_specs=pl.BlockSpec((tm, tn), lambda i, j: (i, j)),
        scratch_shapes=[pltpu.VMEM((tm, mix), CD)],
        compiler_params=_cparams("parallel", "arbitrary"),
        name="outproj",
    )(o_mla, o_fox, o_cmp, o_sel, o_win, o_swa, small, ex, gn.reshape(1, mix), w, h)


def _mlp_kernel(*refs, final):
    if final:
        h_ref, g_ref, wu_ref, wd_ref, gf_ref, o_ref, u_ref = refs
    else:
        h_ref, g_ref, wu_ref, wd_ref, o_ref, u_ref = refs
    c = pl.program_id(1)

    @pl.when(c == 0)
    def _():
        x = h_ref[...]
        u_ref[...] = _rms(x, g_ref[...]).astype(u_ref.dtype)
        o_ref[...] = x

    m = jnp.dot(u_ref[...], wu_ref[...], preferred_element_type=F32)
    a = jnp.square(jnp.maximum(m, 0.0)).astype(CD)
    o_ref[...] += jnp.dot(a, wd_ref[...], preferred_element_type=F32)

    if final:
        @pl.when(c == pl.num_programs(1) - 1)
        def _():
            o_ref[...] = _rms(o_ref[...], gf_ref[...])


def _mlp(h, g, wu, wd, gf, tm, tf):
    n, d = h.shape
    dff = wu.shape[1]
    final = gf is not None
    in_specs = [pl.BlockSpec((tm, d), lambda i, c: (i, 0)),
                pl.BlockSpec((1, d), lambda i, c: (0, 0)),
                pl.BlockSpec((d, tf), lambda i, c: (0, c)),
                pl.BlockSpec((tf, d), lambda i, c: (c, 0))]
    args = [h, g.reshape(1, d), wu, wd]
    if final:
        in_specs.append(pl.BlockSpec((1, d), lambda i, c: (0, 0)))
        args.append(gf.reshape(1, d))
    return pl.pallas_call(
        functools.partial(_mlp_kernel, final=final),
        out_shape=jax.ShapeDtypeStruct((n, d), F32),
        grid=(n // tm, dff // tf),
        in_specs=in_specs,
        out_specs=pl.BlockSpec((tm, d), lambda i, c: (i, 0)),
        scratch_shapes=[pltpu.VMEM((tm, d), CD)],
        compiler_params=_cparams("parallel", "arbitrary"),
        name="mlp",
    )(*args)


def _t5_bucket_np(dist):
    max_exact = REL_BUCKETS // 2
    d = np.maximum(dist, 0)
    ratio = np.log(np.maximum(d, 1).astype(np.float32) / np.float32(max_exact)) / np.float32(
        math.log(REL_MAX_DIST / max_exact))
    large = max_exact + (ratio * np.float32(REL_BUCKETS - max_exact)).astype(np.int32)
    large = np.minimum(large, REL_BUCKETS - 1)
    return np.where(d < max_exact, d, large)


def _bias_tables(rel_bias):
    tbl = (rel_bias.astype(F32) * LOG2E).T.reshape(2, NSA_KV_HEADS, NSA_REP, REL_BUCKETS)
    tbl_nsa, tbl_swa = tbl[0], tbl[1]

    def lookup(tb, bucket):
        onehot = (jnp.asarray(bucket, jnp.int32)[None] == jnp.arange(REL_BUCKETS)[:, None, None]).astype(F32)
        return jnp.einsum('grb,bij->grij', tb, onehot, precision=lax.Precision.HIGHEST)
    i = np.arange(Q_BLOCK)[:, None]
    far_bucket = REL_BUCKETS - 1

    def banded(tb, window):
        jk = np.arange(window + Q_BLOCK)[None, :]
        dist = i + window - jk
        ok = (dist >= 0) & (dist < window)
        vals = lookup(tb, _t5_bucket_np(dist))
        vals = jnp.where(jnp.asarray(ok)[None, None], vals, NEG_INF)
        return vals.reshape(NSA_KV_HEADS, NSA_REP * Q_BLOCK, window + Q_BLOCK)

    bm_win = banded(tbl_nsa, NSA_WINDOW)
    bm_swa = banded(tbl_swa, SWA_WINDOW)

    far = tbl_nsa[:, :, far_bucket]
    jk = np.arange(2 * Q_BLOCK)[None, :]
    tiles = []
    for first_key_back in (0, Q_BLOCK):
        dist = i + first_key_back - jk
        delta = lookup(tbl_nsa, _t5_bucket_np(dist)) - far[:, :, None, None]
        tiles.append(jnp.where(jnp.asarray(dist >= 0)[None, None], delta, NEG_INF))
    near = jnp.stack(tiles, axis=1).reshape(NSA_KV_HEADS, 2, NSA_REP * Q_BLOCK, 2 * Q_BLOCK)
    far_rows = jnp.broadcast_to(far[:, :, None], (NSA_KV_HEADS, NSA_REP, Q_BLOCK)).reshape(NSA_KV_HEADS, -1)
    hi, lo_ = _split_parts(far_rows, 2)
    zeros = lambda k: jnp.zeros((NSA_KV_HEADS, NSA_REP * Q_BLOCK, k), CD)
    farq = jnp.concatenate([zeros(HEAD_DIM), hi[..., None], lo_[..., None], zeros(LANES - HEAD_DIM - 2)], axis=-1)

    u = np.arange(16)[None, :]
    dist_c = i - 16 * u + 113
    band = lookup(tbl_nsa, _t5_bucket_np(dist_c))
    band = jnp.where(jnp.asarray(dist_c >= 0)[None, None], band, MASK_NEG)
    band = band.reshape(NSA_KV_HEADS, NSA_REP * Q_BLOCK, 16)
    bh, bl = _split_parts(band, 2)
    neg = jnp.full((NSA_KV_HEADS, NSA_REP * Q_BLOCK, 1), MASK_NEG, CD)
    pq = jnp.concatenate([bh, bl, hi[..., None], lo_[..., None], neg, zeros(LANES - 35)], axis=-1)
    return bm_win, bm_swa, near, farq, pq


def _overlap_matrix(nc_pad, ns):
    ci = np.arange(nc_pad)[:, None]
    sj = np.arange(LANES)[None, :]
    ov = ((ci * NSA_CMP_STRIDE + NSA_CMP_LEN - 1 >= sj * NSA_SEL_LEN)
          & (ci * NSA_CMP_STRIDE <= sj * NSA_SEL_LEN + NSA_SEL_LEN - 1) & (sj < ns))
    return jnp.asarray(ov.astype(np.float32).T, CD)


def _layout_w_in(w):
    w = w.astype(F32)
    sc = HEAD_DIM ** -0.5 * LOG2E
    z = lambda k: jnp.zeros((w.shape[0], k), F32)
    main = jnp.concatenate([w[:, 544:1056] * sc, w[:, 1056:2080], w[:, 2088:2600] * sc, w[:, 3392:3904] * sc,
                            w[:, 2600:3368], w[:, 3904:4160]], axis=1)
    kr = w[:, 512:544]
    blk_a = jnp.concatenate([w[:, 2080:2088], w[:, 3368:3392], z(32), kr, z(32)], axis=1)
    blk_b = jnp.concatenate([z(64), -kr[:, 16:], kr[:, :16], z(32)], axis=1)
    small = jnp.concatenate([w[:, 0:512], blk_a, blk_b], axis=1)
    return main.astype(CD), small.astype(CD)


_FOX_Q, _FOX_K512, _FOX_V = 0, 1, 8
_NSA_Q256, _SWA_Q256 = 6, 8
_KVC_COL = 2560
_KSEL, _VSEL, _KWIN, _VWIN, _KSWA, _VSWA = 22, 23, 24, 25, 26, 27


def _layout_mla(w_uq, w_ukv):
    w3 = w_uq.astype(F32).reshape(MLA_Q_RANK, GROUP_HEADS, MLA_NOPE + MLA_ROPE)
    nope, rp = w3[:, :, :MLA_NOPE], w3[:, :, MLA_NOPE:]
    half = MLA_ROPE // 2
    sw = jnp.concatenate([-rp[:, :, half:], rp[:, :, :half]], axis=-1)
    z = lambda k: jnp.zeros((MLA_Q_RANK, GROUP_HEADS, k), F32)
    plain = jnp.concatenate([nope, rp, z(32)], axis=-1).reshape(MLA_Q_RANK, -1)
    swapped = jnp.concatenate([z(64), sw, z(32)], axis=-1).reshape(MLA_Q_RANK, -1)
    wq = jnp.concatenate([plain, swapped], axis=1)
    k3 = w_ukv.astype(F32).reshape(MLA_KV_RANK, GROUP_HEADS, 2 * HEAD_DIM)
    wk = jnp.concatenate([k3[:, :, :MLA_NOPE], jnp.zeros((MLA_KV_RANK, GROUP_HEADS, HEAD_DIM), F32)],
                         axis=-1).reshape(MLA_KV_RANK, -1)
    wv = k3[:, :, MLA_NOPE:].reshape(MLA_KV_RANK, -1)
    return wq.astype(CD), wk.astype(CD), wv.astype(CD)


def _rope_tables(t):
    inv = ROPE_THETA ** (-jnp.arange(0, MLA_ROPE, 2, dtype=F32) / MLA_ROPE)
    ang = jnp.arange(t).astype(F32)[:, None] * inv[None, :]
    cc = jnp.concatenate([jnp.cos(ang)] * 2, axis=1)
    ss = jnp.concatenate([jnp.sin(ang)] * 2, axis=1)
    one, z64, z32 = jnp.ones((t, 64), F32), jnp.zeros((t, 64), F32), jnp.zeros((t, 32), F32)
    return (jnp.concatenate([one, cc, z32], axis=1), jnp.concatenate([z64, ss, z32], axis=1),
            jnp.concatenate([z64, cc, z32], axis=1))


def kernel(x, norm_attn, w_in, mla_q_norm, mla_w_uq, mla_kv_norm, mla_w_ukv, fox_b_f, nsa_cmp_pos,
           nsa_cmp_w1, nsa_cmp_w2, swa_sinks, group_norm, w_out, norm_mlp, w_up, w_down, rel_bias,
           final_norm):
    b, t, d = x.shape
    n = b * t
    depth = w_in.shape[0]
    assert t % 1024 == 0 and d == N_GROUPS * GROUP_WIDTH
    tm = 1024 if n % 1024 == 0 else 512
    tq = 512
    nq = 4
    nr = t // NSA_CMP_STRIDE

    cosq, sinq, cosk = _rope_tables(t)
    bm_win, bm_swa, nearbias, farq, pq = _bias_tables(rel_bias)
    overlap = _overlap_matrix(nr, t // NSA_SEL_LEN)

    h = x.reshape(n, d).astype(F32)
    for l in range(depth):
        w_main, w_small = _layout_w_in(w_in[l])
        main = _norm_matmul(h, norm_attn[l], w_main, CD, tm, 896)
        small = _norm_matmul(h, norm_attn[l], w_small, F32, tm, 768)
        main3 = main.reshape(b, t, -1)
        small3 = small.reshape(b, t, -1)

        wq, wk, wv = _layout_mla(mla_w_uq[l], mla_w_ukv[l])
        q_m, k_m, v_m = _mla_prep(small, mla_q_norm[l], mla_kv_norm[l], wq, wk, wv, cosq, sinq, cosk, t, 512)
        o_mla = _dense_attn(q_m.reshape(b, t, -1), k_m.reshape(b, t, -1), v_m.reshape(b, t, -1), None,
                            fox=False, q_col=0, v_col=0, tq=tq)

        k_aug, fq = _fox_prep(small3, fox_b_f[l], main3, _FOX_K512, 512)
        o_fox = _dense_attn(main3, k_aug, main3, fq, fox=True, q_col=_FOX_Q, v_col=_FOX_V, tq=tq)

        kv16 = main3[:, :, _KVC_COL:_KVC_COL + 256].reshape(b, nr, NSA_CMP_STRIDE * 256)
        cmp = _nsa_compress(kv16, nsa_cmp_pos[l], nsa_cmp_w1[l], nsa_cmp_w2[l])
        o_cmp, maskfeat = _cmp_select(main3, _NSA_Q256, cmp, pq, overlap, nq)
        o_win = _banded_attn(main3, _NSA_Q256, _KWIN, _VWIN, bm_win, None, NSA_WINDOW, 2 * nq)
        o_sel = _sel_attn(main3, _NSA_Q256, _KSEL, _VSEL, maskfeat, farq, nearbias, tq)

        sink = (swa_sinks[l].astype(F32) * LOG2E).reshape(NSA_KV_HEADS, NSA_REP, 1, 1)
        sink_rows = jnp.concatenate(
            [jnp.broadcast_to(sink, (NSA_KV_HEADS, NSA_REP, Q_BLOCK, 1)),
             jnp.full((NSA_KV_HEADS, NSA_REP, Q_BLOCK, LANES - 1), NEG_INF, F32)],
            axis=-1).reshape(NSA_KV_HEADS, NSA_REP * Q_BLOCK, LANES)
        o_swa = _banded_attn(main3, _SWA_Q256, _KSWA, _VSWA, bm_swa, sink_rows, SWA_WINDOW, 2 * nq)

        parts = [o.reshape(n, GROUP_WIDTH) for o in (o_mla, o_fox, o_cmp, o_sel, o_win, o_swa)]
        h = _outproj(*parts, small, group_norm[l], w_out[l].astype(CD), h, 512, d)
        gf = final_norm if l == depth - 1 else None
        h = _mlp(h, norm_mlp[l], w_up[l].astype(CD), w_down[l].astype(CD), gf, 512, 1024)
    return h.reshape(b, t, d).astype(x.dtype)
```

```python
import functools
import math
from typing import NamedTuple

import numpy as np
import jax
import jax.numpy as jnp
from jax import lax
from jax.experimental import pallas as pl
from jax.experimental.pallas import tpu as pltpu

HEAD_DIM = 64
GROUP_HEADS = 8
GROUP_WIDTH = GROUP_HEADS * HEAD_DIM
N_GROUPS = 4
Q_BLOCK = 128
EPS = 1e-6
NEG_INF = -1e30

MLA_Q_RANK = 384
MLA_KV_RANK = 128
MLA_NOPE = 64
MLA_ROPE = 32
ROPE_THETA = 10000.0

NSA_KV_HEADS = 2
NSA_REP = GROUP_HEADS // NSA_KV_HEADS
NSA_CMP_LEN = 32
NSA_CMP_STRIDE = 16
NSA_SEL_LEN = 64
NSA_TOP_N = 8
NSA_WINDOW = 256
SWA_WINDOW = 128

REL_BUCKETS = 32
REL_MAX_DIST = 128

V7X_VMEM_BYTES = 64 * 2**20
VMEM_LIMIT = (V7X_VMEM_BYTES * 7) // 8
LANES = 128
CD = jnp.bfloat16
F32 = jnp.float32
MASK_NEG = -(2.0 ** 80)
M_INIT = -(2.0 ** 100)
TILE_GROUP = 8


class _Tiles(NamedTuple):
    proj_rows: int = 1024
    proj_cols: int = 896
    prep_rows: int = 512
    sweep: int = 512
    cmp_blocks: int = 4
    band_blocks: int = 8
    out_rows: int = 512
    mlp_rows: int = 1024
    mlp_cols: int = 512


_TILES = _Tiles()
LOG2E = math.log2(math.e)
MLA_SCORE_SCALE = (MLA_NOPE + MLA_ROPE) ** -0.5 * LOG2E
_TRANS_B = (((1,), (1,)), ((), ()))


def _cparams(*sem):
    return pltpu.CompilerParams(dimension_semantics=sem, vmem_limit_bytes=VMEM_LIMIT)


def _split_parts(x, n):
    parts, r = [], x
    for _ in range(n):
        p = r.astype(CD)
        parts.append(p)
        r = r - p.astype(F32)
    return parts


def _sweep_tiles(fn, n_tiles):
    groups = n_tiles // TILE_GROUP

    def grouped(i, _):
        fn(i * TILE_GROUP, TILE_GROUP)
        return 0

    lax.fori_loop(0, groups, grouped, 0)
    base = groups * TILE_GROUP
    rem = n_tiles - base
    piece = TILE_GROUP // 2
    while piece >= 1:
        @pl.when((rem & piece) != 0)
        def _(piece=piece):
            fn(base + (rem & ~(2 * piece - 1)), piece)
        piece //= 2


def _rms(x, g):
    return x * lax.rsqrt(jnp.mean(x * x, axis=-1, keepdims=True) + EPS) * g


def _norm_matmul_kernel(x_ref, g_ref, w_ref, o_ref, u_ref):
    @pl.when(pl.program_id(1) == 0)
    def _():
        u_ref[...] = _rms(x_ref[...], g_ref[...]).astype(u_ref.dtype)

    o_ref[...] = jnp.dot(u_ref[...], w_ref[...], preferred_element_type=F32).astype(o_ref.dtype)


def _norm_matmul(x, g, w, out_dtype, tm, tn):
    n, d = x.shape
    nc = w.shape[1]
    return pl.pallas_call(
        _norm_matmul_kernel,
        out_shape=jax.ShapeDtypeStruct((n, nc), out_dtype),
        grid=(n // tm, nc // tn),
        in_specs=[
            pl.BlockSpec((tm, d), lambda i, j: (i, 0)),
            pl.BlockSpec((1, d), lambda i, j: (0, 0)),
            pl.BlockSpec((d, tn), lambda i, j: (0, j)),
        ],
        out_specs=pl.BlockSpec((tm, tn), lambda i, j: (i, j)),
        scratch_shapes=[pltpu.VMEM((tm, d), CD)],
        compiler_params=_cparams("parallel", "arbitrary"),
        name="norm_matmul",
    )(x, g.reshape(1, d), w)


def _mla_prep_kernel(sm_ref, qn_ref, kvn_ref, wq_ref, wk_ref, wv_ref, cq_ref, sq_ref, ck_ref,
                     q_ref, k_ref, v_ref):
    sm = sm_ref[...]
    nq = _rms(sm[:, :MLA_Q_RANK], qn_ref[...]).astype(CD)
    nkv = _rms(sm[:, MLA_Q_RANK:MLA_Q_RANK + MLA_KV_RANK], kvn_ref[...]).astype(CD)
    blk_a = sm[:, 512:640]
    blk_b = sm[:, 640:768]
    cosq, sinq, cosk = cq_ref[...], sq_ref[...], ck_ref[...]
    hw = GROUP_HEADS * LANES
    qq = jnp.dot(nq, wq_ref[...], preferred_element_type=F32)
    kk = jnp.dot(nkv, wk_ref[...], preferred_element_type=F32)
    kpe = blk_a * cosk + blk_b * sinq
    for h in range(GROUP_HEADS):
        sl = slice(h * LANES, (h + 1) * LANES)
        qh = qq[:, sl] * cosq + qq[:, hw + h * LANES:hw + (h + 1) * LANES] * sinq
        q_ref[:, sl] = (qh * MLA_SCORE_SCALE).astype(q_ref.dtype)
        k_ref[:, sl] = (kk[:, sl] + kpe).astype(k_ref.dtype)
    v_ref[...] = jnp.dot(nkv, wv_ref[...], preferred_element_type=F32).astype(v_ref.dtype)


def _mla_prep(small, qn, kvn, wq, wk, wv, cosq, sinq, cosk, t, tm):
    n = small.shape[0]
    tb = t // tm
    hw = GROUP_HEADS * LANES
    full = lambda a: pl.BlockSpec(a.shape, lambda i: (0,) * a.ndim)
    tab = pl.BlockSpec((tm, LANES), lambda i: (i % tb, 0))
    qn = qn.reshape(1, -1)
    kvn = kvn.reshape(1, -1)
    return pl.pallas_call(
        _mla_prep_kernel,
        out_shape=(jax.ShapeDtypeStruct((n, hw), CD), jax.ShapeDtypeStruct((n, hw), CD),
                   jax.ShapeDtypeStruct((n, GROUP_WIDTH), CD)),
        grid=(n // tm,),
        in_specs=[pl.BlockSpec((tm, small.shape[1]), lambda i: (i, 0)), full(qn), full(kvn),
                  full(wq), full(wk), full(wv), tab, tab, tab],
        out_specs=(pl.BlockSpec((tm, hw), lambda i: (i, 0)), pl.BlockSpec((tm, hw), lambda i: (i, 0)),
                   pl.BlockSpec((tm, GROUP_WIDTH), lambda i: (i, 0))),
        compiler_params=_cparams("parallel"),
        name="mla_prep",
    )(small, qn, kvn, wq, wk, wv, cosq, sinq, cosk)


def _fox_prep_kernel(fl_ref, bf_ref, k_ref, efq_ref, ekf_ref, kaug_ref, fq_ref, carry_ref, *, tc):
    @pl.when(pl.program_id(1) == 0)
    def _():
        carry_ref[...] = jnp.zeros_like(carry_ref)

    x = fl_ref[...] + bf_ref[...]
    logf = -(jnp.maximum(-x, 0.0) + jnp.log1p(jnp.exp(-jnp.abs(x))))
    lane = lax.broadcasted_iota(jnp.int32, logf.shape, 1)
    logf = jnp.where(lane < GROUP_HEADS, logf, 0.0)
    row = lax.broadcasted_iota(jnp.int32, (tc, tc), 0)
    col = lax.broadcasted_iota(jnp.int32, (tc, tc), 1)
    tri = jnp.where(row >= col, 1.0, 0.0).astype(CD)
    cs = jnp.zeros(logf.shape, F32)
    for part in _split_parts(logf, 3):
        cs = cs + jnp.dot(tri, part, preferred_element_type=F32)
    fc = cs + carry_ref[...]
    carry_ref[...] = fc[tc - 1:tc, :]
    parts = _split_parts(fc * LOG2E, 3)
    fq = jnp.zeros(fq_ref.shape, F32)
    kf = jnp.zeros(fq_ref.shape, F32)
    for i, part in enumerate(parts):
        fq = fq + jnp.dot(part, efq_ref[...], preferred_element_type=F32)
        kf = kf + jnp.dot(part, ekf_ref[i], preferred_element_type=F32)
    fq_ref[...] = fq
    k = k_ref[...]
    for p in range(GROUP_HEADS // 2):
        kaug_ref[:, 2 * p * LANES:(2 * p + 1) * LANES] = k[:, p * LANES:(p + 1) * LANES]
        kaug_ref[:, (2 * p + 1) * LANES:(2 * p + 2) * LANES] = kf[:, p * LANES:(p + 1) * LANES].astype(kaug_ref.dtype)


def _fox_prep(small3, b_f, main3, k_col, tc):
    b, t, _ = small3.shape
    pairs = GROUP_HEADS // 2
    efq = np.zeros((LANES, pairs * LANES), np.float32)
    ekf = np.zeros((3, LANES, pairs * LANES), np.float32)
    for h in range(GROUP_HEADS):
        p, a = divmod(h, 2)
        efq[h, p * LANES + a] = 1.0
        for i in range(3):
            ekf[i, h, p * LANES + 3 * a + i] = -1.0
    bf = jnp.zeros((1, LANES), F32).at[0, :GROUP_HEADS].set(b_f.astype(F32))
    return pl.pallas_call(
        functools.partial(_fox_prep_kernel, tc=tc),
        out_shape=(jax.ShapeDtypeStruct((b, t, 2 * GROUP_WIDTH), CD),
                   jax.ShapeDtypeStruct((b, t, GROUP_WIDTH), F32)),
        grid=(b, t // tc),
        in_specs=[pl.BlockSpec((None, tc, LANES), lambda i, j: (i, j, 4)),
                  pl.BlockSpec((1, LANES), lambda i, j: (0, 0)),
                  pl.BlockSpec((None, tc, GROUP_WIDTH), lambda i, j: (i, j, k_col)),
                  pl.BlockSpec(efq.shape, lambda i, j: (0, 0)),
                  pl.BlockSpec(ekf.shape, lambda i, j: (0, 0, 0))],
        out_specs=(pl.BlockSpec((None, tc, 2 * GROUP_WIDTH), lambda i, j: (i, j, 0)),
                   pl.BlockSpec((None, tc, GROUP_WIDTH), lambda i, j: (i, j, 0))),
        scratch_shapes=[pltpu.VMEM((1, LANES), F32)],
        compiler_params=_cparams("arbitrary", "arbitrary"),
        name="fox_prep",
    )(small3, bf, main3, jnp.asarray(efq, CD), jnp.asarray(ekf, CD))


def _dense_attn_kernel(*refs, fox, tq):
    if fox:
        q_ref, k_ref, v_ref, fq_ref, o_ref, va_ref, qa_ref, s_ref, mx_ref, acc_ref = refs
    else:
        q_ref, k_ref, v_ref, o_ref, va_ref, qa_ref, s_ref, mx_ref, acc_ref = refs
    qi = pl.program_id(2)
    t = k_ref.shape[0]
    nch = tq // LANES

    @pl.when(qi == 0)
    def _():
        chunk = 1024

        def body(i, _):
            off = pl.multiple_of(i * chunk, chunk)
            va_ref[pl.ds(off, chunk), 0:LANES] = v_ref[pl.ds(off, chunk), :]
            va_ref[pl.ds(off, chunk), LANES:2 * LANES] = jnp.ones((chunk, LANES), va_ref.dtype)
            return 0

        lax.fori_loop(0, t // chunk, body, 0)

    half = tq // 2
    outs = []
    for h in range(2):
        q = q_ref[...]
        if fox:
            qa_ref[...] = jnp.concatenate(
                [q * _lane_mask(tq, h * HEAD_DIM, (h + 1) * HEAD_DIM, q.dtype), _lane_mask(tq, 3 * h, 3 * h + 3, q.dtype)],
                axis=1)
        else:
            qa_ref[...] = q[:, h * LANES:(h + 1) * LANES]

        def score_tiles(j0, count, h=h):
            mx = mx_ref[...]
            for gi in range(count):
                off = pl.multiple_of((j0 + gi) * tq, tq)
                for hf in range(2):
                    rows = pl.ds(off + hf * half, half)
                    k = k_ref[rows, :] if fox else k_ref[rows, pl.ds(h * LANES, LANES)]
                    s = lax.dot_general(qa_ref[...], k, _TRANS_B, preferred_element_type=F32)
                    delta = (hf * half + lax.broadcasted_iota(jnp.int32, (tq, half), 1)
                             - lax.broadcasted_iota(jnp.int32, (tq, half), 0))
                    s = jnp.where(delta <= (qi - j0 - gi) * tq, s, NEG_INF)
                    s_ref[j0 + gi, :, hf * half:(hf + 1) * half] = s
                    for i in range(half // LANES):
                        mx = jnp.maximum(mx, s[:, i * LANES:(i + 1) * LANES])
            mx_ref[...] = mx

        def value_tiles(j0, count):
            mc = mx_ref[...]
            ps = []
            for gi in range(count):
                s = s_ref[j0 + gi]
                ps += [jnp.exp2(s[:, i * LANES:(i + 1) * LANES] - mc) for i in range(nch)]
            off = pl.multiple_of(j0 * tq, tq)
            acc_ref[...] += jnp.dot(jnp.concatenate(ps, axis=1).astype(CD), va_ref[pl.ds(off, count * tq), :],
                                    preferred_element_type=F32)

        mx_ref[...] = jnp.full(mx_ref.shape, M_INIT, F32)
        _sweep_tiles(score_tiles, qi + 1)
        m = jnp.max(mx_ref[...], axis=1, keepdims=True)
        if fox:
            f = fq_ref[:, h:h + 1]
            m = (m + f) - f
        mx_ref[...] = jnp.broadcast_to(m, (tq, LANES))

        acc_ref[...] = jnp.zeros(acc_ref.shape, F32)
        _sweep_tiles(value_tiles, qi + 1)
        acc = acc_ref[...]
        outs.append(acc[:, :LANES] * (1.0 / acc[:, LANES:LANES + 1]))
    lane = lax.broadcasted_iota(jnp.int32, (tq, LANES), 1)
    o_ref[...] = jnp.where(lane < HEAD_DIM, outs[0], outs[1]).astype(o_ref.dtype)


def _dense_attn(q3, k3, v3, fq3, *, fox, q_col, v_col, tq):
    b, t, _ = q3.shape
    assert t % tq == 0 and t % 1024 == 0
    pairs = GROUP_HEADS // 2
    qw = LANES if fox else 2 * LANES
    kd = 2 * LANES if fox else LANES
    in_specs = [pl.BlockSpec((None, tq, qw), lambda i, p, j: (i, j, q_col + p)),
                pl.BlockSpec((None, t, 2 * LANES), lambda i, p, j: (i, 0, p)),
                pl.BlockSpec((None, t, LANES), lambda i, p, j: (i, 0, v_col + p))]
    args = [q3, k3, v3]
    if fox:
        in_specs.append(pl.BlockSpec((None, tq, LANES), lambda i, p, j: (i, j, p)))
        args.append(fq3)
    return pl.pallas_call(
        functools.partial(_dense_attn_kernel, fox=fox, tq=tq),
        out_shape=jax.ShapeDtypeStruct((b, t, GROUP_WIDTH), F32),
        grid=(b, pairs, t // tq),
        in_specs=in_specs,
        out_specs=pl.BlockSpec((None, tq, LANES), lambda i, p, j: (i, j, p)),
        scratch_shapes=[pltpu.VMEM((t, 2 * LANES), CD), pltpu.VMEM((tq, kd), CD),
                        pltpu.VMEM((t // tq, tq, tq), F32), pltpu.VMEM((tq, LANES), F32),
                        pltpu.VMEM((tq, 2 * LANES), F32)],
        compiler_params=_cparams("arbitrary", "arbitrary", "arbitrary"),
        name="fox_attn" if fox else "mla_attn",
    )(*args)


def _gelu_tanh(x):
    return 0.5 * x * (1.0 + jnp.tanh(math.sqrt(2.0 / math.pi) * (x + 0.044715 * (x * x * x))))


def _compress_kernel(r_ref, w1a_ref, w1b_ref, pa_ref, pb_ref, w2_ref, o_ref):
    r = r_ref[...]
    nr = r.shape[0]
    ya = jnp.dot(r, w1a_ref[...], preferred_element_type=F32)
    yb = jnp.dot(r, w1b_ref[...], preferred_element_type=F32)
    pc = (jnp.dot(pa_ref[...], w1a_ref[...], preferred_element_type=F32)
          + jnp.dot(pb_ref[...], w1b_ref[...], preferred_element_type=F32))[0:1, :]
    pre = ya + pltpu.roll(yb, nr - 1, 0) + pc
    h1 = _gelu_tanh(pre).astype(CD)
    rowi = lax.broadcasted_iota(jnp.int32, (nr, LANES), 0)
    for g in range(NSA_KV_HEADS):
        y = jnp.dot(h1, w2_ref[g], preferred_element_type=F32)
        o_ref[g] = jnp.where(rowi < nr - 1, y, 0.0).astype(o_ref.dtype)


def _nsa_compress(kv16, cmp_pos, cmp_w1, cmp_w2):
    b, nr, kw = kv16.shape
    half = NSA_CMP_LEN // 2
    eye2 = jnp.eye(2, dtype=F32)
    w1r = cmp_w1.astype(F32).reshape(2, NSA_CMP_LEN, HEAD_DIM, HEAD_DIM)
    expand = lambda w: jnp.einsum('klde,kw,gh->klwgdhe', w, eye2, eye2).reshape(2, kw, LANES).astype(CD)
    w1a, w1b = expand(w1r[:, :half]), expand(w1r[:, half:])
    posr = cmp_pos.astype(F32)
    tile = lambda p: jnp.broadcast_to(p[:, None, :, None, None, :], (2, 8, half, 2, 2, HEAD_DIM)).reshape(2, 8, kw).astype(CD)
    pa, pb = tile(posr[:, :half]), tile(posr[:, half:])
    w2 = cmp_w2.astype(F32)
    w2d = jnp.zeros((2, NSA_KV_HEADS, LANES, LANES), F32)
    for g in range(NSA_KV_HEADS):
        blk = jnp.concatenate([w2, w2], axis=2)
        w2d = w2d.at[:, g, g * HEAD_DIM:(g + 1) * HEAD_DIM, :].set(blk)
    w2d = w2d.astype(CD)
    return pl.pallas_call(
        _compress_kernel,
        out_shape=jax.ShapeDtypeStruct((b, 2, NSA_KV_HEADS, nr, LANES), CD),
        grid=(b, 2),
        in_specs=[pl.BlockSpec((None, nr, kw), lambda i, k: (i, 0, 0)),
                  pl.BlockSpec((None, kw, LANES), lambda i, k: (k, 0, 0)),
                  pl.BlockSpec((None, kw, LANES), lambda i, k: (k, 0, 0)),
                  pl.BlockSpec((None, 8, kw), lambda i, k: (k, 0, 0)),
                  pl.BlockSpec((None, 8, kw), lambda i, k: (k, 0, 0)),
                  pl.BlockSpec((None, NSA_KV_HEADS, LANES, LANES), lambda i, k: (k, 0, 0, 0))],
        out_specs=pl.BlockSpec((None, None, NSA_KV_HEADS, nr, LANES), lambda i, k: (i, k, 0, 0, 0)),
        compiler_params=_cparams("parallel", "arbitrary"),
        name="nsa_compress",
    )(kv16, w1a, w1b, pa, pb, w2d)


def _lane_mask(rows, lo, hi, dtype):
    lane = lax.broadcasted_iota(jnp.int32, (rows, LANES), 1)
    return jnp.where((lane >= lo) & (lane < hi), 1.0, 0.0).astype(dtype)


def _stack_heads_native(q):
    lo = _lane_mask(Q_BLOCK, 0, HEAD_DIM, q.dtype)
    hi = _lane_mask(Q_BLOCK, HEAD_DIM, LANES, q.dtype)
    c0, c1 = q[:, :LANES], q[:, LANES:]
    return jnp.concatenate([c0 * lo, c0 * hi, c1 * lo, c1 * hi], axis=0)


def _unstack_heads(o):
    lane = lax.broadcasted_iota(jnp.int32, (Q_BLOCK, LANES), 1)
    lo = lane < HEAD_DIM
    return jnp.concatenate([jnp.where(lo, o[0:128], o[128:256]), jnp.where(lo, o[256:384], o[384:512])], axis=1)


def _dup_matrix(g):
    i = lax.broadcasted_iota(jnp.int32, (LANES, LANES), 0)
    j = lax.broadcasted_iota(jnp.int32, (LANES, LANES), 1)
    return jnp.where(i == g * HEAD_DIM + (j & (HEAD_DIM - 1)), 1.0, 0.0).astype(CD)


def _fill_rows(dst_ref, row0, src_ref, mat, chunk=1024):
    n = src_ref.shape[0]

    def body(i, _):
        off = pl.multiple_of(i * chunk, chunk)
        dst_ref[pl.ds(row0 + off, chunk), 0:LANES] = jnp.dot(
            src_ref[pl.ds(off, chunk), :], mat, preferred_element_type=F32).astype(dst_ref.dtype)
        return 0

    lax.fori_loop(0, n // chunk, body, 0)


def _cmp_select_kernel(q_ref, kc_ref, vc_ref, pq_ref, ov_ref, o_ref, mf_ref, *, nq):
    nc = kc_ref.shape[0]
    ci = lax.broadcasted_iota(jnp.int32, (nc, LANES), 0)
    f = lax.broadcasted_iota(jnp.int32, (nc, LANES), 1)
    j = lax.broadcasted_iota(jnp.int32, (LANES, Q_BLOCK), 1)
    i = lax.broadcasted_iota(jnp.int32, (LANES, Q_BLOCK), 0)
    bf = i.astype(F32)
    one_if = lambda cond: jnp.where(cond, 1.0, 0.0)
    for u in range(nq):
        n = pl.program_id(2) * nq + u
        rs = slice(u * Q_BLOCK, (u + 1) * Q_BLOCK)
        qaug = jnp.concatenate([_stack_heads_native(q_ref[rs, :]), pq_ref[...]], axis=1)
        uc = ci - 8 * n + 9
        feat = jnp.where(f < 32, one_if(uc == (f & 15)),
                         jnp.where(f < 34, one_if(uc < 0), one_if((f == 34) & (uc > 15)))).astype(CD)
        kaug = jnp.concatenate([kc_ref[...], feat], axis=1)
        s = lax.dot_general(qaug, kaug, _TRANS_B, preferred_element_type=F32)
        m = jnp.max(s, axis=1, keepdims=True)
        p = jnp.exp2(s - m)
        l = jnp.sum(p, axis=1, keepdims=True)
        pc = p * jnp.where(m > 0.5 * MASK_NEG, 1.0 / l, 0.0)
        o = jnp.dot(pc.astype(CD), vc_ref[...], preferred_element_type=F32)
        o_ref[rs, :] = _unstack_heads(o)
        pcs = pc[0:128] + pc[128:256] + pc[256:384] + pc[384:512]
        imp = lax.dot_general(ov_ref[...], pcs.astype(CD), _TRANS_B, preferred_element_type=F32)
        cur = 2 * n + jnp.where(j >= NSA_SEL_LEN, 1, 0)
        causal = i <= cur
        forced = (i == 0) | (i == cur) | (i == cur - 1)
        sel = one_if(forced & causal)
        score = jnp.where(causal & jnp.logical_not(forced), imp, -3.0e38)
        for _ in range(NSA_TOP_N - 3):
            mx = jnp.max(score, axis=0, keepdims=True)
            idx = jnp.min(jnp.where(score == mx, bf, float(LANES)), axis=0, keepdims=True)
            pick = bf == idx
            sel = jnp.where(pick, 1.0, sel)
            score = jnp.where(pick, -3.0e38, score)
        mf_ref[rs, :] = jnp.where(sel > 0.5, 0.0, MASK_NEG).T.astype(mf_ref.dtype)


def _cmp_select(main3, q_col, cmp, pq, overlap, nq):
    b, t, _ = main3.shape
    nc = cmp.shape[3]
    tq = nq * Q_BLOCK
    assert t % tq == 0
    return pl.pallas_call(
        functools.partial(_cmp_select_kernel, nq=nq),
        out_shape=(jax.ShapeDtypeStruct((b, t, GROUP_WIDTH), F32),
                   jax.ShapeDtypeStruct((b, t, NSA_KV_HEADS * LANES), CD)),
        grid=(b, NSA_KV_HEADS, t // tq),
        in_specs=[pl.BlockSpec((None, tq, 2 * LANES), lambda i, g, n: (i, n, q_col + g)),
                  pl.BlockSpec((None, None, None, nc, LANES), lambda i, g, n: (i, 0, g, 0, 0)),
                  pl.BlockSpec((None, None, None, nc, LANES), lambda i, g, n: (i, 1, g, 0, 0)),
                  pl.BlockSpec((None, NSA_REP * Q_BLOCK, LANES), lambda i, g, n: (g, 0, 0)),
                  pl.BlockSpec(overlap.shape, lambda i, g, n: (0, 0))],
        out_specs=(pl.BlockSpec((None, tq, 2 * LANES), lambda i, g, n: (i, n, g)),
                   pl.BlockSpec((None, tq, LANES), lambda i, g, n: (i, n, g))),
        compiler_params=_cparams("parallel", "parallel", "arbitrary"),
        name="nsa_cmp_select",
    )(main3, cmp, cmp, pq, overlap)


def _banded_kernel(*refs, window, sinks, nq):
    if sinks:
        q_ref, k_ref, v_ref, bm_ref, sk_ref, o_ref, kp_ref, vp_ref = refs
    else:
        q_ref, k_ref, v_ref, bm_ref, o_ref, kp_ref, vp_ref = refs
    g = pl.program_id(1)
    kw = window + Q_BLOCK

    @pl.when(pl.program_id(2) == 0)
    def _():
        dup = _dup_matrix(g)
        kp_ref[0:window, :] = jnp.zeros((window, LANES), kp_ref.dtype)
        vp_ref[0:window, :] = jnp.zeros((window, LANES), vp_ref.dtype)
        _fill_rows(kp_ref, window, k_ref, dup)
        _fill_rows(vp_ref, window, v_ref, dup)

    for u in range(nq):
        rs = slice(u * Q_BLOCK, (u + 1) * Q_BLOCK)
        start = pl.multiple_of((pl.program_id(2) * nq + u) * Q_BLOCK, Q_BLOCK)
        ks = kp_ref[pl.ds(start, kw), :]
        vs = vp_ref[pl.ds(start, kw), :]
        s = lax.dot_general(_stack_heads_native(q_ref[rs, :]), ks, _TRANS_B, preferred_element_type=F32)
        s = s + bm_ref[...]
        kpos = start - window + lax.broadcasted_iota(jnp.int32, s.shape, 1)
        s = jnp.where(kpos >= 0, s, NEG_INF)
        if sinks:
            s = jnp.concatenate([s, sk_ref[...]], axis=1)
        m = jnp.max(s, axis=1, keepdims=True)
        p = jnp.exp2(s - m)
        l = jnp.sum(p, axis=1, keepdims=True)
        o = jnp.dot(p[:, :kw].astype(CD), vs, preferred_element_type=F32) * (1.0 / l)
        o_ref[rs, :] = _unstack_heads(o)


def _banded_attn(main3, q_col, k_col, v_col, biasmask, sink_rows, window, nq):
    b, t, _ = main3.shape
    kw = window + Q_BLOCK
    tq = nq * Q_BLOCK
    assert t % tq == 0
    sinks = sink_rows is not None
    in_specs = [pl.BlockSpec((None, tq, 2 * LANES), lambda i, g, n: (i, n, q_col + g)),
                pl.BlockSpec((None, t, LANES), lambda i, g, n: (i, 0, k_col)),
                pl.BlockSpec((None, t, LANES), lambda i, g, n: (i, 0, v_col)),
                pl.BlockSpec((None, NSA_REP * Q_BLOCK, kw), lambda i, g, n: (g, 0, 0))]
    args = [main3, main3, main3, biasmask]
    if sinks:
        in_specs.append(pl.BlockSpec((None, NSA_REP * Q_BLOCK, LANES), lambda i, g, n: (g, 0, 0)))
        args.append(sink_rows)
    return pl.pallas_call(
        functools.partial(_banded_kernel, window=window, sinks=sinks, nq=nq),
        out_shape=jax.ShapeDtypeStruct((b, t, GROUP_WIDTH), F32),
        grid=(b, NSA_KV_HEADS, t // tq),
        in_specs=in_specs,
        out_specs=pl.BlockSpec((None, tq, 2 * LANES), lambda i, g, n: (i, n, g)),
        scratch_shapes=[pltpu.VMEM((window + t, LANES), CD), pltpu.VMEM((window + t, LANES), CD)],
        compiler_params=_cparams("arbitrary", "arbitrary", "arbitrary"),
        name="swa_attn" if sinks else "nsa_win_attn",
    )(*args)


def _sel_kernel(q_ref, k_ref, v_ref, mf_ref, fq_ref, nb_ref, o_ref,
                ka_ref, vd_ref, qa_ref, s_ref, sn_ref, mx_ref, acc_ref):
    g = pl.program_id(1)
    n = pl.program_id(2)
    t = k_ref.shape[0]

    @pl.when(n == 0)
    def _():
        i = lax.broadcasted_iota(jnp.int32, (LANES, LANES), 0)
        j = lax.broadcasted_iota(jnp.int32, (LANES, LANES), 1)
        pick = jnp.where((j < HEAD_DIM) & (i == g * HEAD_DIM + j), 1.0, 0.0).astype(CD)
        chunk = 1024

        def body(c, _):
            off = pl.multiple_of(c * chunk, chunk)
            kk = jnp.dot(k_ref[pl.ds(off, chunk), :], pick, preferred_element_type=F32)
            ln = lax.broadcasted_iota(jnp.int32, (chunk, LANES), 1)
            kk = jnp.where((ln == HEAD_DIM) | (ln == HEAD_DIM + 1), 1.0, kk)
            ka_ref[pl.ds(off, chunk), 0:LANES] = kk.astype(ka_ref.dtype)
            key = off + lax.broadcasted_iota(jnp.int32, (chunk, LANES), 0)
            ka_ref[pl.ds(off, chunk), LANES:2 * LANES] = jnp.where(
                (key >> 6) == ln, 1.0, 0.0).astype(ka_ref.dtype)
            vd_ref[pl.ds(off, chunk), LANES:2 * LANES] = jnp.ones((chunk, LANES), vd_ref.dtype)
            return 0

        lax.fori_loop(0, t // chunk, body, 0)
        _fill_rows(vd_ref, 0, v_ref, _dup_matrix(g))

    q = q_ref[...]
    ii = lax.broadcasted_iota(jnp.int32, (LANES, LANES), 0)
    jj = lax.broadcasted_iota(jnp.int32, (LANES, LANES), 1)
    shift = jnp.where(ii == jj + HEAD_DIM, 1.0, 0.0).astype(CD)
    lom = _lane_mask(Q_BLOCK, 0, HEAD_DIM, q.dtype)
    c0, c1 = q[:, :LANES], q[:, LANES:]
    q0 = jnp.concatenate([
        c0 * lom, jnp.dot(c0, shift, preferred_element_type=F32).astype(q.dtype),
        c1 * lom, jnp.dot(c1, shift, preferred_element_type=F32).astype(q.dtype)], axis=0)
    q1 = q0 + fq_ref[...]
    mf = mf_ref[...]
    qa_ref[...] = jnp.concatenate([q1, jnp.concatenate([mf, mf, mf, mf], axis=0)], axis=1)

    rows = NSA_REP * Q_BLOCK
    tkf = s_ref.shape[2]
    tkn = sn_ref.shape[1]
    near_off = pl.multiple_of(jnp.maximum(n - 1, 0) * Q_BLOCK, Q_BLOCK)
    n_full = near_off // tkf
    rem = near_off - n_full * tkf

    half = tkf // 2

    def fold_max(mx, s):
        for i in range(s.shape[1] // LANES):
            mx = jnp.maximum(mx, s[:, i * LANES:(i + 1) * LANES])
        return mx

    def score_tiles(j0, count):
        mx = mx_ref[...]
        for gi in range(count):
            off = pl.multiple_of((j0 + gi) * tkf, tkf)
            for hf in range(2):
                s = lax.dot_general(qa_ref[...], ka_ref[pl.ds(off + hf * half, half), :], _TRANS_B,
                                    preferred_element_type=F32)
                col = hf * half + lax.broadcasted_iota(jnp.int32, (rows, half), 1)
                s = jnp.where(col < near_off - off, s, NEG_INF)
                s_ref[j0 + gi, :, hf * half:(hf + 1) * half] = s
                mx = fold_max(mx, s)
        mx_ref[...] = mx

    def value_tiles(j0, count):
        off = pl.multiple_of(j0 * tkf, tkf)
        p = jnp.concatenate([probs(s_ref[j0 + gi]) for gi in range(count)], axis=1)
        acc_ref[...] += jnp.dot(p, vd_ref[pl.ds(off, count * tkf), :], preferred_element_type=F32)

    def probs(s):
        mb = mx_ref[...]
        return jnp.concatenate([jnp.exp2(s[:, i * LANES:(i + 1) * LANES] - mb)
                                for i in range(s.shape[1] // LANES)], axis=1).astype(CD)

    mx_ref[...] = jnp.full((rows, LANES), M_INIT, F32)
    n_far = n_full + jnp.where(rem > 0, 1, 0)
    _sweep_tiles(score_tiles, n_far)
    sn = lax.dot_general(qa_ref[...], ka_ref[pl.ds(near_off, tkn), :], _TRANS_B, preferred_element_type=F32)
    sn = sn + nb_ref[jnp.minimum(n, 1)]
    sn_ref[...] = sn
    m = jnp.max(fold_max(mx_ref[...], sn), axis=1, keepdims=True)
    mx_ref[...] = jnp.broadcast_to(m, (rows, LANES))

    acc_ref[...] = jnp.dot(probs(sn_ref[...]), vd_ref[pl.ds(near_off, tkn), :], preferred_element_type=F32)
    _sweep_tiles(value_tiles, n_far)
    acc = acc_ref[...]
    o_ref[...] = _unstack_heads(acc[:, :LANES] * (1.0 / acc[:, LANES:LANES + 1]))


def _sel_attn(main3, q_col, k_col, v_col, maskfeat, farq, nearbias, tkf):
    b, t, _ = main3.shape
    assert t % tkf == 0 and tkf % Q_BLOCK == 0
    rows = NSA_REP * Q_BLOCK
    blk = lambda w: pl.BlockSpec((None, Q_BLOCK, w), lambda i, g, n: (i, n, g))
    return pl.pallas_call(
        _sel_kernel,
        out_shape=jax.ShapeDtypeStruct((b, t, GROUP_WIDTH), F32),
        grid=(b, NSA_KV_HEADS, t // Q_BLOCK),
        in_specs=[pl.BlockSpec((None, Q_BLOCK, 2 * LANES), lambda i, g, n: (i, n, q_col + g)),
                  pl.BlockSpec((None, t, LANES), lambda i, g, n: (i, 0, k_col)),
                  pl.BlockSpec((None, t, LANES), lambda i, g, n: (i, 0, v_col)),
                  blk(LANES),
                  pl.BlockSpec((None, rows, LANES), lambda i, g, n: (g, 0, 0)),
                  pl.BlockSpec((None, 2, rows, 2 * Q_BLOCK), lambda i, g, n: (g, 0, 0, 0))],
        out_specs=blk(2 * LANES),
        scratch_shapes=[pltpu.VMEM((t, 2 * LANES), CD), pltpu.VMEM((t, 2 * LANES), CD),
                        pltpu.VMEM((rows, 2 * LANES), CD), pltpu.VMEM((t // tkf, rows, tkf), F32),
                        pltpu.VMEM((rows, 2 * Q_BLOCK), F32),
                        pltpu.VMEM((rows, LANES), F32), pltpu.VMEM((rows, 2 * LANES), F32)],
        compiler_params=_cparams("arbitrary", "arbitrary", "arbitrary"),
        name="nsa_sel_attn",
    )(main3, main3, main3, maskfeat, farq, nearbias)


def _outproj_kernel(mla_ref, fox_ref, cmp_ref, sel_ref, win_ref, swa_ref, gt_ref, ex_ref, gn_ref, w_ref, h_ref,
                    o_ref, u_ref):
    @pl.when(pl.program_id(1) == 0)
    def _():
        gate = jnp.zeros((gt_ref.shape[0], 3 * GROUP_WIDTH), F32)
        for part in _split_parts(jax.nn.sigmoid(gt_ref[...]), 2):
            gate = gate + jnp.dot(part, ex_ref[...], preferred_element_type=F32)
        nsa = (gate[:, :GROUP_WIDTH] * cmp_ref[...] + gate[:, GROUP_WIDTH:2 * GROUP_WIDTH] * sel_ref[...]
               + gate[:, 2 * GROUP_WIDTH:] * win_ref[...])
        for k, x in enumerate((mla_ref[...], fox_ref[...], nsa, swa_ref[...])):
            sl = slice(k * GROUP_WIDTH, (k + 1) * GROUP_WIDTH)
            u_ref[:, sl] = _rms(x, gn_ref[:, sl]).astype(u_ref.dtype)

    o_ref[...] = h_ref[...] + jnp.dot(u_ref[...], w_ref[...], preferred_element_type=F32)


def _gate_expansion():
    ex = np.zeros((LANES, 3 * GROUP_WIDTH), np.float32)
    for head in range(GROUP_HEADS):
        for branch in range(3):
            lo = branch * GROUP_WIDTH + head * HEAD_DIM
            ex[8 + 3 * head + branch, lo:lo + HEAD_DIM] = 1.0
    return jnp.asarray(ex, CD)


def _outproj(o_mla, o_fox, o_cmp, o_sel, o_win, o_swa, small, gn, w, h, tm, tn):
    n, d = h.shape
    mix = N_GROUPS * GROUP_WIDTH
    part = pl.BlockSpec((tm, GROUP_WIDTH), lambda i, j: (i, 0))
    ex = _gate_expansion()
    return pl.pallas_call(
        _outproj_kernel,
        out_shape=jax.ShapeDtypeStruct((n, d), F32),
        grid=(n // tm, d // tn),
        in_specs=[part, part, part, part, part, part,
                  pl.BlockSpec((tm, LANES), lambda i, j: (i, 4)),
                  pl.BlockSpec(ex.shape, lambda i, j: (0, 0)),
                  pl.BlockSpec((1, mix), lambda i, j: (0, 0)),
                  pl.BlockSpec((mix, tn), lambda i, j: (0, j)),
                  pl.BlockSpec((tm, tn), lambda i, j: (i, j))],
        out_specs=pl.BlockSpec((tm, tn), lambda i, j: (i, j)),
        scratch_shapes=[pltpu.VMEM((tm, mix), CD)],
        compiler_params=_cparams("parallel", "arbitrary"),
        name="outproj",
    )(o_mla, o_fox, o_cmp, o_sel, o_win, o_swa, small, ex, gn.reshape(1, mix), w, h)


def _mlp_kernel(*refs, final):
    if final:
        h_ref, g_ref, wu_ref, wd_ref, gf_ref, o_ref, u_ref = refs
    else:
        h_ref, g_ref, wu_ref, wd_ref, o_ref, u_ref = refs
    c = pl.program_id(1)

    @pl.when(c == 0)
    def _():
        x = h_ref[...]
        u_ref[...] = _rms(x, g_ref[...]).astype(u_ref.dtype)
        o_ref[...] = x

    m = jnp.dot(u_ref[...], wu_ref[...], preferred_element_type=F32)
    a = jnp.square(jnp.maximum(m, 0.0)).astype(CD)
    o_ref[...] += jnp.dot(a, wd_ref[...], preferred_element_type=F32)

    if final:
        @pl.when(c == pl.num_programs(1) - 1)
        def _():
            o_ref[...] = _rms(o_ref[...], gf_ref[...])


def _mlp(h, g, wu, wd, gf, tm, tf):
    n, d = h.shape
    dff = wu.shape[1]
    final = gf is not None
    in_specs = [pl.BlockSpec((tm, d), lambda i, c: (i, 0)),
                pl.BlockSpec((1, d), lambda i, c: (0, 0)),
                pl.BlockSpec((d, tf), lambda i, c: (0, c)),
                pl.BlockSpec((tf, d), lambda i, c: (c, 0))]
    args = [h, g.reshape(1, d), wu, wd]
    if final:
        in_specs.append(pl.BlockSpec((1, d), lambda i, c: (0, 0)))
        args.append(gf.reshape(1, d))
    return pl.pallas_call(
        functools.partial(_mlp_kernel, final=final),
        out_shape=jax.ShapeDtypeStruct((n, d), F32),
        grid=(n // tm, dff // tf),
        in_specs=in_specs,
        out_specs=pl.BlockSpec((tm, d), lambda i, c: (i, 0)),
        scratch_shapes=[pltpu.VMEM((tm, d), CD)],
        compiler_params=_cparams("parallel", "arbitrary"),
        name="mlp",
    )(*args)


def _t5_bucket_np(dist):
    max_exact = REL_BUCKETS // 2
    d = np.maximum(dist, 0)
    ratio = np.log(np.maximum(d, 1).astype(np.float32) / np.float32(max_exact)) / np.float32(
        math.log(REL_MAX_DIST / max_exact))
    large = max_exact + (ratio * np.float32(REL_BUCKETS - max_exact)).astype(np.int32)
    large = np.minimum(large, REL_BUCKETS - 1)
    return np.where(d < max_exact, d, large)


def _bias_tables(rel_bias):
    tbl = (rel_bias.astype(F32) * LOG2E).T.reshape(2, NSA_KV_HEADS, NSA_REP, REL_BUCKETS)
    tbl_nsa, tbl_swa = tbl[0], tbl[1]

    def lookup(tb, bucket):
        onehot = (jnp.asarray(bucket, jnp.int32)[None] == jnp.arange(REL_BUCKETS)[:, None, None]).astype(F32)
        return jnp.einsum('grb,bij->grij', tb, onehot, precision=lax.Precision.HIGHEST)
    i = np.arange(Q_BLOCK)[:, None]
    far_bucket = REL_BUCKETS - 1

    def banded(tb, window):
        jk = np.arange(window + Q_BLOCK)[None, :]
        dist = i + window - jk
        ok = (dist >= 0) & (dist < window)
        vals = lookup(tb, _t5_bucket_np(dist))
        vals = jnp.where(jnp.asarray(ok)[None, None], vals, NEG_INF)
        return vals.reshape(NSA_KV_HEADS, NSA_REP * Q_BLOCK, window + Q_BLOCK)

    bm_win = banded(tbl_nsa, NSA_WINDOW)
    bm_swa = banded(tbl_swa, SWA_WINDOW)

    far = tbl_nsa[:, :, far_bucket]
    jk = np.arange(2 * Q_BLOCK)[None, :]
    tiles = []
    for first_key_back in (0, Q_BLOCK):
        dist = i + first_key_back - jk
        delta = lookup(tbl_nsa, _t5_bucket_np(dist)) - far[:, :, None, None]
        tiles.append(jnp.where(jnp.asarray(dist >= 0)[None, None], delta, NEG_INF))
    near = jnp.stack(tiles, axis=1).reshape(NSA_KV_HEADS, 2, NSA_REP * Q_BLOCK, 2 * Q_BLOCK)
    far_rows = jnp.broadcast_to(far[:, :, None], (NSA_KV_HEADS, NSA_REP, Q_BLOCK)).reshape(NSA_KV_HEADS, -1)
    hi, lo_ = _split_parts(far_rows, 2)
    zeros = lambda k: jnp.zeros((NSA_KV_HEADS, NSA_REP * Q_BLOCK, k), CD)
    farq = jnp.concatenate([zeros(HEAD_DIM), hi[..., None], lo_[..., None], zeros(LANES - HEAD_DIM - 2)], axis=-1)

    u = np.arange(16)[None, :]
    dist_c = i - 16 * u + 113
    band = lookup(tbl_nsa, _t5_bucket_np(dist_c))
    band = jnp.where(jnp.asarray(dist_c >= 0)[None, None], band, MASK_NEG)
    band = band.reshape(NSA_KV_HEADS, NSA_REP * Q_BLOCK, 16)
    bh, bl = _split_parts(band, 2)
    neg = jnp.full((NSA_KV_HEADS, NSA_REP * Q_BLOCK, 1), MASK_NEG, CD)
    pq = jnp.concatenate([bh, bl, hi[..., None], lo_[..., None], neg, zeros(LANES - 35)], axis=-1)
    return bm_win, bm_swa, near, farq, pq


def _overlap_matrix(nc_pad, ns):
    ci = np.arange(nc_pad)[:, None]
    sj = np.arange(LANES)[None, :]
    ov = ((ci * NSA_CMP_STRIDE + NSA_CMP_LEN - 1 >= sj * NSA_SEL_LEN)
          & (ci * NSA_CMP_STRIDE <= sj * NSA_SEL_LEN + NSA_SEL_LEN - 1) & (sj < ns))
    return jnp.asarray(ov.astype(np.float32).T, CD)


def _layout_w_in(w):
    w = w.astype(F32)
    sc = HEAD_DIM ** -0.5 * LOG2E
    z = lambda k: jnp.zeros((w.shape[0], k), F32)
    main = jnp.concatenate([w[:, 544:1056] * sc, w[:, 1056:2080], w[:, 2088:2600] * sc, w[:, 3392:3904] * sc,
                            w[:, 2600:3368], w[:, 3904:4160]], axis=1)
    kr = w[:, 512:544]
    blk_a = jnp.concatenate([w[:, 2080:2088], w[:, 3368:3392], z(32), kr, z(32)], axis=1)
    blk_b = jnp.concatenate([z(64), -kr[:, 16:], kr[:, :16], z(32)], axis=1)
    small = jnp.concatenate([w[:, 0:512], blk_a, blk_b], axis=1)
    return main.astype(CD), small.astype(CD)


_FOX_Q, _FOX_K512, _FOX_V = 0, 1, 8
_NSA_Q256, _SWA_Q256 = 6, 8
_KVC_COL = 2560
_KSEL, _VSEL, _KWIN, _VWIN, _KSWA, _VSWA = 22, 23, 24, 25, 26, 27


def _layout_mla(w_uq, w_ukv):
    w3 = w_uq.astype(F32).reshape(MLA_Q_RANK, GROUP_HEADS, MLA_NOPE + MLA_ROPE)
    nope, rp = w3[:, :, :MLA_NOPE], w3[:, :, MLA_NOPE:]
    half = MLA_ROPE // 2
    sw = jnp.concatenate([-rp[:, :, half:], rp[:, :, :half]], axis=-1)
    z = lambda k: jnp.zeros((MLA_Q_RANK, GROUP_HEADS, k), F32)
    plain = jnp.concatenate([nope, rp, z(32)], axis=-1).reshape(MLA_Q_RANK, -1)
    swapped = jnp.concatenate([z(64), sw, z(32)], axis=-1).reshape(MLA_Q_RANK, -1)
    wq = jnp.concatenate([plain, swapped], axis=1)
    k3 = w_ukv.astype(F32).reshape(MLA_KV_RANK, GROUP_HEADS, 2 * HEAD_DIM)
    wk = jnp.concatenate([k3[:, :, :MLA_NOPE], jnp.zeros((MLA_KV_RANK, GROUP_HEADS, HEAD_DIM), F32)],
                         axis=-1).reshape(MLA_KV_RANK, -1)
    wv = k3[:, :, MLA_NOPE:].reshape(MLA_KV_RANK, -1)
    return wq.astype(CD), wk.astype(CD), wv.astype(CD)


def _rope_tables(t):
    inv = ROPE_THETA ** (-jnp.arange(0, MLA_ROPE, 2, dtype=F32) / MLA_ROPE)
    ang = jnp.arange(t).astype(F32)[:, None] * inv[None, :]
    cc = jnp.concatenate([jnp.cos(ang)] * 2, axis=1)
    ss = jnp.concatenate([jnp.sin(ang)] * 2, axis=1)
    one, z64, z32 = jnp.ones((t, 64), F32), jnp.zeros((t, 64), F32), jnp.zeros((t, 32), F32)
    return (jnp.concatenate([one, cc, z32], axis=1), jnp.concatenate([z64, ss, z32], axis=1),
            jnp.concatenate([z64, cc, z32], axis=1))


def kernel(x, norm_attn, w_in, mla_q_norm, mla_w_uq, mla_kv_norm, mla_w_ukv, fox_b_f, nsa_cmp_pos,
           nsa_cmp_w1, nsa_cmp_w2, swa_sinks, group_norm, w_out, norm_mlp, w_up, w_down, rel_bias,
           final_norm):
    b, t, d = x.shape
    n = b * t
    depth = w_in.shape[0]
    assert t % 1024 == 0 and d == N_GROUPS * GROUP_WIDTH
    tl = _TILES
    nr = t // NSA_CMP_STRIDE

    cosq, sinq, cosk = _rope_tables(t)
    bm_win, bm_swa, nearbias, farq, pq = _bias_tables(rel_bias)
    overlap = _overlap_matrix(nr, t // NSA_SEL_LEN)

    h = x.reshape(n, d).astype(F32)
    for l in range(depth):
        w_main, w_small = _layout_w_in(w_in[l])
        main = _norm_matmul(h, norm_attn[l], w_main, CD, tl.proj_rows, tl.proj_cols)
        small = _norm_matmul(h, norm_attn[l], w_small, F32, tl.proj_rows, w_small.shape[1])
        main3 = main.reshape(b, t, -1)
        small3 = small.reshape(b, t, -1)

        wq, wk, wv = _layout_mla(mla_w_uq[l], mla_w_ukv[l])
        q_m, k_m, v_m = _mla_prep(small, mla_q_norm[l], mla_kv_norm[l], wq, wk, wv, cosq, sinq, cosk, t,
                                  tl.prep_rows)
        o_mla = _dense_attn(q_m.reshape(b, t, -1), k_m.reshape(b, t, -1), v_m.reshape(b, t, -1), None,
                            fox=False, q_col=0, v_col=0, tq=tl.sweep)

        k_aug, fq = _fox_prep(small3, fox_b_f[l], main3, _FOX_K512, tl.prep_rows)
        o_fox = _dense_attn(main3, k_aug, main3, fq, fox=True, q_col=_FOX_Q, v_col=_FOX_V, tq=tl.sweep)

        kv16 = main3[:, :, _KVC_COL:_KVC_COL + 256].reshape(b, nr, NSA_CMP_STRIDE * 256)
        cmp = _nsa_compress(kv16, nsa_cmp_pos[l], nsa_cmp_w1[l], nsa_cmp_w2[l])
        o_cmp, maskfeat = _cmp_select(main3, _NSA_Q256, cmp, pq, overlap, tl.cmp_blocks)
        o_win = _banded_attn(main3, _NSA_Q256, _KWIN, _VWIN, bm_win, None, NSA_WINDOW, tl.band_blocks)
        o_sel = _sel_attn(main3, _NSA_Q256, _KSEL, _VSEL, maskfeat, farq, nearbias, tl.sweep)

        sink = (swa_sinks[l].astype(F32) * LOG2E).reshape(NSA_KV_HEADS, NSA_REP, 1, 1)
        sink_rows = jnp.concatenate(
            [jnp.broadcast_to(sink, (NSA_KV_HEADS, NSA_REP, Q_BLOCK, 1)),
             jnp.full((NSA_KV_HEADS, NSA_REP, Q_BLOCK, LANES - 1), NEG_INF, F32)],
            axis=-1).reshape(NSA_KV_HEADS, NSA_REP * Q_BLOCK, LANES)
        o_swa = _banded_attn(main3, _SWA_Q256, _KSWA, _VSWA, bm_swa, sink_rows, SWA_WINDOW, tl.band_blocks)

        parts = [o.reshape(n, GROUP_WIDTH) for o in (o_mla, o_fox, o_cmp, o_sel, o_win, o_swa)]
        h = _outproj(*parts, small, group_norm[l], w_out[l].astype(CD), h, tl.out_rows, d)
        gf = final_norm if l == depth - 1 else None
        h = _mlp(h, norm_mlp[l], w_up[l].astype(CD), w_down[l].astype(CD), gf, tl.mlp_rows, tl.mlp_cols)
    return h.reshape(b, t, d).astype(x.dtype)
```

```python
import functools
import math
from typing import NamedTuple

import numpy as np
import jax
import jax.numpy as jnp
from jax import lax
from jax.experimental import pallas as pl
from jax.experimental.pallas import tpu as pltpu

HEAD_DIM = 64
GROUP_HEADS = 8
GROUP_WIDTH = GROUP_HEADS * HEAD_DIM
N_GROUPS = 4
Q_BLOCK = 128
EPS = 1e-6
NEG_INF = -1e30

MLA_Q_RANK = 384
MLA_KV_RANK = 128
MLA_NOPE = 64
MLA_ROPE = 32
ROPE_THETA = 10000.0

NSA_KV_HEADS = 2
NSA_REP = GROUP_HEADS // NSA_KV_HEADS
NSA_CMP_LEN = 32
NSA_CMP_STRIDE = 16
NSA_SEL_LEN = 64
NSA_TOP_N = 8
NSA_WINDOW = 256
SWA_WINDOW = 128

REL_BUCKETS = 32
REL_MAX_DIST = 128

V7X_VMEM_BYTES = 64 * 2**20
VMEM_LIMIT = (V7X_VMEM_BYTES * 7) // 8
LANES = 128
CD = jnp.bfloat16
F32 = jnp.float32
MASK_NEG = -(2.0 ** 80)
M_INIT = -(2.0 ** 100)
TILE_GROUP = 8


class _Tiles(NamedTuple):
    proj_rows: int = 1024
    proj_cols: int = 896
    prep_rows: int = 512
    sweep: int = 512
    cmp_blocks: int = 4
    band_blocks: int = 8
    out_rows: int = 512
    mlp_rows: int = 512
    mlp_cols: int = 1024


_TILES = _Tiles()
LOG2E = math.log2(math.e)
MLA_SCORE_SCALE = (MLA_NOPE + MLA_ROPE) ** -0.5 * LOG2E
_TRANS_B = (((1,), (1,)), ((), ()))


def _cparams(*sem):
    return pltpu.CompilerParams(dimension_semantics=sem, vmem_limit_bytes=VMEM_LIMIT)


def _split_parts(x, n):
    parts, r = [], x
    for _ in range(n):
        p = r.astype(CD)
        parts.append(p)
        r = r - p.astype(F32)
    return parts


def _sweep_tiles(fn, n_tiles):
    groups = n_tiles // TILE_GROUP

    def grouped(i, _):
        fn(i * TILE_GROUP, TILE_GROUP)
        return 0

    lax.fori_loop(0, groups, grouped, 0)
    base = groups * TILE_GROUP
    rem = n_tiles - base
    piece = TILE_GROUP // 2
    while piece >= 1:
        @pl.when((rem & piece) != 0)
        def _(piece=piece):
            fn(base + (rem & ~(2 * piece - 1)), piece)
        piece //= 2


def _rms(x, g):
    return x * lax.rsqrt(jnp.mean(x * x, axis=-1, keepdims=True) + EPS) * g


def _norm_matmul_kernel(x_ref, g_ref, w_ref, o_ref, u_ref):
    @pl.when(pl.program_id(1) == 0)
    def _():
        u_ref[...] = _rms(x_ref[...], g_ref[...]).astype(u_ref.dtype)

    o_ref[...] = jnp.dot(u_ref[...], w_ref[...], preferred_element_type=F32).astype(o_ref.dtype)


def _norm_matmul(x, g, w, out_dtype, tm, tn):
    n, d = x.shape
    nc = w.shape[1]
    return pl.pallas_call(
        _norm_matmul_kernel,
        out_shape=jax.ShapeDtypeStruct((n, nc), out_dtype),
        grid=(n // tm, nc // tn),
        in_specs=[
            pl.BlockSpec((tm, d), lambda i, j: (i, 0)),
            pl.BlockSpec((1, d), lambda i, j: (0, 0)),
            pl.BlockSpec((d, tn), lambda i, j: (0, j)),
        ],
        out_specs=pl.BlockSpec((tm, tn), lambda i, j: (i, j)),
        scratch_shapes=[pltpu.VMEM((tm, d), CD)],
        compiler_params=_cparams("parallel", "arbitrary"),
        name="norm_matmul",
    )(x, g.reshape(1, d), w)


def _mla_prep_kernel(sm_ref, qn_ref, kvn_ref, wq_ref, wk_ref, wv_ref, cq_ref, sq_ref, ck_ref,
                     q_ref, k_ref, v_ref):
    sm = sm_ref[...]
    nq = _rms(sm[:, :MLA_Q_RANK], qn_ref[...]).astype(CD)
    nkv = _rms(sm[:, MLA_Q_RANK:MLA_Q_RANK + MLA_KV_RANK], kvn_ref[...]).astype(CD)
    blk_a = sm[:, 512:640]
    blk_b = sm[:, 640:768]
    cosq, sinq, cosk = cq_ref[...], sq_ref[...], ck_ref[...]
    hw = GROUP_HEADS * LANES
    qq = jnp.dot(nq, wq_ref[...], preferred_element_type=F32)
    kk = jnp.dot(nkv, wk_ref[...], preferred_element_type=F32)
    kpe = blk_a * cosk + blk_b * sinq
    for h in range(GROUP_HEADS):
        sl = slice(h * LANES, (h + 1) * LANES)
        qh = qq[:, sl] * cosq + qq[:, hw + h * LANES:hw + (h + 1) * LANES] * sinq
        q_ref[:, sl] = (qh * MLA_SCORE_SCALE).astype(q_ref.dtype)
        k_ref[:, sl] = (kk[:, sl] + kpe).astype(k_ref.dtype)
    v_ref[...] = jnp.dot(nkv, wv_ref[...], preferred_element_type=F32).astype(v_ref.dtype)


def _mla_prep(small, qn, kvn, wq, wk, wv, cosq, sinq, cosk, t, tm):
    n = small.shape[0]
    tb = t // tm
    hw = GROUP_HEADS * LANES
    full = lambda a: pl.BlockSpec(a.shape, lambda i: (0,) * a.ndim)
    tab = pl.BlockSpec((tm, LANES), lambda i: (i % tb, 0))
    qn = qn.reshape(1, -1)
    kvn = kvn.reshape(1, -1)
    return pl.pallas_call(
        _mla_prep_kernel,
        out_shape=(jax.ShapeDtypeStruct((n, hw), CD), jax.ShapeDtypeStruct((n, hw), CD),
                   jax.ShapeDtypeStruct((n, GROUP_WIDTH), CD)),
        grid=(n // tm,),
        in_specs=[pl.BlockSpec((tm, small.shape[1]), lambda i: (i, 0)), full(qn), full(kvn),
                  full(wq), full(wk), full(wv), tab, tab, tab],
        out_specs=(pl.BlockSpec((tm, hw), lambda i: (i, 0)), pl.BlockSpec((tm, hw), lambda i: (i, 0)),
                   pl.BlockSpec((tm, GROUP_WIDTH), lambda i: (i, 0))),
        compiler_params=_cparams("parallel"),
        name="mla_prep",
    )(small, qn, kvn, wq, wk, wv, cosq, sinq, cosk)


def _fox_prep_kernel(fl_ref, bf_ref, k_ref, efq_ref, ekf_ref, kaug_ref, fq_ref, carry_ref, *, tc):
    @pl.when(pl.program_id(1) == 0)
    def _():
        carry_ref[...] = jnp.zeros_like(carry_ref)

    x = fl_ref[...] + bf_ref[...]
    logf = -(jnp.maximum(-x, 0.0) + jnp.log1p(jnp.exp(-jnp.abs(x))))
    lane = lax.broadcasted_iota(jnp.int32, logf.shape, 1)
    logf = jnp.where(lane < GROUP_HEADS, logf, 0.0)
    row = lax.broadcasted_iota(jnp.int32, (tc, tc), 0)
    col = lax.broadcasted_iota(jnp.int32, (tc, tc), 1)
    tri = jnp.where(row >= col, 1.0, 0.0).astype(CD)
    cs = jnp.zeros(logf.shape, F32)
    for part in _split_parts(logf, 3):
        cs = cs + jnp.dot(tri, part, preferred_element_type=F32)
    fc = cs + carry_ref[...]
    carry_ref[...] = fc[tc - 1:tc, :]
    parts = _split_parts(fc * LOG2E, 3)
    fq = jnp.zeros(fq_ref.shape, F32)
    kf = jnp.zeros(fq_ref.shape, F32)
    for i, part in enumerate(parts):
        fq = fq + jnp.dot(part, efq_ref[...], preferred_element_type=F32)
        kf = kf + jnp.dot(part, ekf_ref[i], preferred_element_type=F32)
    fq_ref[...] = fq
    k = k_ref[...]
    for p in range(GROUP_HEADS // 2):
        kaug_ref[:, 2 * p * LANES:(2 * p + 1) * LANES] = k[:, p * LANES:(p + 1) * LANES]
        kaug_ref[:, (2 * p + 1) * LANES:(2 * p + 2) * LANES] = kf[:, p * LANES:(p + 1) * LANES].astype(kaug_ref.dtype)


def _fox_prep(small3, b_f, main3, k_col, tc):
    b, t, _ = small3.shape
    pairs = GROUP_HEADS // 2
    efq = np.zeros((LANES, pairs * LANES), np.float32)
    ekf = np.zeros((3, LANES, pairs * LANES), np.float32)
    for h in range(GROUP_HEADS):
        p, a = divmod(h, 2)
        efq[h, p * LANES + a] = 1.0
        for i in range(3):
            ekf[i, h, p * LANES + 3 * a + i] = -1.0
    bf = jnp.zeros((1, LANES), F32).at[0, :GROUP_HEADS].set(b_f.astype(F32))
    return pl.pallas_call(
        functools.partial(_fox_prep_kernel, tc=tc),
        out_shape=(jax.ShapeDtypeStruct((b, t, 2 * GROUP_WIDTH), CD),
                   jax.ShapeDtypeStruct((b, t, GROUP_WIDTH), F32)),
        grid=(b, t // tc),
        in_specs=[pl.BlockSpec((None, tc, LANES), lambda i, j: (i, j, 4)),
                  pl.BlockSpec((1, LANES), lambda i, j: (0, 0)),
                  pl.BlockSpec((None, tc, GROUP_WIDTH), lambda i, j: (i, j, k_col)),
                  pl.BlockSpec(efq.shape, lambda i, j: (0, 0)),
                  pl.BlockSpec(ekf.shape, lambda i, j: (0, 0, 0))],
        out_specs=(pl.BlockSpec((None, tc, 2 * GROUP_WIDTH), lambda i, j: (i, j, 0)),
                   pl.BlockSpec((None, tc, GROUP_WIDTH), lambda i, j: (i, j, 0))),
        scratch_shapes=[pltpu.VMEM((1, LANES), F32)],
        compiler_params=_cparams("arbitrary", "arbitrary"),
        name="fox_prep",
    )(small3, bf, main3, jnp.asarray(efq, CD), jnp.asarray(ekf, CD))


def _dense_attn_kernel(*refs, fox, tq):
    if fox:
        q_ref, k_ref, v_ref, fq_ref, o_ref, va_ref, qa_ref, s_ref, mx_ref, acc_ref = refs
    else:
        q_ref, k_ref, v_ref, o_ref, va_ref, qa_ref, s_ref, mx_ref, acc_ref = refs
    qi = pl.program_id(2)
    t = k_ref.shape[0]
    nch = tq // LANES

    @pl.when(qi == 0)
    def _():
        chunk = 1024

        def body(i, _):
            off = pl.multiple_of(i * chunk, chunk)
            va_ref[pl.ds(off, chunk), 0:LANES] = v_ref[pl.ds(off, chunk), :]
            va_ref[pl.ds(off, chunk), LANES:2 * LANES] = jnp.ones((chunk, LANES), va_ref.dtype)
            return 0

        lax.fori_loop(0, t // chunk, body, 0)

    half = tq // 2
    outs = []
    for h in range(2):
        q = q_ref[...]
        if fox:
            qa_ref[...] = jnp.concatenate(
                [q * _lane_mask(tq, h * HEAD_DIM, (h + 1) * HEAD_DIM, q.dtype), _lane_mask(tq, 3 * h, 3 * h + 3, q.dtype)],
                axis=1)
        else:
            qa_ref[...] = q[:, h * LANES:(h + 1) * LANES]

        def score_tiles(j0, count, h=h):
            mx = mx_ref[...]
            for gi in range(count):
                off = pl.multiple_of((j0 + gi) * tq, tq)
                for hf in range(2):
                    rows = pl.ds(off + hf * half, half)
                    k = k_ref[rows, :] if fox else k_ref[rows, pl.ds(h * LANES, LANES)]
                    s = lax.dot_general(qa_ref[...], k, _TRANS_B, preferred_element_type=F32)
                    delta = (hf * half + lax.broadcasted_iota(jnp.int32, (tq, half), 1)
                             - lax.broadcasted_iota(jnp.int32, (tq, half), 0))
                    s = jnp.where(delta <= (qi - j0 - gi) * tq, s, NEG_INF)
                    s_ref[j0 + gi, :, hf * half:(hf + 1) * half] = s
                    for i in range(half // LANES):
                        mx = jnp.maximum(mx, s[:, i * LANES:(i + 1) * LANES])
            mx_ref[...] = mx

        def value_tiles(j0, count):
            mc = mx_ref[...]
            ps = []
            for gi in range(count):
                s = s_ref[j0 + gi]
                ps += [jnp.exp2(s[:, i * LANES:(i + 1) * LANES] - mc) for i in range(nch)]
            off = pl.multiple_of(j0 * tq, tq)
            acc_ref[...] += jnp.dot(jnp.concatenate(ps, axis=1).astype(CD), va_ref[pl.ds(off, count * tq), :],
                                    preferred_element_type=F32)

        mx_ref[...] = jnp.full(mx_ref.shape, M_INIT, F32)
        _sweep_tiles(score_tiles, qi + 1)
        m = jnp.max(mx_ref[...], axis=1, keepdims=True)
        if fox:
            f = fq_ref[:, h:h + 1]
            m = (m + f) - f
        mx_ref[...] = jnp.broadcast_to(m, (tq, LANES))

        acc_ref[...] = jnp.zeros(acc_ref.shape, F32)
        _sweep_tiles(value_tiles, qi + 1)
        acc = acc_ref[...]
        outs.append(acc[:, :LANES] * (1.0 / acc[:, LANES:LANES + 1]))
    lane = lax.broadcasted_iota(jnp.int32, (tq, LANES), 1)
    o_ref[...] = jnp.where(lane < HEAD_DIM, outs[0], outs[1]).astype(o_ref.dtype)


def _dense_attn(q3, k3, v3, fq3, *, fox, q_col, v_col, tq):
    b, t, _ = q3.shape
    assert t % tq == 0 and t % 1024 == 0
    pairs = GROUP_HEADS // 2
    qw = LANES if fox else 2 * LANES
    kd = 2 * LANES if fox else LANES
    in_specs = [pl.BlockSpec((None, tq, qw), lambda i, p, j: (i, j, q_col + p)),
                pl.BlockSpec((None, t, 2 * LANES), lambda i, p, j: (i, 0, p)),
                pl.BlockSpec((None, t, LANES), lambda i, p, j: (i, 0, v_col + p))]
    args = [q3, k3, v3]
    if fox:
        in_specs.append(pl.BlockSpec((None, tq, LANES), lambda i, p, j: (i, j, p)))
        args.append(fq3)
    return pl.pallas_call(
        functools.partial(_dense_attn_kernel, fox=fox, tq=tq),
        out_shape=jax.ShapeDtypeStruct((b, t, GROUP_WIDTH), F32),
        grid=(b, pairs, t // tq),
        in_specs=in_specs,
        out_specs=pl.BlockSpec((None, tq, LANES), lambda i, p, j: (i, j, p)),
        scratch_shapes=[pltpu.VMEM((t, 2 * LANES), CD), pltpu.VMEM((tq, kd), CD),
                        pltpu.VMEM((t // tq, tq, tq), F32), pltpu.VMEM((tq, LANES), F32),
                        pltpu.VMEM((tq, 2 * LANES), F32)],
        compiler_params=_cparams("arbitrary", "arbitrary", "arbitrary"),
        name="fox_attn" if fox else "mla_attn",
    )(*args)


def _gelu_tanh(x):
    return 0.5 * x * (1.0 + jnp.tanh(math.sqrt(2.0 / math.pi) * (x + 0.044715 * (x * x * x))))


def _compress_kernel(r_ref, w1a_ref, w1b_ref, pa_ref, pb_ref, w2_ref, o_ref):
    r = r_ref[...]
    nr = r.shape[0]
    ya = jnp.dot(r, w1a_ref[...], preferred_element_type=F32)
    yb = jnp.dot(r, w1b_ref[...], preferred_element_type=F32)
    pc = (jnp.dot(pa_ref[...], w1a_ref[...], preferred_element_type=F32)
          + jnp.dot(pb_ref[...], w1b_ref[...], preferred_element_type=F32))[0:1, :]
    pre = ya + pltpu.roll(yb, nr - 1, 0) + pc
    h1 = _gelu_tanh(pre).astype(CD)
    rowi = lax.broadcasted_iota(jnp.int32, (nr, LANES), 0)
    for g in range(NSA_KV_HEADS):
        y = jnp.dot(h1, w2_ref[g], preferred_element_type=F32)
        o_ref[g] = jnp.where(rowi < nr - 1, y, 0.0).astype(o_ref.dtype)


def _nsa_compress(kv16, cmp_pos, cmp_w1, cmp_w2):
    b, nr, kw = kv16.shape
    half = NSA_CMP_LEN // 2
    eye2 = jnp.eye(2, dtype=F32)
    w1r = cmp_w1.astype(F32).reshape(2, NSA_CMP_LEN, HEAD_DIM, HEAD_DIM)
    expand = lambda w: jnp.einsum('klde,kw,gh->klwgdhe', w, eye2, eye2).reshape(2, kw, LANES).astype(CD)
    w1a, w1b = expand(w1r[:, :half]), expand(w1r[:, half:])
    posr = cmp_pos.astype(F32)
    tile = lambda p: jnp.broadcast_to(p[:, None, :, None, None, :], (2, 8, half, 2, 2, HEAD_DIM)).reshape(2, 8, kw).astype(CD)
    pa, pb = tile(posr[:, :half]), tile(posr[:, half:])
    w2 = cmp_w2.astype(F32)
    w2d = jnp.zeros((2, NSA_KV_HEADS, LANES, LANES), F32)
    for g in range(NSA_KV_HEADS):
        blk = jnp.concatenate([w2, w2], axis=2)
        w2d = w2d.at[:, g, g * HEAD_DIM:(g + 1) * HEAD_DIM, :].set(blk)
    w2d = w2d.astype(CD)
    return pl.pallas_call(
        _compress_kernel,
        out_shape=jax.ShapeDtypeStruct((b, 2, NSA_KV_HEADS, nr, LANES), CD),
        grid=(b, 2),
        in_specs=[pl.BlockSpec((None, nr, kw), lambda i, k: (i, 0, 0)),
                  pl.BlockSpec((None, kw, LANES), lambda i, k: (k, 0, 0)),
                  pl.BlockSpec((None, kw, LANES), lambda i, k: (k, 0, 0)),
                  pl.BlockSpec((None, 8, kw), lambda i, k: (k, 0, 0)),
                  pl.BlockSpec((None, 8, kw), lambda i, k: (k, 0, 0)),
                  pl.BlockSpec((None, NSA_KV_HEADS, LANES, LANES), lambda i, k: (k, 0, 0, 0))],
        out_specs=pl.BlockSpec((None, None, NSA_KV_HEADS, nr, LANES), lambda i, k: (i, k, 0, 0, 0)),
        compiler_params=_cparams("parallel", "arbitrary"),
        name="nsa_compress",
    )(kv16, w1a, w1b, pa, pb, w2d)


def _lane_mask(rows, lo, hi, dtype):
    lane = lax.broadcasted_iota(jnp.int32, (rows, LANES), 1)
    return jnp.where((lane >= lo) & (lane < hi), 1.0, 0.0).astype(dtype)


def _stack_heads_native(q):
    lo = _lane_mask(Q_BLOCK, 0, HEAD_DIM, q.dtype)
    hi = _lane_mask(Q_BLOCK, HEAD_DIM, LANES, q.dtype)
    c0, c1 = q[:, :LANES], q[:, LANES:]
    return jnp.concatenate([c0 * lo, c0 * hi, c1 * lo, c1 * hi], axis=0)


def _unstack_heads(o):
    lane = lax.broadcasted_iota(jnp.int32, (Q_BLOCK, LANES), 1)
    lo = lane < HEAD_DIM
    return jnp.concatenate([jnp.where(lo, o[0:128], o[128:256]), jnp.where(lo, o[256:384], o[384:512])], axis=1)


def _dup_matrix(g):
    i = lax.broadcasted_iota(jnp.int32, (LANES, LANES), 0)
    j = lax.broadcasted_iota(jnp.int32, (LANES, LANES), 1)
    return jnp.where(i == g * HEAD_DIM + (j & (HEAD_DIM - 1)), 1.0, 0.0).astype(CD)


def _fill_rows(dst_ref, row0, src_ref, mat, chunk=1024):
    n = src_ref.shape[0]

    def body(i, _):
        off = pl.multiple_of(i * chunk, chunk)
        dst_ref[pl.ds(row0 + off, chunk), 0:LANES] = jnp.dot(
            src_ref[pl.ds(off, chunk), :], mat, preferred_element_type=F32).astype(dst_ref.dtype)
        return 0

    lax.fori_loop(0, n // chunk, body, 0)


def _cmp_select_kernel(q_ref, kc_ref, vc_ref, pq_ref, ov_ref, o_ref, mf_ref, *, nq):
    nc = kc_ref.shape[0]
    ci = lax.broadcasted_iota(jnp.int32, (nc, LANES), 0)
    f = lax.broadcasted_iota(jnp.int32, (nc, LANES), 1)
    j = lax.broadcasted_iota(jnp.int32, (LANES, Q_BLOCK), 1)
    i = lax.broadcasted_iota(jnp.int32, (LANES, Q_BLOCK), 0)
    bf = i.astype(F32)
    one_if = lambda cond: jnp.where(cond, 1.0, 0.0)
    for u in range(nq):
        n = pl.program_id(2) * nq + u
        rs = slice(u * Q_BLOCK, (u + 1) * Q_BLOCK)
        qaug = jnp.concatenate([_stack_heads_native(q_ref[rs, :]), pq_ref[...]], axis=1)
        uc = ci - 8 * n + 9
        feat = jnp.where(f < 32, one_if(uc == (f & 15)),
                         jnp.where(f < 34, one_if(uc < 0), one_if((f == 34) & (uc > 15)))).astype(CD)
        kaug = jnp.concatenate([kc_ref[...], feat], axis=1)
        s = lax.dot_general(qaug, kaug, _TRANS_B, preferred_element_type=F32)
        m = jnp.max(s, axis=1, keepdims=True)
        p = jnp.exp2(s - m)
        l = jnp.sum(p, axis=1, keepdims=True)
        pc = p * jnp.where(m > 0.5 * MASK_NEG, 1.0 / l, 0.0)
        o = jnp.dot(pc.astype(CD), vc_ref[...], preferred_element_type=F32)
        o_ref[rs, :] = _unstack_heads(o)
        pcs = pc[0:128] + pc[128:256] + pc[256:384] + pc[384:512]
        imp = lax.dot_general(ov_ref[...], pcs.astype(CD), _TRANS_B, preferred_element_type=F32)
        cur = 2 * n + jnp.where(j >= NSA_SEL_LEN, 1, 0)
        causal = i <= cur
        forced = (i == 0) | (i == cur) | (i == cur - 1)
        sel = one_if(forced & causal)
        score = jnp.where(causal & jnp.logical_not(forced), imp, -3.0e38)
        for _ in range(NSA_TOP_N - 3):
            mx = jnp.max(score, axis=0, keepdims=True)
            idx = jnp.min(jnp.where(score == mx, bf, float(LANES)), axis=0, keepdims=True)
            pick = bf == idx
            sel = jnp.where(pick, 1.0, sel)
            score = jnp.where(pick, -3.0e38, score)
        mf_ref[rs, :] = jnp.where(sel > 0.5, 0.0, MASK_NEG).T.astype(mf_ref.dtype)


def _cmp_select(main3, q_col, cmp, pq, overlap, nq):
    b, t, _ = main3.shape
    nc = cmp.shape[3]
    tq = nq * Q_BLOCK
    assert t % tq == 0
    return pl.pallas_call(
        functools.partial(_cmp_select_kernel, nq=nq),
        out_shape=(jax.ShapeDtypeStruct((b, t, GROUP_WIDTH), F32),
                   jax.ShapeDtypeStruct((b, t, NSA_KV_HEADS * LANES), CD)),
        grid=(b, NSA_KV_HEADS, t // tq),
        in_specs=[pl.BlockSpec((None, tq, 2 * LANES), lambda i, g, n: (i, n, q_col + g)),
                  pl.BlockSpec((None, None, None, nc, LANES), lambda i, g, n: (i, 0, g, 0, 0)),
                  pl.BlockSpec((None, None, None, nc, LANES), lambda i, g, n: (i, 1, g, 0, 0)),
                  pl.BlockSpec((None, NSA_REP * Q_BLOCK, LANES), lambda i, g, n: (g, 0, 0)),
                  pl.BlockSpec(overlap.shape, lambda i, g, n: (0, 0))],
        out_specs=(pl.BlockSpec((None, tq, 2 * LANES), lambda i, g, n: (i, n, g)),
                   pl.BlockSpec((None, tq, LANES), lambda i, g, n: (i, n, g))),
        compiler_params=_cparams("parallel", "parallel", "arbitrary"),
        name="nsa_cmp_select",
    )(main3, cmp, cmp, pq, overlap)


def _banded_kernel(*refs, window, sinks, nq):
    if sinks:
        q_ref, k_ref, v_ref, bm_ref, sk_ref, o_ref, kp_ref, vp_ref = refs
    else:
        q_ref, k_ref, v_ref, bm_ref, o_ref, kp_ref, vp_ref = refs
    g = pl.program_id(1)
    kw = window + Q_BLOCK

    @pl.when(pl.program_id(2) == 0)
    def _():
        dup = _dup_matrix(g)
        kp_ref[0:window, :] = jnp.zeros((window, LANES), kp_ref.dtype)
        vp_ref[0:window, :] = jnp.zeros((window, LANES), vp_ref.dtype)
        _fill_rows(kp_ref, window, k_ref, dup)
        _fill_rows(vp_ref, window, v_ref, dup)

    for u in range(nq):
        rs = slice(u * Q_BLOCK, (u + 1) * Q_BLOCK)
        start = pl.multiple_of((pl.program_id(2) * nq + u) * Q_BLOCK, Q_BLOCK)
        ks = kp_ref[pl.ds(start, kw), :]
        vs = vp_ref[pl.ds(start, kw), :]
        s = lax.dot_general(_stack_heads_native(q_ref[rs, :]), ks, _TRANS_B, preferred_element_type=F32)
        s = s + bm_ref[...]
        kpos = start - window + lax.broadcasted_iota(jnp.int32, s.shape, 1)
        s = jnp.where(kpos >= 0, s, NEG_INF)
        if sinks:
            s = jnp.concatenate([s, sk_ref[...]], axis=1)
        m = jnp.max(s, axis=1, keepdims=True)
        p = jnp.exp2(s - m)
        l = jnp.sum(p, axis=1, keepdims=True)
        o = jnp.dot(p[:, :kw].astype(CD), vs, preferred_element_type=F32) * (1.0 / l)
        o_ref[rs, :] = _unstack_heads(o)


def _banded_attn(main3, q_col, k_col, v_col, biasmask, sink_rows, window, nq):
    b, t, _ = main3.shape
    kw = window + Q_BLOCK
    tq = nq * Q_BLOCK
    assert t % tq == 0
    sinks = sink_rows is not None
    in_specs = [pl.BlockSpec((None, tq, 2 * LANES), lambda i, g, n: (i, n, q_col + g)),
                pl.BlockSpec((None, t, LANES), lambda i, g, n: (i, 0, k_col)),
                pl.BlockSpec((None, t, LANES), lambda i, g, n: (i, 0, v_col)),
                pl.BlockSpec((None, NSA_REP * Q_BLOCK, kw), lambda i, g, n: (g, 0, 0))]
    args = [main3, main3, main3, biasmask]
    if sinks:
        in_specs.append(pl.BlockSpec((None, NSA_REP * Q_BLOCK, LANES), lambda i, g, n: (g, 0, 0)))
        args.append(sink_rows)
    return pl.pallas_call(
        functools.partial(_banded_kernel, window=window, sinks=sinks, nq=nq),
        out_shape=jax.ShapeDtypeStruct((b, t, GROUP_WIDTH), F32),
        grid=(b, NSA_KV_HEADS, t // tq),
        in_specs=in_specs,
        out_specs=pl.BlockSpec((None, tq, 2 * LANES), lambda i, g, n: (i, n, g)),
        scratch_shapes=[pltpu.VMEM((window + t, LANES), CD), pltpu.VMEM((window + t, LANES), CD)],
        compiler_params=_cparams("arbitrary", "arbitrary", "arbitrary"),
        name="swa_attn" if sinks else "nsa_win_attn",
    )(*args)


def _sel_kernel(q_ref, k_ref, v_ref, mf_ref, fq_ref, nb_ref, o_ref,
                ka_ref, vd_ref, qa_ref, s_ref, sn_ref, mx_ref, acc_ref):
    g = pl.program_id(1)
    n = pl.program_id(2)
    t = k_ref.shape[0]

    @pl.when(n == 0)
    def _():
        i = lax.broadcasted_iota(jnp.int32, (LANES, LANES), 0)
        j = lax.broadcasted_iota(jnp.int32, (LANES, LANES), 1)
        pick = jnp.where((j < HEAD_DIM) & (i == g * HEAD_DIM + j), 1.0, 0.0).astype(CD)
        chunk = 1024

        def body(c, _):
            off = pl.multiple_of(c * chunk, chunk)
            kk = jnp.dot(k_ref[pl.ds(off, chunk), :], pick, preferred_element_type=F32)
            ln = lax.broadcasted_iota(jnp.int32, (chunk, LANES), 1)
            kk = jnp.where((ln == HEAD_DIM) | (ln == HEAD_DIM + 1), 1.0, kk)
            ka_ref[pl.ds(off, chunk), 0:LANES] = kk.astype(ka_ref.dtype)
            key = off + lax.broadcasted_iota(jnp.int32, (chunk, LANES), 0)
            ka_ref[pl.ds(off, chunk), LANES:2 * LANES] = jnp.where(
                (key >> 6) == ln, 1.0, 0.0).astype(ka_ref.dtype)
            vd_ref[pl.ds(off, chunk), LANES:2 * LANES] = jnp.ones((chunk, LANES), vd_ref.dtype)
            return 0

        lax.fori_loop(0, t // chunk, body, 0)
        _fill_rows(vd_ref, 0, v_ref, _dup_matrix(g))

    q = q_ref[...]
    ii = lax.broadcasted_iota(jnp.int32, (LANES, LANES), 0)
    jj = lax.broadcasted_iota(jnp.int32, (LANES, LANES), 1)
    shift = jnp.where(ii == jj + HEAD_DIM, 1.0, 0.0).astype(CD)
    lom = _lane_mask(Q_BLOCK, 0, HEAD_DIM, q.dtype)
    c0, c1 = q[:, :LANES], q[:, LANES:]
    q0 = jnp.concatenate([
        c0 * lom, jnp.dot(c0, shift, preferred_element_type=F32).astype(q.dtype),
        c1 * lom, jnp.dot(c1, shift, preferred_element_type=F32).astype(q.dtype)], axis=0)
    q1 = q0 + fq_ref[...]
    mf = mf_ref[...]
    qa_ref[...] = jnp.concatenate([q1, jnp.concatenate([mf, mf, mf, mf], axis=0)], axis=1)

    rows = NSA_REP * Q_BLOCK
    tkf = s_ref.shape[2]
    tkn = sn_ref.shape[1]
    near_off = pl.multiple_of(jnp.maximum(n - 1, 0) * Q_BLOCK, Q_BLOCK)
    n_full = near_off // tkf
    rem = near_off - n_full * tkf

    half = tkf // 2

    def fold_max(mx, s):
        for i in range(s.shape[1] // LANES):
            mx = jnp.maximum(mx, s[:, i * LANES:(i + 1) * LANES])
        return mx

    def score_tiles(j0, count):
        mx = mx_ref[...]
        for gi in range(count):
            off = pl.multiple_of((j0 + gi) * tkf, tkf)
            for hf in range(2):
                s = lax.dot_general(qa_ref[...], ka_ref[pl.ds(off + hf * half, half), :], _TRANS_B,
                                    preferred_element_type=F32)
                col = hf * half + lax.broadcasted_iota(jnp.int32, (rows, half), 1)
                s = jnp.where(col < near_off - off, s, NEG_INF)
                s_ref[j0 + gi, :, hf * half:(hf + 1) * half] = s
                mx = fold_max(mx, s)
        mx_ref[...] = mx

    def value_tiles(j0, count):
        off = pl.multiple_of(j0 * tkf, tkf)
        p = jnp.concatenate([probs(s_ref[j0 + gi]) for gi in range(count)], axis=1)
        acc_ref[...] += jnp.dot(p, vd_ref[pl.ds(off, count * tkf), :], preferred_element_type=F32)

    def probs(s):
        mb = mx_ref[...]
        return jnp.concatenate([jnp.exp2(s[:, i * LANES:(i + 1) * LANES] - mb)
                                for i in range(s.shape[1] // LANES)], axis=1).astype(CD)

    mx_ref[...] = jnp.full((rows, LANES), M_INIT, F32)
    n_far = n_full + jnp.where(rem > 0, 1, 0)
    _sweep_tiles(score_tiles, n_far)
    sn = lax.dot_general(qa_ref[...], ka_ref[pl.ds(near_off, tkn), :], _TRANS_B, preferred_element_type=F32)
    sn = sn + nb_ref[jnp.minimum(n, 1)]
    sn_ref[...] = sn
    m = jnp.max(fold_max(mx_ref[...], sn), axis=1, keepdims=True)
    mx_ref[...] = jnp.broadcast_to(m, (rows, LANES))

    acc_ref[...] = jnp.dot(probs(sn_ref[...]), vd_ref[pl.ds(near_off, tkn), :], preferred_element_type=F32)
    _sweep_tiles(value_tiles, n_far)
    acc = acc_ref[...]
    o_ref[...] = _unstack_heads(acc[:, :LANES] * (1.0 / acc[:, LANES:LANES + 1]))


def _sel_attn(main3, q_col, k_col, v_col, maskfeat, farq, nearbias, tkf):
    b, t, _ = main3.shape
    assert t % tkf == 0 and tkf % Q_BLOCK == 0
    rows = NSA_REP * Q_BLOCK
    blk = lambda w: pl.BlockSpec((None, Q_BLOCK, w), lambda i, g, n: (i, n, g))
    return pl.pallas_call(
        _sel_kernel,
        out_shape=jax.ShapeDtypeStruct((b, t, GROUP_WIDTH), F32),
        grid=(b, NSA_KV_HEADS, t // Q_BLOCK),
        in_specs=[pl.BlockSpec((None, Q_BLOCK, 2 * LANES), lambda i, g, n: (i, n, q_col + g)),
                  pl.BlockSpec((None, t, LANES), lambda i, g, n: (i, 0, k_col)),
                  pl.BlockSpec((None, t, LANES), lambda i, g, n: (i, 0, v_col)),
                  blk(LANES),
                  pl.BlockSpec((None, rows, LANES), lambda i, g, n: (g, 0, 0)),
                  pl.BlockSpec((None, 2, rows, 2 * Q_BLOCK), lambda i, g, n: (g, 0, 0, 0))],
        out_specs=blk(2 * LANES),
        scratch_shapes=[pltpu.VMEM((t, 2 * LANES), CD), pltpu.VMEM((t, 2 * LANES), CD),
                        pltpu.VMEM((rows, 2 * LANES), CD), pltpu.VMEM((t // tkf, rows, tkf), F32),
                        pltpu.VMEM((rows, 2 * Q_BLOCK), F32),
                        pltpu.VMEM((rows, LANES), F32), pltpu.VMEM((rows, 2 * LANES), F32)],
        compiler_params=_cparams("arbitrary", "arbitrary", "arbitrary"),
        name="nsa_sel_attn",
    )(main3, main3, main3, maskfeat, farq, nearbias)


def _outproj_kernel(mla_ref, fox_ref, cmp_ref, sel_ref, win_ref, swa_ref, gt_ref, ex_ref, gn_ref, w_ref, h_ref,
                    o_ref, u_ref):
    @pl.when(pl.program_id(1) == 0)
    def _():
        gate = jnp.zeros((gt_ref.shape[0], 3 * GROUP_WIDTH), F32)
        for part in _split_parts(jax.nn.sigmoid(gt_ref[...]), 2):
            gate = gate + jnp.dot(part, ex_ref[...], preferred_element_type=F32)
        nsa = (gate[:, :GROUP_WIDTH] * cmp_ref[...] + gate[:, GROUP_WIDTH:2 * GROUP_WIDTH] * sel_ref[...]
               + gate[:, 2 * GROUP_WIDTH:] * win_ref[...])
        for k, x in enumerate((mla_ref[...], fox_ref[...], nsa, swa_ref[...])):
            sl = slice(k * GROUP_WIDTH, (k + 1) * GROUP_WIDTH)
            u_ref[:, sl] = _rms(x, gn_ref[:, sl]).astype(u_ref.dtype)

    o_ref[...] = h_ref[...] + jnp.dot(u_ref[...], w_ref[...], preferred_element_type=F32)


def _gate_expansion():
    ex = np.zeros((LANES, 3 * GROUP_WIDTH), np.float32)
    for head in range(GROUP_HEADS):
        for branch in range(3):
            lo = branch * GROUP_WIDTH + head * HEAD_DIM
            ex[8 + 3 * head + branch, lo:lo + HEAD_DIM] = 1.0
    return jnp.asarray(ex, CD)


def _outproj(o_mla, o_fox, o_cmp, o_sel, o_win, o_swa, small, gn, w, h, tm, tn):
    n, d = h.shape
    mix = N_GROUPS * GROUP_WIDTH
    part = pl.BlockSpec((tm, GROUP_WIDTH), lambda i, j: (i, 0))
    ex = _gate_expansion()
    return pl.pallas_call(
        _outproj_kernel,
        out_shape=jax.ShapeDtypeStruct((n, d), F32),
        grid=(n // tm, d // tn),
        in_specs=[part, part, part, part, part, part,
                  pl.BlockSpec((tm, LANES), lambda i, j: (i, 4)),
                  pl.BlockSpec(ex.shape, lambda i, j: (0, 0)),
                  pl.BlockSpec((1, mix), lambda i, j: (0, 0)),
                  pl.BlockSpec((mix, tn), lambda i, j: (0, j)),
                  pl.BlockSpec((tm, tn), lambda i, j: (i, j))],
        out_specs=pl.BlockSpec((tm, tn), lambda i, j: (i, j)),
        scratch_shapes=[pltpu.VMEM((tm, mix), CD)],
        compiler_params=_cparams("parallel", "arbitrary"),
        name="outproj",
    )(o_mla, o_fox, o_cmp, o_sel, o_win, o_swa, small, ex, gn.reshape(1, mix), w, h)


def _mlp_kernel(*refs, final):
    if final:
        h_ref, g_ref, wu_ref, wd_ref, gf_ref, o_ref, u_ref = refs
    else:
        h_ref, g_ref, wu_ref, wd_ref, o_ref, u_ref = refs
    c = pl.program_id(1)

    @pl.when(c == 0)
    def _():
        x = h_ref[...]
        u_ref[...] = _rms(x, g_ref[...]).astype(u_ref.dtype)
        o_ref[...] = x

    m = jnp.dot(u_ref[...], wu_ref[...], preferred_element_type=F32)
    a = jnp.square(jnp.maximum(m, 0.0)).astype(CD)
    o_ref[...] += jnp.dot(a, wd_ref[...], preferred_element_type=F32)

    if final:
        @pl.when(c == pl.num_programs(1) - 1)
        def _():
            o_ref[...] = _rms(o_ref[...], gf_ref[...])


def _mlp(h, g, wu, wd, gf, tm, tf):
    n, d = h.shape
    dff = wu.shape[1]
    final = gf is not None
    in_specs = [pl.BlockSpec((tm, d), lambda i, c: (i, 0)),
                pl.BlockSpec((1, d), lambda i, c: (0, 0)),
                pl.BlockSpec((d, tf), lambda i, c: (0, c)),
                pl.BlockSpec((tf, d), lambda i, c: (c, 0))]
    args = [h, g.reshape(1, d), wu, wd]
    if final:
        in_specs.append(pl.BlockSpec((1, d), lambda i, c: (0, 0)))
        args.append(gf.reshape(1, d))
    return pl.pallas_call(
        functools.partial(_mlp_kernel, final=final),
        out_shape=jax.ShapeDtypeStruct((n, d), F32),
        grid=(n // tm, dff // tf),
        in_specs=in_specs,
        out_specs=pl.BlockSpec((tm, d), lambda i, c: (i, 0)),
        scratch_shapes=[pltpu.VMEM((tm, d), CD)],
        compiler_params=_cparams("parallel", "arbitrary"),
        name="mlp",
    )(*args)


def _t5_bucket_np(dist):
    max_exact = REL_BUCKETS // 2
    d = np.maximum(dist, 0)
    ratio = np.log(np.maximum(d, 1).astype(np.float32) / np.float32(max_exact)) / np.float32(
        math.log(REL_MAX_DIST / max_exact))
    large = max_exact + (ratio * np.float32(REL_BUCKETS - max_exact)).astype(np.int32)
    large = np.minimum(large, REL_BUCKETS - 1)
    return np.where(d < max_exact, d, large)


def _bias_tables(rel_bias):
    tbl = (rel_bias.astype(F32) * LOG2E).T.reshape(2, NSA_KV_HEADS, NSA_REP, REL_BUCKETS)
    tbl_nsa, tbl_swa = tbl[0], tbl[1]

    def lookup(tb, bucket):
        onehot = (jnp.asarray(bucket, jnp.int32)[None] == jnp.arange(REL_BUCKETS)[:, None, None]).astype(F32)
        return jnp.einsum('grb,bij->grij', tb, onehot, precision=lax.Precision.HIGHEST)
    i = np.arange(Q_BLOCK)[:, None]
    far_bucket = REL_BUCKETS - 1

    def banded(tb, window):
        jk = np.arange(window + Q_BLOCK)[None, :]
        dist = i + window - jk
        ok = (dist >= 0) & (dist < window)
        vals = lookup(tb, _t5_bucket_np(dist))
        vals = jnp.where(jnp.asarray(ok)[None, None], vals, NEG_INF)
        return vals.reshape(NSA_KV_HEADS, NSA_REP * Q_BLOCK, window + Q_BLOCK)

    bm_win = banded(tbl_nsa, NSA_WINDOW)
    bm_swa = banded(tbl_swa, SWA_WINDOW)

    far = tbl_nsa[:, :, far_bucket]
    jk = np.arange(2 * Q_BLOCK)[None, :]
    tiles = []
    for first_key_back in (0, Q_BLOCK):
        dist = i + first_key_back - jk
        delta = lookup(tbl_nsa, _t5_bucket_np(dist)) - far[:, :, None, None]
        tiles.append(jnp.where(jnp.asarray(dist >= 0)[None, None], delta, NEG_INF))
    near = jnp.stack(tiles, axis=1).reshape(NSA_KV_HEADS, 2, NSA_REP * Q_BLOCK, 2 * Q_BLOCK)
    far_rows = jnp.broadcast_to(far[:, :, None], (NSA_KV_HEADS, NSA_REP, Q_BLOCK)).reshape(NSA_KV_HEADS, -1)
    hi, lo_ = _split_parts(far_rows, 2)
    zeros = lambda k: jnp.zeros((NSA_KV_HEADS, NSA_REP * Q_BLOCK, k), CD)
    farq = jnp.concatenate([zeros(HEAD_DIM), hi[..., None], lo_[..., None], zeros(LANES - HEAD_DIM - 2)], axis=-1)

    u = np.arange(16)[None, :]
    dist_c = i - 16 * u + 113
    band = lookup(tbl_nsa, _t5_bucket_np(dist_c))
    band = jnp.where(jnp.asarray(dist_c >= 0)[None, None], band, MASK_NEG)
    band = band.reshape(NSA_KV_HEADS, NSA_REP * Q_BLOCK, 16)
    bh, bl = _split_parts(band, 2)
    neg = jnp.full((NSA_KV_HEADS, NSA_REP * Q_BLOCK, 1), MASK_NEG, CD)
    pq = jnp.concatenate([bh, bl, hi[..., None], lo_[..., None], neg, zeros(LANES - 35)], axis=-1)
    return bm_win, bm_swa, near, farq, pq


def _overlap_matrix(nc_pad, ns):
    ci = np.arange(nc_pad)[:, None]
    sj = np.arange(LANES)[None, :]
    ov = ((ci * NSA_CMP_STRIDE + NSA_CMP_LEN - 1 >= sj * NSA_SEL_LEN)
          & (ci * NSA_CMP_STRIDE <= sj * NSA_SEL_LEN + NSA_SEL_LEN - 1) & (sj < ns))
    return jnp.asarray(ov.astype(np.float32).T, CD)


def _layout_w_in(w):
    w = w.astype(F32)
    sc = HEAD_DIM ** -0.5 * LOG2E
    z = lambda k: jnp.zeros((w.shape[0], k), F32)
    main = jnp.concatenate([w[:, 544:1056] * sc, w[:, 1056:2080], w[:, 2088:2600] * sc, w[:, 3392:3904] * sc,
                            w[:, 2600:3368], w[:, 3904:4160]], axis=1)
    kr = w[:, 512:544]
    blk_a = jnp.concatenate([w[:, 2080:2088], w[:, 3368:3392], z(32), kr, z(32)], axis=1)
    blk_b = jnp.concatenate([z(64), -kr[:, 16:], kr[:, :16], z(32)], axis=1)
    small = jnp.concatenate([w[:, 0:512], blk_a, blk_b], axis=1)
    return main.astype(CD), small.astype(CD)


_FOX_Q, _FOX_K512, _FOX_V = 0, 1, 8
_NSA_Q256, _SWA_Q256 = 6, 8
_KVC_COL = 2560
_KSEL, _VSEL, _KWIN, _VWIN, _KSWA, _VSWA = 22, 23, 24, 25, 26, 27


def _layout_mla(w_uq, w_ukv):
    w3 = w_uq.astype(F32).reshape(MLA_Q_RANK, GROUP_HEADS, MLA_NOPE + MLA_ROPE)
    nope, rp = w3[:, :, :MLA_NOPE], w3[:, :, MLA_NOPE:]
    half = MLA_ROPE // 2
    sw = jnp.concatenate([-rp[:, :, half:], rp[:, :, :half]], axis=-1)
    z = lambda k: jnp.zeros((MLA_Q_RANK, GROUP_HEADS, k), F32)
    plain = jnp.concatenate([nope, rp, z(32)], axis=-1).reshape(MLA_Q_RANK, -1)
    swapped = jnp.concatenate([z(64), sw, z(32)], axis=-1).reshape(MLA_Q_RANK, -1)
    wq = jnp.concatenate([plain, swapped], axis=1)
    k3 = w_ukv.astype(F32).reshape(MLA_KV_RANK, GROUP_HEADS, 2 * HEAD_DIM)
    wk = jnp.concatenate([k3[:, :, :MLA_NOPE], jnp.zeros((MLA_KV_RANK, GROUP_HEADS, HEAD_DIM), F32)],
                         axis=-1).reshape(MLA_KV_RANK, -1)
    wv = k3[:, :, MLA_NOPE:].reshape(MLA_KV_RANK, -1)
    return wq.astype(CD), wk.astype(CD), wv.astype(CD)


def _rope_tables(t):
    inv = ROPE_THETA ** (-jnp.arange(0, MLA_ROPE, 2, dtype=F32) / MLA_ROPE)
    ang = jnp.arange(t).astype(F32)[:, None] * inv[None, :]
    cc = jnp.concatenate([jnp.cos(ang)] * 2, axis=1)
    ss = jnp.concatenate([jnp.sin(ang)] * 2, axis=1)
    one, z64, z32 = jnp.ones((t, 64), F32), jnp.zeros((t, 64), F32), jnp.zeros((t, 32), F32)
    return (jnp.concatenate([one, cc, z32], axis=1), jnp.concatenate([z64, ss, z32], axis=1),
            jnp.concatenate([z64, cc, z32], axis=1))


def kernel(x, norm_attn, w_in, mla_q_norm, mla_w_uq, mla_kv_norm, mla_w_ukv, fox_b_f, nsa_cmp_pos,
           nsa_cmp_w1, nsa_cmp_w2, swa_sinks, group_norm, w_out, norm_mlp, w_up, w_down, rel_bias,
           final_norm):
    b, t, d = x.shape
    n = b * t
    depth = w_in.shape[0]
    assert t % 1024 == 0 and d == N_GROUPS * GROUP_WIDTH
    tl = _TILES
    nr = t // NSA_CMP_STRIDE

    cosq, sinq, cosk = _rope_tables(t)
    bm_win, bm_swa, nearbias, farq, pq = _bias_tables(rel_bias)
    overlap = _overlap_matrix(nr, t // NSA_SEL_LEN)

    h = x.reshape(n, d).astype(F32)
    for l in range(depth):
        w_main, w_small = _layout_w_in(w_in[l])
        main = _norm_matmul(h, norm_attn[l], w_main, CD, tl.proj_rows, tl.proj_cols)
        small = _norm_matmul(h, norm_attn[l], w_small, F32, tl.proj_rows, w_small.shape[1])
        main3 = main.reshape(b, t, -1)
        small3 = small.reshape(b, t, -1)

        wq, wk, wv = _layout_mla(mla_w_uq[l], mla_w_ukv[l])
        q_m, k_m, v_m = _mla_prep(small, mla_q_norm[l], mla_kv_norm[l], wq, wk, wv, cosq, sinq, cosk, t,
                                  tl.prep_rows)
        o_mla = _dense_attn(q_m.reshape(b, t, -1), k_m.reshape(b, t, -1), v_m.reshape(b, t, -1), None,
                            fox=False, q_col=0, v_col=0, tq=tl.sweep)

        k_aug, fq = _fox_prep(small3, fox_b_f[l], main3, _FOX_K512, tl.prep_rows)
        o_fox = _dense_attn(main3, k_aug, main3, fq, fox=True, q_col=_FOX_Q, v_col=_FOX_V, tq=tl.sweep)

        kv16 = main3[:, :, _KVC_COL:_KVC_COL + 256].reshape(b, nr, NSA_CMP_STRIDE * 256)
        cmp = _nsa_compress(kv16, nsa_cmp_pos[l], nsa_cmp_w1[l], nsa_cmp_w2[l])
        o_cmp, maskfeat = _cmp_select(main3, _NSA_Q256, cmp, pq, overlap, tl.cmp_blocks)
        o_win = _banded_attn(main3, _NSA_Q256, _KWIN, _VWIN, bm_win, None, NSA_WINDOW, tl.band_blocks)
        o_sel = _sel_attn(main3, _NSA_Q256, _KSEL, _VSEL, maskfeat, farq, nearbias, tl.sweep)

        sink = (swa_sinks[l].astype(F32) * LOG2E).reshape(NSA_KV_HEADS, NSA_REP, 1, 1)
        sink_rows = jnp.concatenate(
            [jnp.broadcast_to(sink, (NSA_KV_HEADS, NSA_REP, Q_BLOCK, 1)),
             jnp.full((NSA_KV_HEADS, NSA_REP, Q_BLOCK, LANES - 1), NEG_INF, F32)],
            axis=-1).reshape(NSA_KV_HEADS, NSA_REP * Q_BLOCK, LANES)
        o_swa = _banded_attn(main3, _SWA_Q256, _KSWA, _VSWA, bm_swa, sink_rows, SWA_WINDOW, tl.band_blocks)

        parts = [o.reshape(n, GROUP_WIDTH) for o in (o_mla, o_fox, o_cmp, o_sel, o_win, o_swa)]
        h = _outproj(*parts, small, group_norm[l], w_out[l].astype(CD), h, tl.out_rows, d)
        gf = final_norm if l == depth - 1 else None
        h = _mlp(h, norm_mlp[l], w_up[l].astype(CD), w_down[l].astype(CD), gf, tl.mlp_rows, tl.mlp_cols)
    return h.reshape(b, t, d).astype(x.dtype)
```
